```python
import jax, jax.numpy as jnp
from jax import lax
import numpy as np

D_MODEL = 1024
BATCH = 4
SEQ = 8192
DEPTH = 1
DEC_BATCH = 16
DEC_SEQ = 64
PAST_LEN = 1024

CHUNK = 64
Q_BLOCK = 128
D_RNN = D_MODEL
N_RNN_BLOCKS = 8
RNN_BLOCK = D_RNN // N_RNN_BLOCKS
CONV_W = 4
LRU_C = 8.0
N_HEADS = 8
QK_NOPE = 128
ROPE_DIM = 64
V_HEAD = 128
Q_LORA = 384
KV_LORA = 256
ROPE_THETA = 10000.0
SM_SCALE = (QK_NOPE + ROPE_DIM) ** -0.5
N_GROUPS = 4
EXP_PER_GROUP = 4
N_EXPERTS = N_GROUPS * EXP_PER_GROUP
TOP_K = 2
D_EXPERT = 512
EPS = 1e-6
IN_COLS = 2 * D_RNN + Q_LORA + KV_LORA + ROPE_DIM + 2 * D_MODEL
IN_SPLITS = (D_RNN, 2 * D_RNN, 2 * D_RNN + Q_LORA, 2 * D_RNN + Q_LORA + KV_LORA,
             2 * D_RNN + Q_LORA + KV_LORA + ROPE_DIM, 2 * D_RNN + Q_LORA + KV_LORA + ROPE_DIM + D_MODEL)

kernel_name = "hybrid_rglru_mla_hmoe_stream_step"


def rmsnorm(x, g):
    xf = x.astype(jnp.float32)
    y = xf * lax.rsqrt(jnp.mean(xf * xf, axis=-1, keepdims=True) + EPS)
    return y.astype(x.dtype) * g


def rope(x, pos):
    inv = ROPE_THETA ** (-jnp.arange(0, ROPE_DIM, 2, dtype=jnp.float32) / ROPE_DIM)
    ang = pos.astype(jnp.float32)[:, None] * inv
    if x.ndim == 4:
        ang = ang[:, None, :]
    cos, sin = jnp.cos(ang), jnp.sin(ang)
    xf = x.astype(jnp.float32)
    x1, x2 = xf[..., : ROPE_DIM // 2], xf[..., ROPE_DIM // 2:]
    return jnp.concatenate([x1 * cos - x2 * sin, x2 * cos + x1 * sin], axis=-1).astype(x.dtype)


def lru_scan(a, bx, h0):
    bx = bx.at[:, 0].add(a[:, 0] * h0)

    def combine(l, r):
        al, bl = l
        ar, br = r
        return al * ar, ar * bl + br

    _, h = lax.associative_scan(combine, (a, bx), axis=1)
    return h


def _attend(q_nope, q_pe, k_nope, k_pe, v, q_pos, k_pos):
    s = (jnp.einsum('bqhd,bkhd->bhqk', q_nope, k_nope).astype(jnp.float32)
         + jnp.einsum('bqhr,bkr->bhqk', q_pe, k_pe).astype(jnp.float32)) * SM_SCALE
    mask = (k_pos // CHUNK)[None, :] <= (q_pos // CHUNK)[:, None]
    s = jnp.where(mask[None, None], s, -1e30)
    p = jax.nn.softmax(s, axis=-1).astype(v.dtype)
    return jnp.einsum('bhqk,bkhd->bqhd', p, v)


def chunk_causal_attention(q_nope, q_pe, k_nope, k_pe, v, q_pos, k_pos):
    B, S = q_nope.shape[:2]
    if S > Q_BLOCK and S % Q_BLOCK == 0:
        nb = S // Q_BLOCK

        def to_blocks(t):
            return jnp.moveaxis(t.reshape((B, nb, Q_BLOCK) + t.shape[2:]), 1, 0)

        def blk(args):
            qn, qp, qpos = args
            return _attend(qn, qp, k_nope, k_pe, v, qpos, k_pos)

        o = lax.map(blk, (to_blocks(q_nope), to_blocks(q_pe), q_pos.reshape(nb, Q_BLOCK)))
        return jnp.moveaxis(o, 0, 1).reshape(B, S, N_HEADS, V_HEAD)
    return _attend(q_nope, q_pe, k_nope, k_pe, v, q_pos, k_pos)


def token_mixer(h, ckv_past, kpe_past, conv_state, lru_state, p):
    B, S, _ = h.shape
    L = ckv_past.shape[1]
    proj = h @ p['w_in']
    xr, gr, ql, kvl, kr, ga, gb = jnp.split(proj, IN_SPLITS, axis=-1)
    conv_in = jnp.concatenate([conv_state.astype(xr.dtype), xr], axis=1)
    new_conv = conv_in[:, -(CONV_W - 1):]
    xc = p['b_conv'] + conv_in[:, 0:S] * p['w_conv'][0]
    for k in range(1, CONV_W):
        xc = xc + conv_in[:, k:k + S] * p['w_conv'][k]
    xb = xc.reshape(B, S, N_RNN_BLOCKS, RNN_BLOCK)
    r = jax.nn.sigmoid(jnp.einsum('bsnd,nde->bsne', xb, p['w_rgate']).reshape(B, S, D_RNN) + p['b_rgate'])
    i = jax.nn.sigmoid(jnp.einsum('bsnd,nde->bsne', xb, p['w_igate']).reshape(B, S, D_RNN) + p['b_igate'])
    log_a = -LRU_C * r.astype(jnp.float32) * jax.nn.softplus(-p['lru_lambda'].astype(jnp.float32))
    a = jnp.exp(log_a)
    bx = jnp.sqrt(-jnp.expm1(2.0 * log_a)) * (i * xc).astype(jnp.float32)
    hs = lru_scan(a, bx, lru_state.astype(jnp.float32))
    new_lru = hs[:, -1].astype(lru_state.dtype)
    y_a = (hs.astype(h.dtype) * jax.nn.gelu(gr)) @ p['w_rnn_out']
    pos = L + jnp.arange(S)
    q = (rmsnorm(ql, p['g_q_lat']) @ p['w_q_up']).reshape(B, S, N_HEADS, QK_NOPE + ROPE_DIM)
    q_nope, q_pe = q[..., :QK_NOPE], rope(q[..., QK_NOPE:], pos)
    ckv = rmsnorm(kvl, p['g_kv_lat'])
    kpe = rope(kr, pos)
    ckv_all = jnp.concatenate([ckv_past.astype(ckv.dtype), ckv], axis=1)
    kpe_all = jnp.concatenate([kpe_past.astype(kpe.dtype), kpe], axis=1)
    k_nope = (ckv_all @ p['w_k_up']).reshape(B, L + S, N_HEADS, QK_NOPE)
    v = (ckv_all @ p['w_v_up']).reshape(B, L + S, N_HEADS, V_HEAD)
    o = chunk_causal_attention(q_nope, q_pe, k_nope, kpe_all, v, pos, jnp.arange(L + S))
    y_b = o.reshape(B, S, N_HEADS * V_HEAD) @ p['w_attn_out']
    m = jax.nn.sigmoid(ga) * y_a + jax.nn.sigmoid(gb) * y_b
    return m @ p['w_out'], (ckv, kpe, new_conv, new_lru)


def hier_moe(h, p):
    g_logits = (h @ p['w_group'] + p['b_group']).astype(jnp.float32)
    g_w, g_idx = lax.top_k(jax.nn.softmax(g_logits, axis=-1), 1)
    e_logits = (h @ p['w_erouter'] + p['b_erouter']).astype(jnp.float32)
    e_logits = e_logits.reshape(h.shape[:-1] + (N_GROUPS, EXP_PER_GROUP))
    e_sel = jnp.take_along_axis(e_logits, g_idx[..., None], axis=-2)[..., 0, :]
    e_w, e_idx = lax.top_k(jax.nn.softmax(e_sel, axis=-1), TOP_K)
    e_w = e_w / jnp.sum(e_w, axis=-1, keepdims=True)
    w = g_w * e_w
    flat = g_idx * EXP_PER_GROUP + e_idx
    comb = jnp.einsum('bsk,bske->bse', w, jax.nn.one_hot(flat, N_EXPERTS, dtype=jnp.float32)).astype(h.dtype)
    out = jnp.zeros_like(h)
    for e in range(N_EXPERTS):
        hid = jax.nn.silu(h @ p['w_exp_gate'][e]) * (h @ p['w_exp_up'][e])
        out = out + comb[..., e:e + 1] * (hid @ p['w_exp_down'][e])
    return out


def layer(x, c, ckv_past, kpe_past, conv_state, lru_state, p):
    ada = jax.nn.silu(c) @ p['w_ada'] + p['b_ada']
    sh1, sc1, gt1, sh2, sc2, gt2 = [t[:, None, :] for t in jnp.split(ada, 6, axis=-1)]
    h = rmsnorm(x, p['g_pre1']) * (1 + sc1) + sh1
    y, st = token_mixer(h, ckv_past, kpe_past, conv_state, lru_state, p)
    x = x + gt1 * rmsnorm(y, p['g_post1'])
    h = rmsnorm(x, p['g_pre2']) * (1 + sc2) + sh2
    x = x + gt2 * rmsnorm(hier_moe(h, p), p['g_post2'])
    return x, st


def setup_inputs(seed: int = 0) -> dict:
    key = jax.random.key(seed)
    ks = jax.random.split(key, 40)
    f32 = jnp.float32

    def nrm(k, shape, scale=1.0):
        return jax.random.normal(k, shape, f32) * scale

    u = jax.random.uniform(ks[16], (DEPTH, D_RNN), f32, minval=0.9, maxval=0.999)
    a_base = u ** (1.0 / LRU_C)
    lam = jnp.log(a_base) - jnp.log1p(-a_base)
    return {
        'x_prompt': nrm(ks[0], (BATCH, SEQ, D_MODEL)),
        'x_sample': nrm(ks[1], (DEC_BATCH, DEC_SEQ, D_MODEL)),
        'c_prompt': nrm(ks[2], (BATCH, D_MODEL)),
        'c_sample': nrm(ks[3], (DEC_BATCH, D_MODEL)),
        'cache_ckv': nrm(ks[4], (DEPTH, DEC_BATCH, PAST_LEN, KV_LORA)),
        'cache_kpe': nrm(ks[5], (DEPTH, DEC_BATCH, PAST_LEN, ROPE_DIM)),
        'state_conv': nrm(ks[6], (DEPTH, DEC_BATCH, CONV_W - 1, D_RNN)),
        'state_rglru': nrm(ks[7], (DEPTH, DEC_BATCH, D_RNN), 0.3),
        'w_ada': nrm(ks[8], (DEPTH, D_MODEL, 6 * D_MODEL), 0.5 * D_MODEL ** -0.5),
        'b_ada': nrm(ks[9], (DEPTH, 6 * D_MODEL), 0.01),
        'g_pre1': 1.0 + nrm(ks[10], (DEPTH, D_MODEL), 0.01),
        'g_post1': 1.0 + nrm(ks[11], (DEPTH, D_MODEL), 0.01),
        'g_pre2': 1.0 + nrm(ks[12], (DEPTH, D_MODEL), 0.01),
        'g_post2': 1.0 + nrm(ks[13], (DEPTH, D_MODEL), 0.01),
        'w_in': nrm(ks[14], (DEPTH, D_MODEL, IN_COLS), D_MODEL ** -0.5),
        'w_conv': nrm(ks[15], (DEPTH, CONV_W, D_RNN), CONV_W ** -0.5),
        'b_conv': nrm(ks[17], (DEPTH, D_RNN), 0.01),
        'w_rgate': nrm(ks[18], (DEPTH, N_RNN_BLOCKS, RNN_BLOCK, RNN_BLOCK), RNN_BLOCK ** -0.5),
        'b_rgate': nrm(ks[19], (DEPTH, D_RNN), 0.01),
        'w_igate': nrm(ks[20], (DEPTH, N_RNN_BLOCKS, RNN_BLOCK, RNN_BLOCK), RNN_BLOCK ** -0.5),
        'b_igate': nrm(ks[21], (DEPTH, D_RNN), 0.01),
        'lru_lambda': lam,
        'w_rnn_out': nrm(ks[22], (DEPTH, D_RNN, D_MODEL), D_RNN ** -0.5),
        'g_q_lat': 1.0 + nrm(ks[23], (DEPTH, Q_LORA), 0.01),
        'w_q_up': nrm(ks[24], (DEPTH, Q_LORA, N_HEADS * (QK_NOPE + ROPE_DIM)), Q_LORA ** -0.5),
        'g_kv_lat': 1.0 + nrm(ks[25], (DEPTH, KV_LORA), 0.01),
        'w_k_up': nrm(ks[26], (DEPTH, KV_LORA, N_HEADS * QK_NOPE), KV_LORA ** -0.5),
        'w_v_up': nrm(ks[27], (DEPTH, KV_LORA, N_HEADS * V_HEAD), KV_LORA ** -0.5),
        'w_attn_out': nrm(ks[28], (DEPTH, N_HEADS * V_HEAD, D_MODEL), (N_HEADS * V_HEAD) ** -0.5),
        'w_out': nrm(ks[29], (DEPTH, D_MODEL, D_MODEL), D_MODEL ** -0.5),
        'w_group': nrm(ks[30], (DEPTH, D_MODEL, N_GROUPS), D_MODEL ** -0.5),
        'b_group': nrm(ks[31], (DEPTH, N_GROUPS), 0.01),
        'w_erouter': nrm(ks[32], (DEPTH, D_MODEL, N_EXPERTS), D_MODEL ** -0.5),
        'b_erouter': nrm(ks[33], (DEPTH, N_EXPERTS), 0.01),
        'w_exp_gate': nrm(ks[34], (DEPTH, N_EXPERTS, D_MODEL, D_EXPERT), D_MODEL ** -0.5),
        'w_exp_up': nrm(ks[35], (DEPTH, N_EXPERTS, D_MODEL, D_EXPERT), D_MODEL ** -0.5),
        'w_exp_down': nrm(ks[36], (DEPTH, N_EXPERTS, D_EXPERT, D_MODEL), D_EXPERT ** -0.5),
    }


def reference(x_prompt, x_sample, c_prompt, c_sample, cache_ckv, cache_kpe, state_conv, state_rglru,
              w_ada, b_ada, g_pre1, g_post1, g_pre2, g_post2, w_in, w_conv, b_conv, w_rgate, b_rgate,
              w_igate, b_igate, lru_lambda, w_rnn_out, g_q_lat, w_q_up, g_kv_lat, w_k_up, w_v_up,
              w_attn_out, w_out, w_group, b_group, w_erouter, b_erouter, w_exp_gate, w_exp_up, w_exp_down):
    yp, ys = x_prompt, x_sample
    B = x_prompt.shape[0]
    dt = x_prompt.dtype
    ckv_p, kpe_p, conv_p, lru_p = [], [], [], []
    ckv_s, kpe_s, conv_s, lru_s = [], [], [], []
    for l in range(DEPTH):
        p = {
            'w_ada': w_ada[l], 'b_ada': b_ada[l], 'g_pre1': g_pre1[l], 'g_post1': g_post1[l],
            'g_pre2': g_pre2[l], 'g_post2': g_post2[l], 'w_in': w_in[l], 'w_conv': w_conv[l],
            'b_conv': b_conv[l], 'w_rgate': w_rgate[l], 'b_rgate': b_rgate[l], 'w_igate': w_igate[l],
            'b_igate': b_igate[l], 'lru_lambda': lru_lambda[l], 'w_rnn_out': w_rnn_out[l],
            'g_q_lat': g_q_lat[l], 'w_q_up': w_q_up[l], 'g_kv_lat': g_kv_lat[l], 'w_k_up': w_k_up[l],
            'w_v_up': w_v_up[l], 'w_attn_out': w_attn_out[l], 'w_out': w_out[l], 'w_group': w_group[l],
            'b_group': b_group[l], 'w_erouter': w_erouter[l], 'b_erouter': b_erouter[l],
            'w_exp_gate': w_exp_gate[l], 'w_exp_up': w_exp_up[l], 'w_exp_down': w_exp_down[l],
        }
        yp, (a1, a2, a3, a4) = layer(
            yp, c_prompt,
            jnp.zeros((B, 0, KV_LORA), dt), jnp.zeros((B, 0, ROPE_DIM), dt),
            jnp.zeros((B, CONV_W - 1, D_RNN), dt), jnp.zeros((B, D_RNN), dt), p)
        ckv_p.append(a1); kpe_p.append(a2); conv_p.append(a3); lru_p.append(a4)
        ys, (s1, s2, s3, s4) = layer(ys, c_sample, cache_ckv[l], cache_kpe[l], state_conv[l], state_rglru[l], p)
        ckv_s.append(s1); kpe_s.append(s2); conv_s.append(s3); lru_s.append(s4)
    return (yp, ys, jnp.stack(ckv_p), jnp.stack(kpe_p), jnp.stack(conv_p), jnp.stack(lru_p),
            jnp.stack(ckv_s), jnp.stack(kpe_s), jnp.stack(conv_s), jnp.stack(lru_s))
```

```python
import functools

import jax
import jax.numpy as jnp
from jax import lax
from jax.experimental import pallas as pl
from jax.experimental.pallas import tpu as pltpu

F32 = jnp.float32
BF16 = jnp.bfloat16

D_MODEL = 1024
CHUNK = 64
D_RNN = 1024
N_RNN_BLOCKS = 8
RNN_BLOCK = D_RNN // N_RNN_BLOCKS
CONV_W = 4
LRU_C = 8.0
N_HEADS = 8
QK_NOPE = 128
ROPE_DIM = 64
V_HEAD = 128
Q_LORA = 384
KV_LORA = 256
ROPE_THETA = 10000.0
SM_SCALE = (QK_NOPE + ROPE_DIM) ** -0.5
N_GROUPS = 4
EXP_PER_GROUP = 4
N_EXPERTS = N_GROUPS * EXP_PER_GROUP
D_EXPERT = 512
EPS = 1e-6

LANES = 128
HEAD_K = QK_NOPE + 2 * ROPE_DIM
OFF_XR = 0
OFF_GR = OFF_XR + D_RNN
OFF_QL = OFF_GR + D_RNN
OFF_KVL = OFF_QL + Q_LORA
OFF_KR = OFF_KVL + KV_LORA
OFF_GA = OFF_KR + 2 * ROPE_DIM
OFF_GB = OFF_GA + D_MODEL
IN_COLS2 = OFF_GB + D_MODEL
ROUTE_OFF = N_GROUPS
CONV_PAD = 8
NEG = -1e30
VMEM_LIMIT = 56 * 1024 * 1024


def _cparams(sem):
    return pltpu.CompilerParams(dimension_semantics=sem, vmem_limit_bytes=VMEM_LIMIT)


def _const_spec(shape):
    n = len(shape)
    return pl.BlockSpec(shape, lambda *_: (0,) * n)


def _rms(x, g):
    return x * lax.rsqrt(jnp.mean(x * x, axis=-1, keepdims=True) + EPS) * g


def _dot(a, b):
    return jnp.dot(a, b, preferred_element_type=F32)


def _rope(v, c, s):
    return v * c + pltpu.roll(v, ROPE_DIM, axis=1) * s


def _ada_kernel(c_ref, w_ref, b_ref, o_ref):
    c = c_ref[...]
    s = c * jax.nn.sigmoid(c)
    o_ref[...] = jnp.dot(s, w_ref[...], preferred_element_type=F32,
                         precision=lax.Precision.HIGHEST) + b_ref[...]


def _ada(c, w, b):
    nb = c.shape[0]
    n = w.shape[1]
    bn = n // 6
    return pl.pallas_call(
        _ada_kernel,
        grid=(n // bn,),
        in_specs=[_const_spec((nb, D_MODEL)),
                  pl.BlockSpec((D_MODEL, bn), lambda j: (0, j)),
                  pl.BlockSpec((1, bn), lambda j: (0, j))],
        out_specs=pl.BlockSpec((nb, bn), lambda j: (0, j)),
        out_shape=jax.ShapeDtypeStruct((nb, n), F32),
        compiler_params=_cparams(("arbitrary",)),
        name="ada",
    )(c, w, b.reshape(1, n))


def _mixer_in_kernel(x_ref, sh_ref, sc_ref, gpre_ref, win_ref, cst_ref, lst_ref, wconv_ref, bconv_ref,
                     wgate_ref, br_ref, bi_ref, lam_ref, wrnn_ref, gq_ref, wqup_ref, gkv_ref,
                     rc_ref, rs_ref,
                     ma_ref, sgb_ref, q_ref, ckv_ref, kpe_ref, kpe128_ref, cout_ref, lout_ref,
                     xbuf, a_scr, b_scr, hcar, *, T):
    t = pl.program_id(1)

    @pl.when(t == 0)
    def _():
        xbuf[CONV_PAD - (CONV_W - 1):CONV_PAD, :] = cst_ref[0]
        hcar[...] = lst_ref[0]

    x = x_ref[0]
    h = _rms(x, gpre_ref[...]) * (1.0 + sc_ref[0]) + sh_ref[0]
    hb = h.astype(BF16)

    xbuf[CONV_PAD:CONV_PAD + T, :] = _dot(hb, win_ref[:, OFF_XR:OFF_XR + D_RNN])
    xc = bconv_ref[...]
    for k in range(CONV_W):
        lo = CONV_PAD - (CONV_W - 1) + k
        xc = xc + xbuf[lo:lo + T, :] * wconv_ref[k:k + 1, :]
    tail = xbuf[T + CONV_PAD - (CONV_W - 1):T + CONV_PAD, :]
    cout_ref[0] = tail
    xbuf[CONV_PAD - (CONV_W - 1):CONV_PAD, :] = tail

    lam = lam_ref[...]
    softplus_neg_lam = jnp.maximum(-lam, 0.0) + jnp.log1p(jnp.exp(-jnp.abs(lam)))
    for n in range(N_RNN_BLOCKS):
        blk = slice(n * RNN_BLOCK, (n + 1) * RNN_BLOCK)
        xcb = xc[:, blk]
        g = _dot(xcb.astype(BF16), wgate_ref[n])
        r = jax.nn.sigmoid(g[:, :RNN_BLOCK] + br_ref[:, blk])
        i = jax.nn.sigmoid(g[:, RNN_BLOCK:] + bi_ref[:, blk])
        log_a = -LRU_C * r * softplus_neg_lam[:, blk]
        a = jnp.exp(log_a)
        a_scr[:, blk] = a
        b_scr[:, blk] = jnp.sqrt(-jnp.tanh(log_a) * (a * a + 1.0)) * (i * xcb)

    def row(s, hprev):
        hnew = a_scr[pl.ds(s, 1), :] * hprev + b_scr[pl.ds(s, 1), :]
        b_scr[pl.ds(s, 1), :] = hnew
        return hnew

    hlast = lax.fori_loop(0, T, row, hcar[...], unroll=8)
    hcar[...] = hlast
    lout_ref[0] = hlast

    gr = _dot(hb, win_ref[:, OFF_GR:OFF_GR + D_RNN])
    y_a = _dot((b_scr[...] * jax.nn.gelu(gr)).astype(BF16), wrnn_ref[...])
    ga = _dot(hb, win_ref[:, OFF_GA:OFF_GA + D_MODEL])
    ma_ref[0] = (jax.nn.sigmoid(ga) * y_a).astype(BF16)
    gb = _dot(hb, win_ref[:, OFF_GB:OFF_GB + D_MODEL])
    sgb_ref[0] = jax.nn.sigmoid(gb).astype(BF16)

    rc = rc_ref[...]
    rs = rs_ref[...]
    ql = _dot(hb, win_ref[:, OFF_QL:OFF_QL + Q_LORA])
    q = _dot(_rms(ql, gq_ref[...]).astype(BF16), wqup_ref[...])
    for hd in range(N_HEADS):
        base = hd * HEAD_K
        q_ref[0, :, base:base + QK_NOPE] = (q[:, base:base + QK_NOPE] * SM_SCALE).astype(BF16)
        pe = _rope(q[:, base + QK_NOPE:base + HEAD_K], rc, rs)
        q_ref[0, :, base + QK_NOPE:base + HEAD_K] = (pe * SM_SCALE).astype(BF16)
    kvl = _dot(hb, win_ref[:, OFF_KVL:OFF_KVL + KV_LORA])
    ckv_ref[0] = _rms(kvl, gkv_ref[...])
    kp = _rope(_dot(hb, win_ref[:, OFF_KR:OFF_KR + 2 * ROPE_DIM]), rc, rs)
    kpe_ref[0] = kp[:, :ROPE_DIM]
    kpe128_ref[0] = kp.astype(BF16)


def _mixer_in(x, sh1, sc1, conv_state, lru_state, rope_c, rope_s, W, T):
    B, S, _ = x.shape
    nt = S // T
    kern = functools.partial(_mixer_in_kernel, T=T)
    bt = lambda b, t: (b, t, 0)
    bo = lambda b, t: (b, 0, 0)
    tt = lambda b, t: (t, 0)
    in_specs = [
        pl.BlockSpec((1, T, D_MODEL), bt),
        pl.BlockSpec((1, 1, D_MODEL), bo),
        pl.BlockSpec((1, 1, D_MODEL), bo),
        _const_spec((1, D_MODEL)),
        _const_spec((D_MODEL, IN_COLS2)),
        pl.BlockSpec((1, CONV_W - 1, D_RNN), bo),
        pl.BlockSpec((1, 1, D_RNN), bo),
        _const_spec((CONV_W, D_RNN)),
        _const_spec((1, D_RNN)),
        _const_spec((N_RNN_BLOCKS, RNN_BLOCK, 2 * RNN_BLOCK)),
        _const_spec((1, D_RNN)),
        _const_spec((1, D_RNN)),
        _const_spec((1, D_RNN)),
        _const_spec((D_RNN, D_MODEL)),
        _const_spec((1, Q_LORA)),
        _const_spec((Q_LORA, N_HEADS * HEAD_K)),
        _const_spec((1, KV_LORA)),
        pl.BlockSpec((T, LANES), tt),
        pl.BlockSpec((T, LANES), tt),
    ]
    out_specs = [
        pl.BlockSpec((1, T, D_MODEL), bt),
        pl.BlockSpec((1, T, D_MODEL), bt),
        pl.BlockSpec((1, T, N_HEADS * HEAD_K), bt),
        pl.BlockSpec((1, T, KV_LORA), bt),
        pl.BlockSpec((1, T, ROPE_DIM), bt),
        pl.BlockSpec((1, T, LANES), bt),
        pl.BlockSpec((1, CONV_W - 1, D_RNN), bo),
        pl.BlockSpec((1, 1, D_RNN), bo),
    ]
    out_shape = [
        jax.ShapeDtypeStruct((B, S, D_MODEL), BF16),
        jax.ShapeDtypeStruct((B, S, D_MODEL), BF16),
        jax.ShapeDtypeStruct((B, S, N_HEADS * HEAD_K), BF16),
        jax.ShapeDtypeStruct((B, S, KV_LORA), F32),
        jax.ShapeDtypeStruct((B, S, ROPE_DIM), F32),
        jax.ShapeDtypeStruct((B, S, LANES), BF16),
        jax.ShapeDtypeStruct((B, CONV_W - 1, D_RNN), F32),
        jax.ShapeDtypeStruct((B, 1, D_RNN), F32),
    ]
    scratch = [
        pltpu.VMEM((T + CONV_PAD, D_RNN), F32),
        pltpu.VMEM((T, D_RNN), F32),
        pltpu.VMEM((T, D_RNN), F32),
        pltpu.VMEM((1, D_RNN), F32),
    ]
    return pl.pallas_call(
        kern, grid=(B, nt), in_specs=in_specs, out_specs=out_specs, out_shape=out_shape,
        scratch_shapes=scratch, compiler_params=_cparams(("arbitrary", "arbitrary")),
        name="mixer_in",
    )(x, sh1, sc1, W["g_pre1"], W["w_in2"], conv_state, lru_state, W["w_conv"], W["b_conv"],
      W["w_gates"], W["b_rgate"], W["b_igate"], W["lam"], W["w_rnn_out"], W["g_q"], W["w_qup"],
      W["g_kv"], rope_c, rope_s)


def _kv_up_kernel(ckv_ref, kpe_ref, wk_ref, wv_ref, k_ref, v_ref):
    c = ckv_ref[...].astype(BF16)
    kn = _dot(c, wk_ref[...])
    kpe = kpe_ref[...]
    for hd in range(N_HEADS):
        base = hd * HEAD_K
        k_ref[:, base:base + QK_NOPE] = kn[:, hd * QK_NOPE:(hd + 1) * QK_NOPE].astype(BF16)
        k_ref[:, base + QK_NOPE:base + HEAD_K] = kpe
    v_ref[...] = _dot(c, wv_ref[...]).astype(BF16)


def _kv_up(ckv, kpe128, W, T):
    R = ckv.shape[0]
    row = lambda i: (i, 0)
    return pl.pallas_call(
        _kv_up_kernel, grid=(R // T,),
        in_specs=[pl.BlockSpec((T, KV_LORA), row), pl.BlockSpec((T, LANES), row),
                  _const_spec((KV_LORA, N_HEADS * QK_NOPE)), _const_spec((KV_LORA, N_HEADS * V_HEAD))],
        out_specs=[pl.BlockSpec((T, N_HEADS * HEAD_K), row), pl.BlockSpec((T, N_HEADS * V_HEAD), row)],
        out_shape=[jax.ShapeDtypeStruct((R, N_HEADS * HEAD_K), BF16),
                   jax.ShapeDtypeStruct((R, N_HEADS * V_HEAD), BF16)],
        compiler_params=_cparams(("arbitrary",)),
        name="kv_up",
    )(ckv, kpe128, W["w_k_up"], W["w_v_up"])


def _attn_prompt_kernel(q_ref, k_ref, v_ref, o_ref, m_scr, l_scr, acc_scr, *, QB):
    qi = pl.program_id(2)
    q = q_ref[0]
    m_scr[...] = jnp.full(m_scr.shape, NEG, F32)
    l_scr[...] = jnp.zeros(l_scr.shape, F32)
    acc_scr[...] = jnp.zeros(acc_scr.shape, F32)

    def step(j, masked):
        start = pl.multiple_of(j * QB, QB)
        k = k_ref[0, pl.ds(start, QB), :]
        s = lax.dot_general(q, k, (((1,), (1,)), ((), ())), preferred_element_type=F32)
        if masked:
            rq = lax.broadcasted_iota(jnp.int32, (QB, QB), 0) // CHUNK
            ck = lax.broadcasted_iota(jnp.int32, (QB, QB), 1) // CHUNK
            s = jnp.where(ck <= rq, s, NEG)
        m_old = m_scr[...]
        m_new = jnp.maximum(m_old, jnp.max(s, axis=1, keepdims=True))
        p = jnp.exp(s - m_new)
        alpha = jnp.exp(m_old - m_new)
        l_scr[...] = alpha * l_scr[...] + jnp.sum(p, axis=1, keepdims=True)
        acc_scr[...] = alpha * acc_scr[...] + _dot(p.astype(BF16), v_ref[0, pl.ds(start, QB), :])
        m_scr[...] = m_new

    def body(j, c):
        step(j, False)
        return c

    lax.fori_loop(0, qi, body, 0)
    step(qi, True)
    o_ref[0] = (acc_scr[...] / l_scr[...]).astype(BF16)


def _attn_prompt(q, k, v, QB):
    B, S, _ = q.shape
    kern = functools.partial(_attn_prompt_kernel, QB=QB)
    return pl.pallas_call(
        kern, grid=(B, N_HEADS, S // QB),
        in_specs=[pl.BlockSpec((1, QB, HEAD_K), lambda b, h, i: (b, i, h)),
                  pl.BlockSpec((1, S, HEAD_K), lambda b, h, i: (b, 0, h)),
                  pl.BlockSpec((1, S, V_HEAD), lambda b, h, i: (b, 0, h))],
        out_specs=pl.BlockSpec((1, QB, V_HEAD), lambda b, h, i: (b, i, h)),
        out_shape=jax.ShapeDtypeStruct((B, S, N_HEADS * V_HEAD), BF16),
        scratch_shapes=[pltpu.VMEM((QB, 1), F32), pltpu.VMEM((QB, 1), F32), pltpu.VMEM((QB, V_HEAD), F32)],
        compiler_params=_cparams(("arbitrary", "arbitrary", "arbitrary")),
        name="attn_prompt",
    )(q, k, v)


def _attn_sample_kernel(q_ref, k_ref, v_ref, o_ref):
    for hd in range(N_HEADS):
        q = q_ref[0, :, hd * HEAD_K:(hd + 1) * HEAD_K]
        k = k_ref[0, :, hd * HEAD_K:(hd + 1) * HEAD_K]
        s = lax.dot_general(q, k, (((1,), (1,)), ((), ())), preferred_element_type=F32)
        p = jnp.exp(s - jnp.max(s, axis=1, keepdims=True))
        l = jnp.sum(p, axis=1, keepdims=True)
        o = _dot(p.astype(BF16), v_ref[0, :, hd * V_HEAD:(hd + 1) * V_HEAD])
        o_ref[0, :, hd * V_HEAD:(hd + 1) * V_HEAD] = (o / l).astype(BF16)


def _attn_sample(q, k, v):
    B, S, _ = q.shape
    LK = k.shape[1]
    b3 = lambda b: (b, 0, 0)
    return pl.pallas_call(
        _attn_sample_kernel, grid=(B,),
        in_specs=[pl.BlockSpec((1, S, N_HEADS * HEAD_K), b3),
                  pl.BlockSpec((1, LK, N_HEADS * HEAD_K), b3),
                  pl.BlockSpec((1, LK, N_HEADS * V_HEAD), b3)],
        out_specs=pl.BlockSpec((1, S, N_HEADS * V_HEAD), b3),
        out_shape=jax.ShapeDtypeStruct((B, S, N_HEADS * V_HEAD), BF16),
        compiler_params=_cparams(("arbitrary",)),
        name="attn_sample",
    )(q, k, v)


def _lane_min_where(mask, lane):
    return jnp.min(jnp.where(mask, lane, LANES), axis=1, keepdims=True)


def _merge_kernel(o_ref, ma_ref, sgb_ref, x_ref, gt1_ref, sh2_ref, sc2_ref, gpost1_ref, gpre2_ref,
                  wao_ref, wout_ref, wr_ref, br_ref, x1_ref, h2_ref, comb_ref, *, T):
    nc = T // CHUNK
    y_b = _dot(o_ref[...], wao_ref[...])
    m = ma_ref[...].astype(F32) + sgb_ref[...].astype(F32) * y_b
    y = _dot(m.astype(BF16), wout_ref[...])
    yn = _rms(y, gpost1_ref[...]).reshape(nc, CHUNK, D_MODEL)
    x1 = x_ref[...].reshape(nc, CHUNK, D_MODEL) + gt1_ref[...] * yn
    x1_ref[...] = x1.reshape(T, D_MODEL)
    h2 = (_rms(x1, gpre2_ref[...]) * (1.0 + sc2_ref[...]) + sh2_ref[...]).reshape(T, D_MODEL)
    h2b = h2.astype(BF16)
    h2_ref[...] = h2b

    logits = _dot(h2b, wr_ref[...]) + br_ref[...]
    lane = lax.broadcasted_iota(jnp.int32, (T, LANES), 1)
    is_g = lane < N_GROUPS
    gmax = jnp.max(jnp.where(is_g, logits, NEG), axis=1, keepdims=True)
    gidx = _lane_min_where(is_g & (logits == gmax), lane)
    gsum = jnp.sum(jnp.where(is_g, jnp.exp(logits - gmax), 0.0), axis=1, keepdims=True)
    lo = ROUTE_OFF + EXP_PER_GROUP * gidx
    sel = (lane >= lo) & (lane < lo + EXP_PER_GROUP)
    m1 = jnp.max(jnp.where(sel, logits, NEG), axis=1, keepdims=True)
    i1 = _lane_min_where(sel & (logits == m1), lane)
    rest = sel & (lane != i1)
    m2 = jnp.max(jnp.where(rest, logits, NEG), axis=1, keepdims=True)
    i2 = _lane_min_where(rest & (logits == m2), lane)
    e2 = jnp.exp(m2 - m1)
    w1 = 1.0 / (gsum * (1.0 + e2))
    w2 = w1 * e2
    comb_ref[...] = jnp.where(lane == i1, w1, 0.0) + jnp.where(lane == i2, w2, 0.0)


def _merge(o, ma, sgb, x, gt1c, sh2c, sc2c, W, T):
    N = x.shape[0]
    nc = T // CHUNK
    row = lambda i: (i, 0)
    chunk = lambda i: (i, 0, 0)
    kern = functools.partial(_merge_kernel, T=T)
    return pl.pallas_call(
        kern, grid=(N // T,),
        in_specs=[pl.BlockSpec((T, D_MODEL), row), pl.BlockSpec((T, D_MODEL), row),
                  pl.BlockSpec((T, D_MODEL), row), pl.BlockSpec((T, D_MODEL), row),
                  pl.BlockSpec((nc, 1, D_MODEL), chunk), pl.BlockSpec((nc, 1, D_MODEL), chunk),
                  pl.BlockSpec((nc, 1, D_MODEL), chunk),
                  _const_spec((1, D_MODEL)), _const_spec((1, D_MODEL)),
                  _const_spec((D_MODEL, D_MODEL)), _const_spec((D_MODEL, D_MODEL)),
                  _const_spec((D_MODEL, LANES)), _const_spec((1, LANES))],
        out_specs=[pl.BlockSpec((T, D_MODEL), row), pl.BlockSpec((T, D_MODEL), row),
                   pl.BlockSpec((T, LANES), row)],
        out_shape=[jax.ShapeDtypeStruct((N, D_MODEL), F32), jax.ShapeDtypeStruct((N, D_MODEL), BF16),
                   jax.ShapeDtypeStruct((N, LANES), F32)],
        compiler_params=_cparams(("arbitrary",)),
        name="merge",
    )(o, ma, sgb, x, gt1c, sh2c, sc2c, W["g_post1"], W["g_pre2"], W["w_attn_out"], W["w_out"],
      W["w_route"], W["b_route"])


def _moe_kernel(h_ref, comb_ref, x1_ref, gt2_ref, gpost2_ref, wg_ref, wu_ref, wd_ref, y_ref, acc, *, T):
    e = pl.program_id(1)
    nc = T // CHUNK

    @pl.when(e == 0)
    def _():
        acc[...] = jnp.zeros(acc.shape, F32)

    h = h_ref[...]
    g = _dot(h, wg_ref[0])
    u = _dot(h, wu_ref[0])
    hid = (g * jax.nn.sigmoid(g) * u).astype(BF16)
    lane = lax.broadcasted_iota(jnp.int32, (T, LANES), 1)
    ce = jnp.sum(jnp.where(lane == e + ROUTE_OFF, comb_ref[...], 0.0), axis=1, keepdims=True)
    acc[...] += ce * _dot(hid, wd_ref[0])

    @pl.when(e == N_EXPERTS - 1)
    def _():
        on = _rms(acc[...], gpost2_ref[...]).reshape(nc, CHUNK, D_MODEL)
        y = x1_ref[...].reshape(nc, CHUNK, D_MODEL) + gt2_ref[...] * on
        y_ref[...] = y.reshape(T, D_MODEL)


def _moe(h2, comb, x1, gt2c, W, T):
    N = h2.shape[0]
    nc = T // CHUNK
    row = lambda i, e: (i, 0)
    kern = functools.partial(_moe_kernel, T=T)
    return pl.pallas_call(
        kern, grid=(N // T, N_EXPERTS),
        in_specs=[pl.BlockSpec((T, D_MODEL), row), pl.BlockSpec((T, LANES), row),
                  pl.BlockSpec((T, D_MODEL), row),
                  pl.BlockSpec((nc, 1, D_MODEL), lambda i, e: (i, 0, 0)),
                  _const_spec((1, D_MODEL)),
                  pl.BlockSpec((1, D_MODEL, D_EXPERT), lambda i, e: (e, 0, 0)),
                  pl.BlockSpec((1, D_MODEL, D_EXPERT), lambda i, e: (e, 0, 0)),
                  pl.BlockSpec((1, D_EXPERT, D_MODEL), lambda i, e: (e, 0, 0))],
        out_specs=pl.BlockSpec((T, D_MODEL), row),
        out_shape=jax.ShapeDtypeStruct((N, D_MODEL), F32),
        scratch_shapes=[pltpu.VMEM((T, D_MODEL), F32)],
        compiler_params=_cparams(("arbitrary", "arbitrary")),
        name="moe",
    )(h2, comb, x1, gt2c, W["g_post2"], W["w_exp_gate"], W["w_exp_up"], W["w_exp_down"])


def _rotate_half_cols(w):
    half = ROPE_DIM // 2
    return jnp.concatenate([-w[..., half:], w[..., :half]], axis=-1)


def _rope_tables(pos):
    inv = ROPE_THETA ** (-jnp.arange(0, ROPE_DIM, 2, dtype=F32) / ROPE_DIM)
    ang = pos.astype(F32)[:, None] * inv
    z = jnp.zeros((pos.shape[0], LANES - ROPE_DIM), F32)
    c, s = jnp.cos(ang), jnp.sin(ang)
    return jnp.concatenate([c, c, z], axis=1), jnp.concatenate([s, s, z], axis=1)


def _chunk_rows(v, seq):
    B, D = v.shape
    return jnp.broadcast_to(v[:, None, None, :], (B, seq // CHUNK, 1, D)).reshape(B * (seq // CHUNK), 1, D)


def _layer(x, ada, ckv_past, kpe_past, conv_state, lru_state, pos0, W, T_in, T_tok):
    B, S, _ = x.shape
    L = 0 if ckv_past is None else ckv_past.shape[1]
    sh1, sc1, gt1, sh2, sc2, gt2 = jnp.split(ada, 6, axis=-1)
    rope_c, rope_s = _rope_tables(pos0 + jnp.arange(S))
    ma, sgb, q, ckv, kpe, kpe128, conv_new, lru_new = _mixer_in(
        x, sh1[:, None, :], sc1[:, None, :], conv_state, lru_state[:, None, :], rope_c, rope_s, W, T_in)

    if L == 0:
        k, v = _kv_up(ckv.reshape(B * S, KV_LORA), kpe128.reshape(B * S, LANES), W, 256)
        o = _attn_prompt(q, k.reshape(B, S, -1), v.reshape(B, S, -1), 256)
    else:
        ckv_all = jnp.concatenate([ckv_past, ckv], axis=1)
        kpe_past128 = jnp.pad(kpe_past, ((0, 0), (0, 0), (0, LANES - ROPE_DIM))).astype(BF16)
        kpe_all = jnp.concatenate([kpe_past128, kpe128], axis=1)
        LK = L + S
        k, v = _kv_up(ckv_all.reshape(B * LK, KV_LORA), kpe_all.reshape(B * LK, LANES), W, 256)
        o = _attn_sample(q, k.reshape(B, LK, -1), v.reshape(B, LK, -1))

    N = B * S
    x1, h2, comb = _merge(o.reshape(N, D_MODEL), ma.reshape(N, D_MODEL), sgb.reshape(N, D_MODEL),
                          x.reshape(N, D_MODEL), _chunk_rows(gt1, S), _chunk_rows(sh2, S),
                          _chunk_rows(sc2, S), W, T_tok)
    y = _moe(h2, comb, x1, _chunk_rows(gt2, S), W, T_tok)
    return y.reshape(B, S, D_MODEL), ckv, kpe, conv_new, lru_new.reshape(B, D_RNN)


def kernel(x_prompt, x_sample, c_prompt, c_sample, cache_ckv, cache_kpe, state_conv, state_rglru, w_ada, b_ada, g_pre1, g_post1, g_pre2, g_post2, w_in, w_conv, b_conv, w_rgate, b_rgate, w_igate, b_igate, lru_lambda, w_rnn_out, g_q_lat, w_q_up, g_kv_lat, w_k_up, w_v_up, w_attn_out, w_out, w_group, b_group, w_erouter, b_erouter, w_exp_gate, w_exp_up, w_exp_down):
    assert w_in.shape[0] == 1, "single-layer trunk"
    B = x_prompt.shape[0]
    wi = w_in[0]
    sp = lambda a, b: wi[:, a:b]
    xr, gr = sp(0, D_RNN), sp(D_RNN, 2 * D_RNN)
    o = 2 * D_RNN
    ql, kvl, kr = sp(o, o + Q_LORA), sp(o + Q_LORA, o + Q_LORA + KV_LORA), \
        sp(o + Q_LORA + KV_LORA, o + Q_LORA + KV_LORA + ROPE_DIM)
    o = o + Q_LORA + KV_LORA + ROPE_DIM
    ga, gb = sp(o, o + D_MODEL), sp(o + D_MODEL, o + 2 * D_MODEL)
    wq = w_q_up[0].reshape(Q_LORA, N_HEADS, QK_NOPE + ROPE_DIM)
    wq_pe = wq[..., QK_NOPE:]
    row = lambda a: a[0].reshape(1, -1)
    W = {
        "g_pre1": row(g_pre1), "g_post1": row(g_post1), "g_pre2": row(g_pre2), "g_post2": row(g_post2),
        "w_in2": jnp.concatenate([xr, gr, ql, kvl, kr, _rotate_half_cols(kr), ga, gb], axis=1).astype(BF16),
        "w_conv": w_conv[0], "b_conv": row(b_conv),
        "w_gates": jnp.concatenate([w_rgate[0], w_igate[0]], axis=-1).astype(BF16),
        "b_rgate": row(b_rgate), "b_igate": row(b_igate), "lam": row(lru_lambda),
        "w_rnn_out": w_rnn_out[0].astype(BF16),
        "g_q": row(g_q_lat), "g_kv": row(g_kv_lat),
        "w_qup": jnp.concatenate([wq[..., :QK_NOPE], wq_pe, _rotate_half_cols(wq_pe)], axis=-1)
                 .reshape(Q_LORA, N_HEADS * HEAD_K).astype(BF16),
        "w_k_up": w_k_up[0].astype(BF16), "w_v_up": w_v_up[0].astype(BF16),
        "w_attn_out": w_attn_out[0].astype(BF16), "w_out": w_out[0].astype(BF16),
        "w_route": jnp.pad(jnp.concatenate([w_group[0], w_erouter[0]], axis=1),
                           ((0, 0), (0, LANES - N_GROUPS - N_EXPERTS))).astype(BF16),
        "b_route": jnp.pad(jnp.concatenate([b_group[0], b_erouter[0]]), (0, LANES - N_GROUPS - N_EXPERTS))
                   .reshape(1, LANES),
        "w_exp_gate": w_exp_gate[0].astype(BF16), "w_exp_up": w_exp_up[0].astype(BF16),
        "w_exp_down": w_exp_down[0].astype(BF16),
    }
    ada = _ada(jnp.concatenate([c_prompt, c_sample], axis=0), w_ada[0], b_ada[0])
    zeros_conv = jnp.zeros((B, CONV_W - 1, D_RNN), F32)
    zeros_lru = jnp.zeros((B, D_RNN), F32)
    yp, ckv_p, kpe_p, conv_p, lru_p = _layer(x_prompt, ada[:B], None, None, zeros_conv, zeros_lru, 0, W, 256, 512)
    ys, ckv_s, kpe_s, conv_s, lru_s = _layer(x_sample, ada[B:], cache_ckv[0], cache_kpe[0], state_conv[0],
                                             state_rglru[0], cache_ckv.shape[2], W, 64, 512)
    return (yp, ys, ckv_p[None], kpe_p[None], conv_p[None], lru_p[None],
            ckv_s[None], kpe_s[None], conv_s[None], lru_s[None])
```

```python
import functools

import jax
import jax.numpy as jnp
from jax import lax
from jax.experimental import pallas as pl
from jax.experimental.pallas import tpu as pltpu

F32 = jnp.float32
BF16 = jnp.bfloat16

D_MODEL = 1024
CHUNK = 64
D_RNN = 1024
N_RNN_BLOCKS = 8
RNN_BLOCK = D_RNN // N_RNN_BLOCKS
CONV_W = 4
LRU_C = 8.0
N_HEADS = 8
QK_NOPE = 128
ROPE_DIM = 64
V_HEAD = 128
Q_LORA = 384
KV_LORA = 256
ROPE_THETA = 10000.0
SM_SCALE = (QK_NOPE + ROPE_DIM) ** -0.5
LOG2E = 1.4426950408889634
Q_SCALE = SM_SCALE * LOG2E
N_GROUPS = 4
EXP_PER_GROUP = 4
N_EXPERTS = N_GROUPS * EXP_PER_GROUP
D_EXPERT = 512
EPS = 1e-6

LANES = 128
HEAD_K = QK_NOPE + 2 * ROPE_DIM
OFF_XR = 0
OFF_GR = OFF_XR + D_RNN
OFF_QL = OFF_GR + D_RNN
OFF_KVL = OFF_QL + Q_LORA
OFF_KR = OFF_KVL + KV_LORA
OFF_GA = OFF_KR + 2 * ROPE_DIM
OFF_GB = OFF_GA + D_MODEL
IN_COLS2 = OFF_GB + D_MODEL
ROUTE_OFF = N_GROUPS
CONV_PAD = 8
NEG = -1e30
ATTN_BLOCK = 512
ATTN_HEADS_PER_STEP = 2
VMEM_LIMIT = 56 * 1024 * 1024


def _cparams(sem):
    return pltpu.CompilerParams(dimension_semantics=sem, vmem_limit_bytes=VMEM_LIMIT)


def _const_spec(shape):
    n = len(shape)
    return pl.BlockSpec(shape, lambda *_: (0,) * n)


def _rms(x, g):
    return x * lax.rsqrt(jnp.mean(x * x, axis=-1, keepdims=True) + EPS) * g


def _dot(a, b):
    return jnp.dot(a, b, preferred_element_type=F32)


def _rope(v, c, s):
    return v * c + pltpu.roll(v, ROPE_DIM, axis=1) * s


def _ada_kernel(c_ref, w_ref, b_ref, o_ref):
    c = c_ref[...]
    s = c * jax.nn.sigmoid(c)
    o_ref[...] = jnp.dot(s, w_ref[...], preferred_element_type=F32,
                         precision=lax.Precision.HIGHEST) + b_ref[...]


def _ada(c, w, b):
    nb = c.shape[0]
    n = w.shape[1]
    bn = n // 6
    return pl.pallas_call(
        _ada_kernel,
        grid=(n // bn,),
        in_specs=[_const_spec((nb, D_MODEL)),
                  pl.BlockSpec((D_MODEL, bn), lambda j: (0, j)),
                  pl.BlockSpec((1, bn), lambda j: (0, j))],
        out_specs=pl.BlockSpec((nb, bn), lambda j: (0, j)),
        out_shape=jax.ShapeDtypeStruct((nb, n), F32),
        compiler_params=_cparams(("arbitrary",)),
        name="ada",
    )(c, w, b.reshape(1, n))


def _mixer_in_kernel(x_ref, sh_ref, sc_ref, gpre_ref, win_ref, cst_ref, lst_ref, wconv_ref, bconv_ref,
                     wgate_ref, br_ref, bi_ref, lam_ref, wrnn_ref, gq_ref, wqup_ref, gkv_ref,
                     rc_ref, rs_ref,
                     ma_ref, sgb_ref, q_ref, ckv_ref, kpe_ref, kpe128_ref, cout_ref, lout_ref,
                     xbuf, a_scr, b_scr, hcar, *, T):
    t = pl.program_id(1)

    @pl.when(t == 0)
    def _():
        xbuf[CONV_PAD - (CONV_W - 1):CONV_PAD, :] = cst_ref[0]
        hcar[...] = lst_ref[0]

    x = x_ref[0]
    h = _rms(x, gpre_ref[...]) * (1.0 + sc_ref[0]) + sh_ref[0]
    hb = h.astype(BF16)

    xbuf[CONV_PAD:CONV_PAD + T, :] = _dot(hb, win_ref[:, OFF_XR:OFF_XR + D_RNN])
    xc = bconv_ref[...]
    for k in range(CONV_W):
        lo = CONV_PAD - (CONV_W - 1) + k
        xc = xc + xbuf[lo:lo + T, :] * wconv_ref[k:k + 1, :]
    tail = xbuf[T + CONV_PAD - (CONV_W - 1):T + CONV_PAD, :]
    cout_ref[0] = tail
    xbuf[CONV_PAD - (CONV_W - 1):CONV_PAD, :] = tail

    lam = lam_ref[...]
    softplus_neg_lam = jnp.maximum(-lam, 0.0) + jnp.log1p(jnp.exp(-jnp.abs(lam)))
    for n in range(N_RNN_BLOCKS):
        blk = slice(n * RNN_BLOCK, (n + 1) * RNN_BLOCK)
        xcb = xc[:, blk]
        g = _dot(xcb.astype(BF16), wgate_ref[n])
        r = jax.nn.sigmoid(g[:, :RNN_BLOCK] + br_ref[:, blk])
        i = jax.nn.sigmoid(g[:, RNN_BLOCK:] + bi_ref[:, blk])
        log_a = -LRU_C * r * softplus_neg_lam[:, blk]
        a = jnp.exp(log_a)
        a_scr[:, blk] = a
        b_scr[:, blk] = jnp.sqrt(-jnp.tanh(log_a) * (a * a + 1.0)) * (i * xcb)

    def row(s, hprev):
        hnew = a_scr[pl.ds(s, 1), :] * hprev + b_scr[pl.ds(s, 1), :]
        b_scr[pl.ds(s, 1), :] = hnew
        return hnew

    hlast = lax.fori_loop(0, T, row, hcar[...], unroll=8)
    hcar[...] = hlast
    lout_ref[0] = hlast

    gr = _dot(hb, win_ref[:, OFF_GR:OFF_GR + D_RNN])
    y_a = _dot((b_scr[...] * jax.nn.gelu(gr)).astype(BF16), wrnn_ref[...])
    ga = _dot(hb, win_ref[:, OFF_GA:OFF_GA + D_MODEL])
    ma_ref[0] = (jax.nn.sigmoid(ga) * y_a).astype(BF16)
    gb = _dot(hb, win_ref[:, OFF_GB:OFF_GB + D_MODEL])
    sgb_ref[0] = jax.nn.sigmoid(gb).astype(BF16)

    rc = rc_ref[...]
    rs = rs_ref[...]
    ql = _dot(hb, win_ref[:, OFF_QL:OFF_QL + Q_LORA])
    q = _dot(_rms(ql, gq_ref[...]).astype(BF16), wqup_ref[...])
    for hd in range(N_HEADS):
        base = hd * HEAD_K
        q_ref[0, :, base:base + QK_NOPE] = (q[:, base:base + QK_NOPE] * Q_SCALE).astype(BF16)
        pe = _rope(q[:, base + QK_NOPE:base + HEAD_K], rc, rs)
        q_ref[0, :, base + QK_NOPE:base + HEAD_K] = (pe * Q_SCALE).astype(BF16)
    kvl = _dot(hb, win_ref[:, OFF_KVL:OFF_KVL + KV_LORA])
    ckv_ref[0] = _rms(kvl, gkv_ref[...])
    kp = _rope(_dot(hb, win_ref[:, OFF_KR:OFF_KR + 2 * ROPE_DIM]), rc, rs)
    kpe_ref[0] = kp[:, :ROPE_DIM]
    kpe128_ref[0] = kp.astype(BF16)


def _mixer_in(x, sh1, sc1, conv_state, lru_state, rope_c, rope_s, W, T):
    B, S, _ = x.shape
    nt = S // T
    kern = functools.partial(_mixer_in_kernel, T=T)
    bt = lambda b, t: (b, t, 0)
    bo = lambda b, t: (b, 0, 0)
    tt = lambda b, t: (t, 0)
    in_specs = [
        pl.BlockSpec((1, T, D_MODEL), bt),
        pl.BlockSpec((1, 1, D_MODEL), bo),
        pl.BlockSpec((1, 1, D_MODEL), bo),
        _const_spec((1, D_MODEL)),
        _const_spec((D_MODEL, IN_COLS2)),
        pl.BlockSpec((1, CONV_W - 1, D_RNN), bo),
        pl.BlockSpec((1, 1, D_RNN), bo),
        _const_spec((CONV_W, D_RNN)),
        _const_spec((1, D_RNN)),
        _const_spec((N_RNN_BLOCKS, RNN_BLOCK, 2 * RNN_BLOCK)),
        _const_spec((1, D_RNN)),
        _const_spec((1, D_RNN)),
        _const_spec((1, D_RNN)),
        _const_spec((D_RNN, D_MODEL)),
        _const_spec((1, Q_LORA)),
        _const_spec((Q_LORA, N_HEADS * HEAD_K)),
        _const_spec((1, KV_LORA)),
        pl.BlockSpec((T, LANES), tt),
        pl.BlockSpec((T, LANES), tt),
    ]
    out_specs = [
        pl.BlockSpec((1, T, D_MODEL), bt),
        pl.BlockSpec((1, T, D_MODEL), bt),
        pl.BlockSpec((1, T, N_HEADS * HEAD_K), bt),
        pl.BlockSpec((1, T, KV_LORA), bt),
        pl.BlockSpec((1, T, ROPE_DIM), bt),
        pl.BlockSpec((1, T, LANES), bt),
        pl.BlockSpec((1, CONV_W - 1, D_RNN), bo),
        pl.BlockSpec((1, 1, D_RNN), bo),
    ]
    out_shape = [
        jax.ShapeDtypeStruct((B, S, D_MODEL), BF16),
        jax.ShapeDtypeStruct((B, S, D_MODEL), BF16),
        jax.ShapeDtypeStruct((B, S, N_HEADS * HEAD_K), BF16),
        jax.ShapeDtypeStruct((B, S, KV_LORA), F32),
        jax.ShapeDtypeStruct((B, S, ROPE_DIM), F32),
        jax.ShapeDtypeStruct((B, S, LANES), BF16),
        jax.ShapeDtypeStruct((B, CONV_W - 1, D_RNN), F32),
        jax.ShapeDtypeStruct((B, 1, D_RNN), F32),
    ]
    scratch = [
        pltpu.VMEM((T + CONV_PAD, D_RNN), F32),
        pltpu.VMEM((T, D_RNN), F32),
        pltpu.VMEM((T, D_RNN), F32),
        pltpu.VMEM((1, D_RNN), F32),
    ]
    return pl.pallas_call(
        kern, grid=(B, nt), in_specs=in_specs, out_specs=out_specs, out_shape=out_shape,
        scratch_shapes=scratch, compiler_params=_cparams(("arbitrary", "arbitrary")),
        name="mixer_in",
    )(x, sh1, sc1, W["g_pre1"], W["w_in2"], conv_state, lru_state, W["w_conv"], W["b_conv"],
      W["w_gates"], W["b_rgate"], W["b_igate"], W["lam"], W["w_rnn_out"], W["g_q"], W["w_qup"],
      W["g_kv"], rope_c, rope_s)


def _kv_up_kernel(ckv_ref, kpe_ref, wk_ref, wv_ref, k_ref, v_ref, *, v_transposed):
    c = ckv_ref[...].astype(BF16)
    kn = _dot(c, wk_ref[...])
    kpe = kpe_ref[...]
    for hd in range(N_HEADS):
        base = hd * HEAD_K
        k_ref[:, base:base + QK_NOPE] = kn[:, hd * QK_NOPE:(hd + 1) * QK_NOPE].astype(BF16)
        k_ref[:, base + QK_NOPE:base + HEAD_K] = kpe
    if v_transposed:
        vt = lax.dot_general(wv_ref[...], c, (((1,), (1,)), ((), ())), preferred_element_type=F32)
        for hd in range(N_HEADS):
            v_ref[0, hd] = vt[hd * V_HEAD:(hd + 1) * V_HEAD, :].astype(BF16)
    else:
        v_ref[...] = _dot(c, wv_ref[...]).astype(BF16)


def _kv_up(ckv, kpe128, wk, wv, T, v_transposed):
    R = ckv.shape[0]
    row = lambda i: (i, 0)
    if v_transposed:
        v_spec = pl.BlockSpec((1, N_HEADS, V_HEAD, T), lambda i: (i, 0, 0, 0))
        v_shape = jax.ShapeDtypeStruct((R // T, N_HEADS, V_HEAD, T), BF16)
    else:
        v_spec = pl.BlockSpec((T, N_HEADS * V_HEAD), row)
        v_shape = jax.ShapeDtypeStruct((R, N_HEADS * V_HEAD), BF16)
    return pl.pallas_call(
        functools.partial(_kv_up_kernel, v_transposed=v_transposed), grid=(R // T,),
        in_specs=[pl.BlockSpec((T, KV_LORA), row), pl.BlockSpec((T, LANES), row),
                  _const_spec(wk.shape), _const_spec(wv.shape)],
        out_specs=[pl.BlockSpec((T, N_HEADS * HEAD_K), row), v_spec],
        out_shape=[jax.ShapeDtypeStruct((R, N_HEADS * HEAD_K), BF16), v_shape],
        compiler_params=_cparams(("arbitrary",)),
        name="kv_up",
    )(ckv, kpe128, wk, wv)


def _attn_prompt_kernel(q_ref, k_ref, vt_ref, o_ref, qt_scr, s0, s1, m_scr, l_scr, acc_scr, *, QB, HP):
    qi = pl.program_id(2)
    for hh in range(HP):
        qt_scr[hh] = q_ref[0, :, hh * HEAD_K:(hh + 1) * HEAD_K].T
    m_scr[...] = jnp.full(m_scr.shape, NEG, F32)
    l_scr[...] = jnp.zeros(l_scr.shape, F32)
    acc_scr[...] = jnp.zeros(acc_scr.shape, F32)

    def scores(j, dst):
        start = pl.multiple_of(j * QB, QB)
        for hh in range(HP):
            dst[hh] = _dot(k_ref[0, pl.ds(start, QB), hh * HEAD_K:(hh + 1) * HEAD_K], qt_scr[hh])

    def update(j, src, masked):
        for hh in range(HP):
            s = src[hh]
            if masked:
                ck = lax.broadcasted_iota(jnp.int32, (QB, QB), 0) // CHUNK
                cq = lax.broadcasted_iota(jnp.int32, (QB, QB), 1) // CHUNK
                s = jnp.where(ck <= cq, s, NEG)
            m_old = m_scr[hh]
            m_new = jnp.maximum(m_old, jnp.max(s, axis=0, keepdims=True))
            p = jnp.exp2(s - m_new)
            alpha = jnp.exp2(m_old - m_new)
            l_scr[hh] = alpha * l_scr[hh] + jnp.sum(p, axis=0, keepdims=True)
            acc_scr[hh] = alpha * acc_scr[hh] + _dot(vt_ref[j, hh], p.astype(BF16))
            m_scr[hh] = m_new

    scores(0, s0)

    def pair(jj, c):
        j = 2 * jj
        scores(j + 1, s1)
        update(j, s0, False)
        scores(j + 2, s0)
        update(j + 1, s1, False)
        return c

    lax.fori_loop(0, qi // 2, pair, 0)

    @pl.when(qi % 2 == 0)
    def _():
        update(qi, s0, True)

    @pl.when(qi % 2 == 1)
    def _():
        scores(qi, s1)
        update(qi - 1, s0, False)
        update(qi, s1, True)

    for hh in range(HP):
        o_ref[0, :, hh * V_HEAD:(hh + 1) * V_HEAD] = (acc_scr[hh] / l_scr[hh]).T.astype(BF16)


def _attn_prompt(q, k, vt, QB, HP):
    B, S, _ = q.shape
    nkb = S // QB
    kern = functools.partial(_attn_prompt_kernel, QB=QB, HP=HP)
    return pl.pallas_call(
        kern, grid=(B, N_HEADS // HP, S // QB),
        in_specs=[pl.BlockSpec((1, QB, HP * HEAD_K), lambda b, h, i: (b, i, h)),
                  pl.BlockSpec((1, S, HP * HEAD_K), lambda b, h, i: (b, 0, h)),
                  pl.BlockSpec((nkb, HP, V_HEAD, QB), lambda b, h, i: (b, h, 0, 0))],
        out_specs=pl.BlockSpec((1, QB, HP * V_HEAD), lambda b, h, i: (b, i, h)),
        out_shape=jax.ShapeDtypeStruct((B, S, N_HEADS * V_HEAD), BF16),
        scratch_shapes=[pltpu.VMEM((HP, HEAD_K, QB), BF16),
                        pltpu.VMEM((HP, QB, QB), F32), pltpu.VMEM((HP, QB, QB), F32),
                        pltpu.VMEM((HP, 1, QB), F32),
                        pltpu.VMEM((HP, 1, QB), F32), pltpu.VMEM((HP, V_HEAD, QB), F32)],
        compiler_params=_cparams(("arbitrary", "arbitrary", "arbitrary")),
        name="attn_prompt",
    )(q, k, vt)


def _attn_sample_kernel(q_ref, k_ref, v_ref, o_ref):
    for hd in range(N_HEADS):
        q = q_ref[0, :, hd * HEAD_K:(hd + 1) * HEAD_K]
        k = k_ref[0, :, hd * HEAD_K:(hd + 1) * HEAD_K]
        s = lax.dot_general(q, k, (((1,), (1,)), ((), ())), preferred_element_type=F32)
        p = jnp.exp2(s - jnp.max(s, axis=1, keepdims=True))
        l = jnp.sum(p, axis=1, keepdims=True)
        o = _dot(p.astype(BF16), v_ref[0, :, hd * V_HEAD:(hd + 1) * V_HEAD])
        o_ref[0, :, hd * V_HEAD:(hd + 1) * V_HEAD] = (o / l).astype(BF16)


def _attn_sample(q, k, v):
    B, S, _ = q.shape
    LK = k.shape[1]
    b3 = lambda b: (b, 0, 0)
    return pl.pallas_call(
        _attn_sample_kernel, grid=(B,),
        in_specs=[pl.BlockSpec((1, S, N_HEADS * HEAD_K), b3),
                  pl.BlockSpec((1, LK, N_HEADS * HEAD_K), b3),
                  pl.BlockSpec((1, LK, N_HEADS * V_HEAD), b3)],
        out_specs=pl.BlockSpec((1, S, N_HEADS * V_HEAD), b3),
        out_shape=jax.ShapeDtypeStruct((B, S, N_HEADS * V_HEAD), BF16),
        compiler_params=_cparams(("arbitrary",)),
        name="attn_sample",
    )(q, k, v)


def _lane_min_where(mask, lane):
    return jnp.min(jnp.where(mask, lane, LANES), axis=1, keepdims=True)


def _merge_kernel(o_ref, ma_ref, sgb_ref, x_ref, gt1_ref, sh2_ref, sc2_ref, gpost1_ref, gpre2_ref,
                  wao_ref, wout_ref, wr_ref, br_ref, x1_ref, h2_ref, comb_ref, *, T):
    nc = T // CHUNK
    y_b = _dot(o_ref[...], wao_ref[...])
    m = ma_ref[...].astype(F32) + sgb_ref[...].astype(F32) * y_b
    y = _dot(m.astype(BF16), wout_ref[...])
    yn = _rms(y, gpost1_ref[...]).reshape(nc, CHUNK, D_MODEL)
    x1 = x_ref[...].reshape(nc, CHUNK, D_MODEL) + gt1_ref[...] * yn
    x1_ref[...] = x1.reshape(T, D_MODEL)
    h2 = (_rms(x1, gpre2_ref[...]) * (1.0 + sc2_ref[...]) + sh2_ref[...]).reshape(T, D_MODEL)
    h2b = h2.astype(BF16)
    h2_ref[...] = h2b

    logits = _dot(h2b, wr_ref[...]) + br_ref[...]
    lane = lax.broadcasted_iota(jnp.int32, (T, LANES), 1)
    is_g = lane < N_GROUPS
    gmax = jnp.max(jnp.where(is_g, logits, NEG), axis=1, keepdims=True)
    gidx = _lane_min_where(is_g & (logits == gmax), lane)
    gsum = jnp.sum(jnp.where(is_g, jnp.exp(logits - gmax), 0.0), axis=1, keepdims=True)
    lo = ROUTE_OFF + EXP_PER_GROUP * gidx
    sel = (lane >= lo) & (lane < lo + EXP_PER_GROUP)
    m1 = jnp.max(jnp.where(sel, logits, NEG), axis=1, keepdims=True)
    i1 = _lane_min_where(sel & (logits == m1), lane)
    rest = sel & (lane != i1)
    m2 = jnp.max(jnp.where(rest, logits, NEG), axis=1, keepdims=True)
    i2 = _lane_min_where(rest & (logits == m2), lane)
    e2 = jnp.exp(m2 - m1)
    w1 = 1.0 / (gsum * (1.0 + e2))
    w2 = w1 * e2
    comb_ref[...] = jnp.where(lane == i1, w1, 0.0) + jnp.where(lane == i2, w2, 0.0)


def _merge(o, ma, sgb, x, gt1c, sh2c, sc2c, W, T):
    N = x.shape[0]
    nc = T // CHUNK
    row = lambda i: (i, 0)
    chunk = lambda i: (i, 0, 0)
    kern = functools.partial(_merge_kernel, T=T)
    return pl.pallas_call(
        kern, grid=(N // T,),
        in_specs=[pl.BlockSpec((T, D_MODEL), row), pl.BlockSpec((T, D_MODEL), row),
                  pl.BlockSpec((T, D_MODEL), row), pl.BlockSpec((T, D_MODEL), row),
                  pl.BlockSpec((nc, 1, D_MODEL), chunk), pl.BlockSpec((nc, 1, D_MODEL), chunk),
                  pl.BlockSpec((nc, 1, D_MODEL), chunk),
                  _const_spec((1, D_MODEL)), _const_spec((1, D_MODEL)),
                  _const_spec((D_MODEL, D_MODEL)), _const_spec((D_MODEL, D_MODEL)),
                  _const_spec((D_MODEL, LANES)), _const_spec((1, LANES))],
        out_specs=[pl.BlockSpec((T, D_MODEL), row), pl.BlockSpec((T, D_MODEL), row),
                   pl.BlockSpec((T, LANES), row)],
        out_shape=[jax.ShapeDtypeStruct((N, D_MODEL), F32), jax.ShapeDtypeStruct((N, D_MODEL), BF16),
                   jax.ShapeDtypeStruct((N, LANES), F32)],
        compiler_params=_cparams(("arbitrary",)),
        name="merge",
    )(o, ma, sgb, x, gt1c, sh2c, sc2c, W["g_post1"], W["g_pre2"], W["w_attn_out"], W["w_out"],
      W["w_route"], W["b_route"])


def _moe_kernel(h_ref, comb_ref, x1_ref, gt2_ref, gpost2_ref, wg_ref, wu_ref, wd_ref, y_ref, acc, *, T):
    e = pl.program_id(1)
    nc = T // CHUNK

    @pl.when(e == 0)
    def _():
        acc[...] = jnp.zeros(acc.shape, F32)

    h = h_ref[...]
    g = _dot(h, wg_ref[0])
    u = _dot(h, wu_ref[0])
    hid = (g * jax.nn.sigmoid(g) * u).astype(BF16)
    lane = lax.broadcasted_iota(jnp.int32, (T, LANES), 1)
    ce = jnp.sum(jnp.where(lane == e + ROUTE_OFF, comb_ref[...], 0.0), axis=1, keepdims=True)
    acc[...] += ce * _dot(hid, wd_ref[0])

    @pl.when(e == N_EXPERTS - 1)
    def _():
        on = _rms(acc[...], gpost2_ref[...]).reshape(nc, CHUNK, D_MODEL)
        y = x1_ref[...].reshape(nc, CHUNK, D_MODEL) + gt2_ref[...] * on
        y_ref[...] = y.reshape(T, D_MODEL)


def _moe(h2, comb, x1, gt2c, W, T):
    N = h2.shape[0]
    nc = T // CHUNK
    row = lambda i, e: (i, 0)
    kern = functools.partial(_moe_kernel, T=T)
    return pl.pallas_call(
        kern, grid=(N // T, N_EXPERTS),
        in_specs=[pl.BlockSpec((T, D_MODEL), row), pl.BlockSpec((T, LANES), row),
                  pl.BlockSpec((T, D_MODEL), row),
                  pl.BlockSpec((nc, 1, D_MODEL), lambda i, e: (i, 0, 0)),
                  _const_spec((1, D_MODEL)),
                  pl.BlockSpec((1, D_MODEL, D_EXPERT), lambda i, e: (e, 0, 0)),
                  pl.BlockSpec((1, D_MODEL, D_EXPERT), lambda i, e: (e, 0, 0)),
                  pl.BlockSpec((1, D_EXPERT, D_MODEL), lambda i, e: (e, 0, 0))],
        out_specs=pl.BlockSpec((T, D_MODEL), row),
        out_shape=jax.ShapeDtypeStruct((N, D_MODEL), F32),
        scratch_shapes=[pltpu.VMEM((T, D_MODEL), F32)],
        compiler_params=_cparams(("arbitrary", "arbitrary")),
        name="moe",
    )(h2, comb, x1, gt2c, W["g_post2"], W["w_exp_gate"], W["w_exp_up"], W["w_exp_down"])


def _rotate_half_cols(w):
    half = ROPE_DIM // 2
    return jnp.concatenate([-w[..., half:], w[..., :half]], axis=-1)


def _rope_tables(pos):
    inv = ROPE_THETA ** (-jnp.arange(0, ROPE_DIM, 2, dtype=F32) / ROPE_DIM)
    ang = pos.astype(F32)[:, None] * inv
    z = jnp.zeros((pos.shape[0], LANES - ROPE_DIM), F32)
    c, s = jnp.cos(ang), jnp.sin(ang)
    return jnp.concatenate([c, c, z], axis=1), jnp.concatenate([s, s, z], axis=1)


def _chunk_rows(v, seq):
    B, D = v.shape
    return jnp.broadcast_to(v[:, None, None, :], (B, seq // CHUNK, 1, D)).reshape(B * (seq // CHUNK), 1, D)


def _layer(x, ada, ckv_past, kpe_past, conv_state, lru_state, pos0, W, T_in, T_tok):
    B, S, _ = x.shape
    L = 0 if ckv_past is None else ckv_past.shape[1]
    sh1, sc1, gt1, sh2, sc2, gt2 = jnp.split(ada, 6, axis=-1)
    rope_c, rope_s = _rope_tables(pos0 + jnp.arange(S))
    ma, sgb, q, ckv, kpe, kpe128, conv_new, lru_new = _mixer_in(
        x, sh1[:, None, :], sc1[:, None, :], conv_state, lru_state[:, None, :], rope_c, rope_s, W, T_in)

    if L == 0:
        k, vt = _kv_up(ckv.reshape(B * S, KV_LORA), kpe128.reshape(B * S, LANES), W["w_k_up"], W["w_v_up_t"],
                       ATTN_BLOCK, True)
        o = _attn_prompt(q, k.reshape(B, S, -1), vt, ATTN_BLOCK, ATTN_HEADS_PER_STEP)
    else:
        ckv_all = jnp.concatenate([ckv_past, ckv], axis=1)
        kpe_past128 = jnp.pad(kpe_past, ((0, 0), (0, 0), (0, LANES - ROPE_DIM))).astype(BF16)
        kpe_all = jnp.concatenate([kpe_past128, kpe128], axis=1)
        LK = L + S
        k, v = _kv_up(ckv_all.reshape(B * LK, KV_LORA), kpe_all.reshape(B * LK, LANES), W["w_k_up"], W["w_v_up"],
                      256, False)
        o = _attn_sample(q, k.reshape(B, LK, -1), v.reshape(B, LK, -1))

    N = B * S
    x1, h2, comb = _merge(o.reshape(N, D_MODEL), ma.reshape(N, D_MODEL), sgb.reshape(N, D_MODEL),
                          x.reshape(N, D_MODEL), _chunk_rows(gt1, S), _chunk_rows(sh2, S),
                          _chunk_rows(sc2, S), W, T_tok)
    y = _moe(h2, comb, x1, _chunk_rows(gt2, S), W, T_tok)
    return y.reshape(B, S, D_MODEL), ckv, kpe, conv_new, lru_new.reshape(B, D_RNN)


def kernel(x_prompt, x_sample, c_prompt, c_sample, cache_ckv, cache_kpe, state_conv, state_rglru, w_ada, b_ada, g_pre1, g_post1, g_pre2, g_post2, w_in, w_conv, b_conv, w_rgate, b_rgate, w_igate, b_igate, lru_lambda, w_rnn_out, g_q_lat, w_q_up, g_kv_lat, w_k_up, w_v_up, w_attn_out, w_out, w_group, b_group, w_erouter, b_erouter, w_exp_gate, w_exp_up, w_exp_down):
    assert w_in.shape[0] == 1, "single-layer trunk"
    B = x_prompt.shape[0]
    wi = w_in[0]
    sp = lambda a, b: wi[:, a:b]
    xr, gr = sp(0, D_RNN), sp(D_RNN, 2 * D_RNN)
    o = 2 * D_RNN
    ql, kvl, kr = sp(o, o + Q_LORA), sp(o + Q_LORA, o + Q_LORA + KV_LORA), \
        sp(o + Q_LORA + KV_LORA, o + Q_LORA + KV_LORA + ROPE_DIM)
    o = o + Q_LORA + KV_LORA + ROPE_DIM
    ga, gb = sp(o, o + D_MODEL), sp(o + D_MODEL, o + 2 * D_MODEL)
    wq = w_q_up[0].reshape(Q_LORA, N_HEADS, QK_NOPE + ROPE_DIM)
    wq_pe = wq[..., QK_NOPE:]
    row = lambda a: a[0].reshape(1, -1)
    W = {
        "g_pre1": row(g_pre1), "g_post1": row(g_post1), "g_pre2": row(g_pre2), "g_post2": row(g_post2),
        "w_in2": jnp.concatenate([xr, gr, ql, kvl, kr, _rotate_half_cols(kr), ga, gb], axis=1).astype(BF16),
        "w_conv": w_conv[0], "b_conv": row(b_conv),
        "w_gates": jnp.concatenate([w_rgate[0], w_igate[0]], axis=-1).astype(BF16),
        "b_rgate": row(b_rgate), "b_igate": row(b_igate), "lam": row(lru_lambda),
        "w_rnn_out": w_rnn_out[0].astype(BF16),
        "g_q": row(g_q_lat), "g_kv": row(g_kv_lat),
        "w_qup": jnp.concatenate([wq[..., :QK_NOPE], wq_pe, _rotate_half_cols(wq_pe)], axis=-1)
                 .reshape(Q_LORA, N_HEADS * HEAD_K).astype(BF16),
        "w_k_up": w_k_up[0].astype(BF16), "w_v_up": w_v_up[0].astype(BF16),
        "w_v_up_t": w_v_up[0].T.astype(BF16),
        "w_attn_out": w_attn_out[0].astype(BF16), "w_out": w_out[0].astype(BF16),
        "w_route": jnp.pad(jnp.concatenate([w_group[0], w_erouter[0]], axis=1),
                           ((0, 0), (0, LANES - N_GROUPS - N_EXPERTS))).astype(BF16),
        "b_route": jnp.pad(jnp.concatenate([b_group[0], b_erouter[0]]), (0, LANES - N_GROUPS - N_EXPERTS))
                   .reshape(1, LANES),
        "w_exp_gate": w_exp_gate[0].astype(BF16), "w_exp_up": w_exp_up[0].astype(BF16),
        "w_exp_down": w_exp_down[0].astype(BF16),
    }
    ada = _ada(jnp.concatenate([c_prompt, c_sample], axis=0), w_ada[0], b_ada[0])
    zeros_conv = jnp.zeros((B, CONV_W - 1, D_RNN), F32)
    zeros_lru = jnp.zeros((B, D_RNN), F32)
    yp, ckv_p, kpe_p, conv_p, lru_p = _layer(x_prompt, ada[:B], None, None, zeros_conv, zeros_lru, 0, W, 256, 512)
    ys, ckv_s, kpe_s, conv_s, lru_s = _layer(x_sample, ada[B:], cache_ckv[0], cache_kpe[0], state_conv[0],
                                             state_rglru[0], cache_ckv.shape[2], W, 64, 512)
    return (yp, ys, ckv_p[None], kpe_p[None], conv_p[None], lru_p[None],
            ckv_s[None], kpe_s[None], conv_s[None], lru_s[None])
```

```python
import functools

import jax
import jax.numpy as jnp
from jax import lax
from jax.experimental import pallas as pl
from jax.experimental.pallas import tpu as pltpu

F32 = jnp.float32
BF16 = jnp.bfloat16

D_MODEL = 1024
CHUNK = 64
D_RNN = 1024
N_RNN_BLOCKS = 8
RNN_BLOCK = D_RNN // N_RNN_BLOCKS
CONV_W = 4
LRU_C = 8.0
N_HEADS = 8
QK_NOPE = 128
ROPE_DIM = 64
V_HEAD = 128
Q_LORA = 384
KV_LORA = 256
ROPE_THETA = 10000.0
SM_SCALE = (QK_NOPE + ROPE_DIM) ** -0.5
LOG2E = 1.4426950408889634
Q_SCALE = SM_SCALE * LOG2E
N_GROUPS = 4
EXP_PER_GROUP = 4
N_EXPERTS = N_GROUPS * EXP_PER_GROUP
D_EXPERT = 512
EPS = 1e-6

LANES = 128
SUBLANES = 8
HEAD_K = QK_NOPE + 2 * ROPE_DIM
OFF_XR = 0
OFF_GR = OFF_XR + D_RNN
OFF_QL = OFF_GR + D_RNN
OFF_KVL = OFF_QL + Q_LORA
OFF_KR = OFF_KVL + KV_LORA
OFF_GA = OFF_KR + 2 * ROPE_DIM
OFF_GB = OFF_GA + D_MODEL
IN_COLS2 = OFF_GB + D_MODEL
ROUTE_OFF = N_GROUPS
CONV_PAD = 8
NEG = -1e30
ATTN_BLOCK = 512
MOE_TILE = 1024
KEY_STRIDE = 2048
MOE_CHUNK = 192
ATTN_HEADS_PER_STEP = 2
VMEM_LIMIT = 56 * 1024 * 1024


def _cparams(sem):
    return pltpu.CompilerParams(dimension_semantics=sem, vmem_limit_bytes=VMEM_LIMIT)


def _const_spec(shape):
    n = len(shape)
    return pl.BlockSpec(shape, lambda *_: (0,) * n, pipeline_mode=pl.Buffered(1))


def _rms(x, g):
    return x * lax.rsqrt(jnp.mean(x * x, axis=-1, keepdims=True) + EPS) * g


def _dot(a, b):
    return jnp.dot(a, b, preferred_element_type=F32)


def _rope(v, c, s):
    return v * c + pltpu.roll(v, ROPE_DIM, axis=1) * s


def _ada_kernel(c_ref, w_ref, b_ref, o_ref):
    c = c_ref[...]
    s = c * jax.nn.sigmoid(c)
    o_ref[...] = jnp.dot(s, w_ref[...], preferred_element_type=F32,
                         precision=lax.Precision.HIGHEST) + b_ref[...]


def _ada(c, w, b):
    nb = c.shape[0]
    n = w.shape[1]
    bn = n // 6
    return pl.pallas_call(
        _ada_kernel,
        grid=(n // bn,),
        in_specs=[_const_spec((nb, D_MODEL)),
                  pl.BlockSpec((D_MODEL, bn), lambda j: (0, j)),
                  pl.BlockSpec((1, bn), lambda j: (0, j))],
        out_specs=pl.BlockSpec((nb, bn), lambda j: (0, j)),
        out_shape=jax.ShapeDtypeStruct((nb, n), F32),
        compiler_params=_cparams(("arbitrary",)),
        name="ada",
    )(c, w, b.reshape(1, n))


def _mixer_in_kernel(x_ref, sh_ref, sc_ref, gpre_ref, win_ref, cst_ref, lst_ref, wconv_ref, bconv_ref,
                     wgate_ref, br_ref, bi_ref, lam_ref, wrnn_ref, gq_ref, wqup_ref, gkv_ref,
                     rc_ref, rs_ref,
                     ma_ref, sgb_ref, q_ref, ckv_ref, kpe_ref, kpe128_ref, cout_ref, lout_ref,
                     xbuf, b_scr, hcar, *, T):
    t = pl.program_id(1)

    @pl.when(t == 0)
    def _():
        xbuf[CONV_PAD - (CONV_W - 1):CONV_PAD, :] = cst_ref[0]
        hcar[...] = lst_ref[0]

    x = x_ref[0]
    h = _rms(x, gpre_ref[...]) * (1.0 + sc_ref[0]) + sh_ref[0]
    hb = h.astype(BF16)

    xbuf[CONV_PAD:CONV_PAD + T, :] = _dot(hb, win_ref[:, OFF_XR:OFF_XR + D_RNN])
    xc = bconv_ref[...]
    for k in range(CONV_W):
        lo = CONV_PAD - (CONV_W - 1) + k
        xc = xc + xbuf[lo:lo + T, :] * wconv_ref[k:k + 1, :]
    tail = xbuf[T + CONV_PAD - (CONV_W - 1):T + CONV_PAD, :]
    cout_ref[0] = tail
    xbuf[CONV_PAD - (CONV_W - 1):CONV_PAD, :] = tail

    lam = lam_ref[...]
    softplus_neg_lam = jnp.maximum(-lam, 0.0) + jnp.log1p(jnp.exp(-jnp.abs(lam)))
    row_in_group = lax.broadcasted_iota(jnp.int32, (T, RNN_BLOCK), 0) % SUBLANES
    for n in range(N_RNN_BLOCKS):
        blk = slice(n * RNN_BLOCK, (n + 1) * RNN_BLOCK)
        xcb = xc[:, blk]
        g = _dot(xcb.astype(BF16), wgate_ref[n])
        r = jax.nn.sigmoid(g[:, :RNN_BLOCK] + br_ref[:, blk])
        i = jax.nn.sigmoid(g[:, RNN_BLOCK:] + bi_ref[:, blk])
        log_a = -LRU_C * r * softplus_neg_lam[:, blk]
        a = jnp.exp(log_a)
        b = jnp.sqrt(-jnp.tanh(log_a) * (a * a + 1.0)) * (i * xcb)
        for d in (1, 2, 4):
            keep = row_in_group >= d
            a_prev = jnp.where(keep, pltpu.roll(a, d, axis=0), 1.0)
            b_prev = jnp.where(keep, pltpu.roll(b, d, axis=0), 0.0)
            b = b + a * b_prev
            a = a * a_prev
        hprev = hcar[:, blk]
        for grp in range(T // SUBLANES):
            rows = slice(grp * SUBLANES, (grp + 1) * SUBLANES)
            hg = b[rows, :] + a[rows, :] * hprev
            b_scr[rows, blk] = hg
            hprev = hg[SUBLANES - 1:SUBLANES, :]
        hcar[:, blk] = hprev
    lout_ref[0] = hcar[...]

    gr = _dot(hb, win_ref[:, OFF_GR:OFF_GR + D_RNN])
    y_a = _dot((b_scr[...] * jax.nn.gelu(gr)).astype(BF16), wrnn_ref[...])
    ga = _dot(hb, win_ref[:, OFF_GA:OFF_GA + D_MODEL])
    ma_ref[0] = (jax.nn.sigmoid(ga) * y_a).astype(BF16)
    gb = _dot(hb, win_ref[:, OFF_GB:OFF_GB + D_MODEL])
    sgb_ref[0] = jax.nn.sigmoid(gb).astype(BF16)

    rc = rc_ref[...]
    rs = rs_ref[...]
    ql = _dot(hb, win_ref[:, OFF_QL:OFF_QL + Q_LORA])
    q = _dot(_rms(ql, gq_ref[...]).astype(BF16), wqup_ref[...])
    for hd in range(N_HEADS):
        base = hd * HEAD_K
        q_ref[0, :, base:base + QK_NOPE] = (q[:, base:base + QK_NOPE] * Q_SCALE).astype(BF16)
        pe = _rope(q[:, base + QK_NOPE:base + HEAD_K], rc, rs)
        q_ref[0, :, base + QK_NOPE:base + HEAD_K] = (pe * Q_SCALE).astype(BF16)
    kvl = _dot(hb, win_ref[:, OFF_KVL:OFF_KVL + KV_LORA])
    ckv_ref[0] = _rms(kvl, gkv_ref[...])
    kp = _rope(_dot(hb, win_ref[:, OFF_KR:OFF_KR + 2 * ROPE_DIM]), rc, rs)
    kpe_ref[0] = kp[:, :ROPE_DIM]
    kpe128_ref[0] = kp.astype(BF16)


def _mixer_in(x, sh1, sc1, conv_state, lru_state, rope_c, rope_s, W, T):
    B, S, _ = x.shape
    nt = S // T
    kern = functools.partial(_mixer_in_kernel, T=T)
    bt = lambda b, t: (b, t, 0)
    bo = lambda b, t: (b, 0, 0)
    tt = lambda b, t: (t, 0)
    in_specs = [
        pl.BlockSpec((1, T, D_MODEL), bt),
        pl.BlockSpec((1, 1, D_MODEL), bo),
        pl.BlockSpec((1, 1, D_MODEL), bo),
        _const_spec((1, D_MODEL)),
        _const_spec((D_MODEL, IN_COLS2)),
        pl.BlockSpec((1, CONV_W - 1, D_RNN), bo),
        pl.BlockSpec((1, 1, D_RNN), bo),
        _const_spec((CONV_W, D_RNN)),
        _const_spec((1, D_RNN)),
        _const_spec((N_RNN_BLOCKS, RNN_BLOCK, 2 * RNN_BLOCK)),
        _const_spec((1, D_RNN)),
        _const_spec((1, D_RNN)),
        _const_spec((1, D_RNN)),
        _const_spec((D_RNN, D_MODEL)),
        _const_spec((1, Q_LORA)),
        _const_spec((Q_LORA, N_HEADS * HEAD_K)),
        _const_spec((1, KV_LORA)),
        pl.BlockSpec((T, LANES), tt),
        pl.BlockSpec((T, LANES), tt),
    ]
    out_specs = [
        pl.BlockSpec((1, T, D_MODEL), bt),
        pl.BlockSpec((1, T, D_MODEL), bt),
        pl.BlockSpec((1, T, N_HEADS * HEAD_K), bt),
        pl.BlockSpec((1, T, KV_LORA), bt),
        pl.BlockSpec((1, T, ROPE_DIM), bt),
        pl.BlockSpec((1, T, LANES), bt),
        pl.BlockSpec((1, CONV_W - 1, D_RNN), bo),
        pl.BlockSpec((1, 1, D_RNN), bo),
    ]
    out_shape = [
        jax.ShapeDtypeStruct((B, S, D_MODEL), BF16),
        jax.ShapeDtypeStruct((B, S, D_MODEL), BF16),
        jax.ShapeDtypeStruct((B, S, N_HEADS * HEAD_K), BF16),
        jax.ShapeDtypeStruct((B, S, KV_LORA), F32),
        jax.ShapeDtypeStruct((B, S, ROPE_DIM), F32),
        jax.ShapeDtypeStruct((B, S, LANES), BF16),
        jax.ShapeDtypeStruct((B, CONV_W - 1, D_RNN), F32),
        jax.ShapeDtypeStruct((B, 1, D_RNN), F32),
    ]
    scratch = [
        pltpu.VMEM((T + CONV_PAD, D_RNN), F32),
        pltpu.VMEM((T, D_RNN), F32),
        pltpu.VMEM((1, D_RNN), F32),
    ]
    return pl.pallas_call(
        kern, grid=(B, nt), in_specs=in_specs, out_specs=out_specs, out_shape=out_shape,
        scratch_shapes=scratch, compiler_params=_cparams(("arbitrary", "arbitrary")),
        name="mixer_in",
    )(x, sh1, sc1, W["g_pre1"], W["w_in2"], conv_state, lru_state, W["w_conv"], W["b_conv"],
      W["w_gates"], W["b_rgate"], W["b_igate"], W["lam"], W["w_rnn_out"], W["g_q"], W["w_qup"],
      W["g_kv"], rope_c, rope_s)


def _kv_up_kernel(ckv_ref, kpe_ref, wk_ref, wv_ref, k_ref, v_ref, *, v_transposed):
    c = ckv_ref[...].astype(BF16)
    kn = _dot(c, wk_ref[...])
    kpe = kpe_ref[...]
    for hd in range(N_HEADS):
        base = hd * HEAD_K
        k_ref[:, base:base + QK_NOPE] = kn[:, hd * QK_NOPE:(hd + 1) * QK_NOPE].astype(BF16)
        k_ref[:, base + QK_NOPE:base + HEAD_K] = kpe
    if v_transposed:
        vt = lax.dot_general(wv_ref[...], c, (((1,), (1,)), ((), ())), preferred_element_type=F32)
        for hd in range(N_HEADS):
            v_ref[0, hd] = vt[hd * V_HEAD:(hd + 1) * V_HEAD, :].astype(BF16)
    else:
        v_ref[...] = _dot(c, wv_ref[...]).astype(BF16)


def _kv_up(ckv, kpe128, wk, wv, T, v_transposed):
    R = ckv.shape[0]
    row = lambda i: (i, 0)
    if v_transposed:
        v_spec = pl.BlockSpec((1, N_HEADS, V_HEAD, T), lambda i: (i, 0, 0, 0))
        v_shape = jax.ShapeDtypeStruct((R // T, N_HEADS, V_HEAD, T), BF16)
    else:
        v_spec = pl.BlockSpec((T, N_HEADS * V_HEAD), row)
        v_shape = jax.ShapeDtypeStruct((R, N_HEADS * V_HEAD), BF16)
    return pl.pallas_call(
        functools.partial(_kv_up_kernel, v_transposed=v_transposed), grid=(R // T,),
        in_specs=[pl.BlockSpec((T, KV_LORA), row), pl.BlockSpec((T, LANES), row),
                  _const_spec(wk.shape), _const_spec(wv.shape)],
        out_specs=[pl.BlockSpec((T, N_HEADS * HEAD_K), row), v_spec],
        out_shape=[jax.ShapeDtypeStruct((R, N_HEADS * HEAD_K), BF16), v_shape],
        compiler_params=_cparams(("arbitrary",)),
        name="kv_up",
    )(ckv, kpe128, wk, wv)


def _attn_prompt_kernel(q_ref, k_ref, vt_ref, o_ref, qt_scr, s0, s1, m_scr, l_scr, acc_scr, *, QB, HP):
    qi = pl.program_id(2)
    for hh in range(HP):
        qt_scr[hh] = q_ref[0, :, hh * HEAD_K:(hh + 1) * HEAD_K].T
    m_scr[...] = jnp.full(m_scr.shape, NEG, F32)
    l_scr[...] = jnp.zeros(l_scr.shape, F32)
    acc_scr[...] = jnp.zeros(acc_scr.shape, F32)

    def scores(j, dst):
        start = pl.multiple_of(j * QB, QB)
        for hh in range(HP):
            dst[hh] = _dot(k_ref[0, pl.ds(start, QB), hh * HEAD_K:(hh + 1) * HEAD_K], qt_scr[hh])

    def update(j, src, masked):
        for hh in range(HP):
            s = src[hh]
            if masked:
                ck = lax.broadcasted_iota(jnp.int32, (QB, QB), 0) // CHUNK
                cq = lax.broadcasted_iota(jnp.int32, (QB, QB), 1) // CHUNK
                s = jnp.where(ck <= cq, s, NEG)
            m_old = m_scr[hh]
            m_new = jnp.maximum(m_old, jnp.max(s, axis=0, keepdims=True))
            p = jnp.exp2(s - m_new)
            alpha = jnp.exp2(m_old - m_new)
            l_scr[hh] = alpha * l_scr[hh] + jnp.sum(p, axis=0, keepdims=True)
            acc_scr[hh] = alpha * acc_scr[hh] + _dot(vt_ref[j, hh], p.astype(BF16))
            m_scr[hh] = m_new

    scores(0, s0)

    def pair(jj, c):
        j = 2 * jj
        scores(j + 1, s1)
        update(j, s0, False)
        scores(j + 2, s0)
        update(j + 1, s1, False)
        return c

    lax.fori_loop(0, qi // 2, pair, 0)

    @pl.when(qi % 2 == 0)
    def _():
        update(qi, s0, True)

    @pl.when(qi % 2 == 1)
    def _():
        scores(qi, s1)
        update(qi - 1, s0, False)
        update(qi, s1, True)

    for hh in range(HP):
        o_ref[0, :, hh * V_HEAD:(hh + 1) * V_HEAD] = (acc_scr[hh] / l_scr[hh]).T.astype(BF16)


def _attn_prompt(q, k, vt, QB, HP):
    B, S, _ = q.shape
    nkb = S // QB
    kern = functools.partial(_attn_prompt_kernel, QB=QB, HP=HP)
    return pl.pallas_call(
        kern, grid=(B, N_HEADS // HP, S // QB),
        in_specs=[pl.BlockSpec((1, QB, HP * HEAD_K), lambda b, h, i: (b, i, h)),
                  pl.BlockSpec((1, S, HP * HEAD_K), lambda b, h, i: (b, 0, h)),
                  pl.BlockSpec((nkb, HP, V_HEAD, QB), lambda b, h, i: (b, h, 0, 0))],
        out_specs=pl.BlockSpec((1, QB, HP * V_HEAD), lambda b, h, i: (b, i, h)),
        out_shape=jax.ShapeDtypeStruct((B, S, N_HEADS * V_HEAD), BF16),
        scratch_shapes=[pltpu.VMEM((HP, HEAD_K, QB), BF16),
                        pltpu.VMEM((HP, QB, QB), F32), pltpu.VMEM((HP, QB, QB), F32),
                        pltpu.VMEM((HP, 1, QB), F32),
                        pltpu.VMEM((HP, 1, QB), F32), pltpu.VMEM((HP, V_HEAD, QB), F32)],
        compiler_params=_cparams(("arbitrary", "arbitrary", "arbitrary")),
        name="attn_prompt",
    )(q, k, vt)


def _attn_sample_kernel(q_ref, k_ref, v_ref, o_ref):
    for hd in range(N_HEADS):
        q = q_ref[0, :, hd * HEAD_K:(hd + 1) * HEAD_K]
        k = k_ref[0, :, hd * HEAD_K:(hd + 1) * HEAD_K]
        s = lax.dot_general(q, k, (((1,), (1,)), ((), ())), preferred_element_type=F32)
        p = jnp.exp2(s - jnp.max(s, axis=1, keepdims=True))
        l = jnp.sum(p, axis=1, keepdims=True)
        o = _dot(p.astype(BF16), v_ref[0, :, hd * V_HEAD:(hd + 1) * V_HEAD])
        o_ref[0, :, hd * V_HEAD:(hd + 1) * V_HEAD] = (o / l).astype(BF16)


def _attn_sample(q, k, v):
    B, S, _ = q.shape
    LK = k.shape[1]
    b3 = lambda b: (b, 0, 0)
    return pl.pallas_call(
        _attn_sample_kernel, grid=(B,),
        in_specs=[pl.BlockSpec((1, S, N_HEADS * HEAD_K), b3),
                  pl.BlockSpec((1, LK, N_HEADS * HEAD_K), b3),
                  pl.BlockSpec((1, LK, N_HEADS * V_HEAD), b3)],
        out_specs=pl.BlockSpec((1, S, N_HEADS * V_HEAD), b3),
        out_shape=jax.ShapeDtypeStruct((B, S, N_HEADS * V_HEAD), BF16),
        compiler_params=_cparams(("arbitrary",)),
        name="attn_sample",
    )(q, k, v)


def _lane_min_where(mask, lane):
    return jnp.min(jnp.where(mask, lane, LANES), axis=1, keepdims=True)


def _merge_kernel(o_ref, ma_ref, sgb_ref, x_ref, gt1_ref, sh2_ref, sc2_ref, gpost1_ref, gpre2_ref,
                  wao_ref, wout_ref, wr_ref, br_ref, x1_ref, h2_ref, route_ref, cnt_ref, *, T):
    nc = T // CHUNK
    y_b = _dot(o_ref[...], wao_ref[...])
    m = ma_ref[...].astype(F32) + sgb_ref[...].astype(F32) * y_b
    y = _dot(m.astype(BF16), wout_ref[...])
    yn = _rms(y, gpost1_ref[...]).reshape(nc, CHUNK, D_MODEL)
    x1 = x_ref[...].reshape(nc, CHUNK, D_MODEL) + gt1_ref[...] * yn
    x1_ref[...] = x1.reshape(T, D_MODEL)
    h2 = (_rms(x1, gpre2_ref[...]) * (1.0 + sc2_ref[...]) + sh2_ref[...]).reshape(T, D_MODEL)
    h2b = h2.astype(BF16)
    h2_ref[...] = h2b

    logits = _dot(h2b, wr_ref[...]) + br_ref[...]
    lane = lax.broadcasted_iota(jnp.int32, (T, LANES), 1)
    is_g = lane < N_GROUPS
    gmax = jnp.max(jnp.where(is_g, logits, NEG), axis=1, keepdims=True)
    gidx = _lane_min_where(is_g & (logits == gmax), lane)
    gsum = jnp.sum(jnp.where(is_g, jnp.exp(logits - gmax), 0.0), axis=1, keepdims=True)
    lo = ROUTE_OFF + EXP_PER_GROUP * gidx
    sel = (lane >= lo) & (lane < lo + EXP_PER_GROUP)
    m1 = jnp.max(jnp.where(sel, logits, NEG), axis=1, keepdims=True)
    i1 = _lane_min_where(sel & (logits == m1), lane)
    rest = sel & (lane != i1)
    m2 = jnp.max(jnp.where(rest, logits, NEG), axis=1, keepdims=True)
    i2 = _lane_min_where(rest & (logits == m2), lane)
    e2 = jnp.exp(m2 - m1)
    w1 = 1.0 / (gsum * (1.0 + e2))
    w2 = w1 * e2
    uses = jnp.where((lane == i1) | (lane == i2), 1.0, 0.0)
    earlier = lax.broadcasted_iota(jnp.int32, (T, T), 1) < lax.broadcasted_iota(jnp.int32, (T, T), 0)
    rank = _dot(jnp.where(earlier, 1.0, 0.0).astype(BF16), uses.astype(BF16))
    cnt_ref[0] = (rank[T - 1:T, :] + uses[T - 1:T, :]).astype(jnp.int32)
    rank1 = jnp.sum(jnp.where(lane == i1, rank, 0.0), axis=1, keepdims=True)
    rank2 = jnp.sum(jnp.where(lane == i2, rank, 0.0), axis=1, keepdims=True)
    fields = (i1.astype(F32), i2.astype(F32), w1, w2, rank1, rank2)
    row = jnp.zeros((T, LANES), F32)
    for f, val in enumerate(fields):
        row = jnp.where(lane == f, val, row)
    route_ref[...] = row


def _merge(o, ma, sgb, x, gt1c, sh2c, sc2c, W, T):
    N = x.shape[0]
    nc = T // CHUNK
    row = lambda i: (i, 0)
    chunk = lambda i: (i, 0, 0)
    kern = functools.partial(_merge_kernel, T=T)
    return pl.pallas_call(
        kern, grid=(N // T,),
        in_specs=[pl.BlockSpec((T, D_MODEL), row), pl.BlockSpec((T, D_MODEL), row),
                  pl.BlockSpec((T, D_MODEL), row), pl.BlockSpec((T, D_MODEL), row),
                  pl.BlockSpec((nc, 1, D_MODEL), chunk), pl.BlockSpec((nc, 1, D_MODEL), chunk),
                  pl.BlockSpec((nc, 1, D_MODEL), chunk),
                  _const_spec((1, D_MODEL)), _const_spec((1, D_MODEL)),
                  _const_spec((D_MODEL, D_MODEL)), _const_spec((D_MODEL, D_MODEL)),
                  _const_spec((D_MODEL, LANES)), _const_spec((1, LANES))],
        out_specs=[pl.BlockSpec((T, D_MODEL), row), pl.BlockSpec((T, D_MODEL), row),
                   pl.BlockSpec((T, LANES), row), pl.BlockSpec((1, 1, LANES), chunk)],
        out_shape=[jax.ShapeDtypeStruct((N, D_MODEL), F32), jax.ShapeDtypeStruct((N, D_MODEL), BF16),
                   jax.ShapeDtypeStruct((N, LANES), F32), jax.ShapeDtypeStruct((N // T, 1, LANES), jnp.int32)],
        compiler_params=_cparams(("arbitrary",)),
        name="merge",
    )(o, ma, sgb, x, gt1c, sh2c, sc2c, W["g_post1"], W["g_pre2"], W["w_attn_out"], W["w_out"],
      W["w_route"], W["b_route"])


def _moe_kernel(cnt_ref, h_ref, route_ref, x1_ref, gt2_ref, gpost2_ref, wg_ref, wu_ref, wd_ref, y_ref,
                key_l, key_s, comb_w, *, T, TR, CH):
    i = pl.program_id(0)
    e = pl.program_id(1)
    nc = T // CHUNK
    nr = T // TR

    def keys(sel, rank, tok):
        expert = sel.astype(jnp.int32) - ROUTE_OFF
        rank = rank.astype(jnp.int32)
        for r in range(1, nr):
            before = jnp.zeros_like(rank)
            for x in range(N_EXPERTS):
                before = jnp.where(expert == x, cnt_ref[(i * nr + r - 1) * N_EXPERTS + x], before)
            rank = rank + jnp.where(tok >= r * TR, before, 0)
        return expert * KEY_STRIDE + rank

    @pl.when(e == 0)
    def _():
        y_ref[...] = jnp.zeros(y_ref.shape, F32)
        rt = route_ref[...].T
        tok_l = lax.broadcasted_iota(jnp.int32, (1, T), 1)
        tok_s = lax.broadcasted_iota(jnp.int32, (T, 1), 0)
        key_l[0:1, :] = keys(rt[0:1, :], rt[4:5, :], tok_l)
        key_l[1:2, :] = keys(rt[1:2, :], rt[5:6, :], tok_l)
        key_s[0] = keys(route_ref[:, 0:1], route_ref[:, 4:5], tok_s)
        key_s[1] = keys(route_ref[:, 1:2], route_ref[:, 5:6], tok_s)
        key_s[2] = route_ref[:, 0:1].astype(jnp.int32) - ROUTE_OFF
        comb_w[0] = route_ref[:, 2:3]
        comb_w[1] = route_ref[:, 3:4]

    total = cnt_ref[(i * nr) * N_EXPERTS + e]
    for r in range(1, nr):
        total = total + cnt_ref[(i * nr + r) * N_EXPERTS + e]
    k1_l, k2_l = key_l[0:1, :], key_l[1:2, :]
    k1_s, k2_s = key_s[0], key_s[1]
    comb_s = jnp.where(key_s[2] == e, comb_w[0], comb_w[1])

    def chunk(c, carry):
        base = e * KEY_STRIDE + c * CH
        want_s = base + lax.broadcasted_iota(jnp.int32, (CH, 1), 0)
        pick = (k1_l == want_s) | (k2_l == want_s)
        xg = _dot(jnp.where(pick, 1.0, 0.0).astype(BF16), h_ref[...]).astype(BF16)
        g = _dot(xg, wg_ref[0])
        u = _dot(xg, wu_ref[0])
        hid = (g * jax.nn.sigmoid(g) * u).astype(BF16)
        out = _dot(hid, wd_ref[0]).astype(BF16)
        want_l = base + lax.broadcasted_iota(jnp.int32, (1, CH), 1)
        put = (k1_s == want_l) | (k2_s == want_l)
        y_ref[...] += comb_s * _dot(jnp.where(put, 1.0, 0.0).astype(BF16), out)
        return carry

    lax.fori_loop(0, (total + CH - 1) // CH, chunk, 0)

    @pl.when(e == N_EXPERTS - 1)
    def _():
        on = _rms(y_ref[...], gpost2_ref[...]).reshape(nc, CHUNK, D_MODEL)
        y = x1_ref[...].reshape(nc, CHUNK, D_MODEL) + gt2_ref[...] * on
        y_ref[...] = y.reshape(T, D_MODEL)


def _moe(h2, route, cnt, x1, gt2c, W, T, TR, CH):
    N = h2.shape[0]
    nc = T // CHUNK
    row = lambda i, e, c: (i, 0)
    wsel = lambda i, e, c: (e, 0, 0)
    kern = functools.partial(_moe_kernel, T=T, TR=TR, CH=CH)
    grid_spec = pltpu.PrefetchScalarGridSpec(
        num_scalar_prefetch=1, grid=(N // T, N_EXPERTS),
        in_specs=[pl.BlockSpec((T, D_MODEL), row), pl.BlockSpec((T, LANES), row),
                  pl.BlockSpec((T, D_MODEL), row),
                  pl.BlockSpec((nc, 1, D_MODEL), lambda i, e, c: (i, 0, 0)),
                  pl.BlockSpec((1, D_MODEL), lambda i, e, c: (0, 0)),
                  pl.BlockSpec((1, D_MODEL, D_EXPERT), wsel),
                  pl.BlockSpec((1, D_MODEL, D_EXPERT), wsel),
                  pl.BlockSpec((1, D_EXPERT, D_MODEL), wsel)],
        out_specs=pl.BlockSpec((T, D_MODEL), row),
        scratch_shapes=[pltpu.VMEM((8, T), jnp.int32), pltpu.VMEM((3, T, 1), jnp.int32),
                        pltpu.VMEM((2, T, 1), F32)])
    return pl.pallas_call(
        kern, grid_spec=grid_spec,
        out_shape=jax.ShapeDtypeStruct((N, D_MODEL), F32),
        compiler_params=_cparams(("arbitrary", "arbitrary")),
        name="moe",
    )(cnt, h2, route, x1, gt2c, W["g_post2"], W["w_exp_gate"], W["w_exp_up"], W["w_exp_down"])


def _rotate_half_cols(w):
    half = ROPE_DIM // 2
    return jnp.concatenate([-w[..., half:], w[..., :half]], axis=-1)


def _rope_tables(pos):
    inv = ROPE_THETA ** (-jnp.arange(0, ROPE_DIM, 2, dtype=F32) / ROPE_DIM)
    ang = pos.astype(F32)[:, None] * inv
    z = jnp.zeros((pos.shape[0], LANES - ROPE_DIM), F32)
    c, s = jnp.cos(ang), jnp.sin(ang)
    return jnp.concatenate([c, c, z], axis=1), jnp.concatenate([s, s, z], axis=1)


def _chunk_rows(v, seq):
    B, D = v.shape
    return jnp.broadcast_to(v[:, None, None, :], (B, seq // CHUNK, 1, D)).reshape(B * (seq // CHUNK), 1, D)


def _layer(x, ada, ckv_past, kpe_past, conv_state, lru_state, pos0, W, T_in, T_tok):
    B, S, _ = x.shape
    L = 0 if ckv_past is None else ckv_past.shape[1]
    sh1, sc1, gt1, sh2, sc2, gt2 = jnp.split(ada, 6, axis=-1)
    rope_c, rope_s = _rope_tables(pos0 + jnp.arange(S))
    ma, sgb, q, ckv, kpe, kpe128, conv_new, lru_new = _mixer_in(
        x, sh1[:, None, :], sc1[:, None, :], conv_state, lru_state[:, None, :], rope_c, rope_s, W, T_in)

    if L == 0:
        k, vt = _kv_up(ckv.reshape(B * S, KV_LORA), kpe128.reshape(B * S, LANES), W["w_k_up"], W["w_v_up_t"],
                       ATTN_BLOCK, True)
        o = _attn_prompt(q, k.reshape(B, S, -1), vt, ATTN_BLOCK, ATTN_HEADS_PER_STEP)
    else:
        ckv_all = jnp.concatenate([ckv_past, ckv], axis=1)
        kpe_past128 = jnp.pad(kpe_past, ((0, 0), (0, 0), (0, LANES - ROPE_DIM))).astype(BF16)
        kpe_all = jnp.concatenate([kpe_past128, kpe128], axis=1)
        LK = L + S
        k, v = _kv_up(ckv_all.reshape(B * LK, KV_LORA), kpe_all.reshape(B * LK, LANES), W["w_k_up"], W["w_v_up"],
                      256, False)
        o = _attn_sample(q, k.reshape(B, LK, -1), v.reshape(B, LK, -1))

    N = B * S
    x1, h2, route, cnt = _merge(o.reshape(N, D_MODEL), ma.reshape(N, D_MODEL), sgb.reshape(N, D_MODEL),
                                x.reshape(N, D_MODEL), _chunk_rows(gt1, S), _chunk_rows(sh2, S),
                                _chunk_rows(sc2, S), W, T_tok)
    cnt = cnt[:, 0, ROUTE_OFF:ROUTE_OFF + N_EXPERTS].reshape(-1)
    y = _moe(h2, route, cnt, x1, _chunk_rows(gt2, S), W, MOE_TILE, T_tok, MOE_CHUNK)
    return y.reshape(B, S, D_MODEL), ckv, kpe, conv_new, lru_new.reshape(B, D_RNN)


def kernel(x_prompt, x_sample, c_prompt, c_sample, cache_ckv, cache_kpe, state_conv, state_rglru, w_ada, b_ada, g_pre1, g_post1, g_pre2, g_post2, w_in, w_conv, b_conv, w_rgate, b_rgate, w_igate, b_igate, lru_lambda, w_rnn_out, g_q_lat, w_q_up, g_kv_lat, w_k_up, w_v_up, w_attn_out, w_out, w_group, b_group, w_erouter, b_erouter, w_exp_gate, w_exp_up, w_exp_down):
    assert w_in.shape[0] == 1, "single-layer trunk"
    B = x_prompt.shape[0]
    wi = w_in[0]
    sp = lambda a, b: wi[:, a:b]
    xr, gr = sp(0, D_RNN), sp(D_RNN, 2 * D_RNN)
    o = 2 * D_RNN
    ql, kvl, kr = sp(o, o + Q_LORA), sp(o + Q_LORA, o + Q_LORA + KV_LORA), \
        sp(o + Q_LORA + KV_LORA, o + Q_LORA + KV_LORA + ROPE_DIM)
    o = o + Q_LORA + KV_LORA + ROPE_DIM
    ga, gb = sp(o, o + D_MODEL), sp(o + D_MODEL, o + 2 * D_MODEL)
    wq = w_q_up[0].reshape(Q_LORA, N_HEADS, QK_NOPE + ROPE_DIM)
    wq_pe = wq[..., QK_NOPE:]
    row = lambda a: a[0].reshape(1, -1)
    W = {
        "g_pre1": row(g_pre1), "g_post1": row(g_post1), "g_pre2": row(g_pre2), "g_post2": row(g_post2),
        "w_in2": jnp.concatenate([xr, gr, ql, kvl, kr, _rotate_half_cols(kr), ga, gb], axis=1).astype(BF16),
        "w_conv": w_conv[0], "b_conv": row(b_conv),
        "w_gates": jnp.concatenate([w_rgate[0], w_igate[0]], axis=-1).astype(BF16),
        "b_rgate": row(b_rgate), "b_igate": row(b_igate), "lam": row(lru_lambda),
        "w_rnn_out": w_rnn_out[0].astype(BF16),
        "g_q": row(g_q_lat), "g_kv": row(g_kv_lat),
        "w_qup": jnp.concatenate([wq[..., :QK_NOPE], wq_pe, _rotate_half_cols(wq_pe)], axis=-1)
                 .reshape(Q_LORA, N_HEADS * HEAD_K).astype(BF16),
        "w_k_up": w_k_up[0].astype(BF16), "w_v_up": w_v_up[0].astype(BF16),
        "w_v_up_t": w_v_up[0].T.astype(BF16),
        "w_attn_out": w_attn_out[0].astype(BF16), "w_out": w_out[0].astype(BF16),
        "w_route": jnp.pad(jnp.concatenate([w_group[0], w_erouter[0]], axis=1),
                           ((0, 0), (0, LANES - N_GROUPS - N_EXPERTS))).astype(BF16),
        "b_route": jnp.pad(jnp.concatenate([b_group[0], b_erouter[0]]), (0, LANES - N_GROUPS - N_EXPERTS))
                   .reshape(1, LANES),
        "w_exp_gate": w_exp_gate[0].astype(BF16), "w_exp_up": w_exp_up[0].astype(BF16),
        "w_exp_down": w_exp_down[0].astype(BF16),
    }
    ada = _ada(jnp.concatenate([c_prompt, c_sample], axis=0), w_ada[0], b_ada[0])
    zeros_conv = jnp.zeros((B, CONV_W - 1, D_RNN), F32)
    zeros_lru = jnp.zeros((B, D_RNN), F32)
    yp, ckv_p, kpe_p, conv_p, lru_p = _layer(x_prompt, ada[:B], None, None, zeros_conv, zeros_lru, 0, W, 512, 512)
    ys, ckv_s, kpe_s, conv_s, lru_s = _layer(x_sample, ada[B:], cache_ckv[0], cache_kpe[0], state_conv[0],
                                             state_rglru[0], cache_ckv.shape[2], W, 64, 512)
    return (yp, ys, ckv_p[None], kpe_p[None], conv_p[None], lru_p[None],
            ckv_s[None], kpe_s[None], conv_s[None], lru_s[None])
```

```python
import functools

import jax
import jax.numpy as jnp
from jax import lax
from jax.experimental import pallas as pl
from jax.experimental.pallas import tpu as pltpu

F32 = jnp.float32
BF16 = jnp.bfloat16

D_MODEL = 1024
CHUNK = 64
D_RNN = 1024
N_RNN_BLOCKS = 8
RNN_BLOCK = D_RNN // N_RNN_BLOCKS
CONV_W = 4
LRU_C = 8.0
N_HEADS = 8
QK_NOPE = 128
ROPE_DIM = 64
V_HEAD = 128
Q_LORA = 384
KV_LORA = 256
ROPE_THETA = 10000.0
SM_SCALE = (QK_NOPE + ROPE_DIM) ** -0.5
LOG2E = 1.4426950408889634
Q_SCALE = SM_SCALE * LOG2E
N_GROUPS = 4
EXP_PER_GROUP = 4
N_EXPERTS = N_GROUPS * EXP_PER_GROUP
D_EXPERT = 512
EPS = 1e-6

LANES = 128
SUBLANES = 8
HEAD_K = QK_NOPE + 2 * ROPE_DIM
OFF_XR = 0
OFF_GR = OFF_XR + D_RNN
OFF_QL = OFF_GR + D_RNN
OFF_KVL = OFF_QL + Q_LORA
OFF_KR = OFF_KVL + KV_LORA
OFF_GA = OFF_KR + 2 * ROPE_DIM
OFF_GB = OFF_GA + D_MODEL
IN_COLS2 = OFF_GB + D_MODEL
ROUTE_OFF = N_GROUPS
CONV_PAD = 8
NEG = -1e30
ATTN_BLOCK = 512
MOE_TILE = 1024
KEY_STRIDE = 2048
MOE_EXPERTS_PER_STEP = 2
MOE_CHUNK = 160
ATTN_HEADS_PER_STEP = 2
VMEM_LIMIT = 56 * 1024 * 1024


def _cparams(sem):
    return pltpu.CompilerParams(dimension_semantics=sem, vmem_limit_bytes=VMEM_LIMIT)


def _const_spec(shape):
    n = len(shape)
    return pl.BlockSpec(shape, lambda *_: (0,) * n, pipeline_mode=pl.Buffered(1))


def _rms(x, g):
    return x * lax.rsqrt(jnp.mean(x * x, axis=-1, keepdims=True) + EPS) * g


def _dot(a, b):
    return jnp.dot(a, b, preferred_element_type=F32)


def _rope(v, c, s):
    return v * c + pltpu.roll(v, ROPE_DIM, axis=1) * s


def _ada_kernel(c_ref, w_ref, b_ref, o_ref):
    c = c_ref[...]
    s = c * jax.nn.sigmoid(c)
    o_ref[...] = jnp.dot(s, w_ref[...], preferred_element_type=F32,
                         precision=lax.Precision.HIGHEST) + b_ref[...]


def _ada(c, w, b):
    nb = c.shape[0]
    n = w.shape[1]
    bn = n // 6
    return pl.pallas_call(
        _ada_kernel,
        grid=(n // bn,),
        in_specs=[_const_spec((nb, D_MODEL)),
                  pl.BlockSpec((D_MODEL, bn), lambda j: (0, j)),
                  pl.BlockSpec((1, bn), lambda j: (0, j))],
        out_specs=pl.BlockSpec((nb, bn), lambda j: (0, j)),
        out_shape=jax.ShapeDtypeStruct((nb, n), F32),
        compiler_params=_cparams(("arbitrary",)),
        name="ada",
    )(c, w, b.reshape(1, n))


def _mixer_in_kernel(x_ref, sh_ref, sc_ref, gpre_ref, win_ref, cst_ref, lst_ref, wconv_ref, bconv_ref,
                     wgate_ref, br_ref, bi_ref, lam_ref, wrnn_ref, gq_ref, wqup_ref, gkv_ref,
                     rc_ref, rs_ref,
                     ma_ref, sgb_ref, q_ref, ckv_ref, kpe_ref, kpe128_ref, cout_ref, lout_ref,
                     xbuf, b_scr, hcar, *, T):
    t = pl.program_id(1)

    @pl.when(t == 0)
    def _():
        xbuf[CONV_PAD - (CONV_W - 1):CONV_PAD, :] = cst_ref[0]
        hcar[...] = lst_ref[0]

    x = x_ref[0]
    h = _rms(x, gpre_ref[...]) * (1.0 + sc_ref[0]) + sh_ref[0]
    hb = h.astype(BF16)

    xbuf[CONV_PAD:CONV_PAD + T, :] = _dot(hb, win_ref[:, OFF_XR:OFF_XR + D_RNN])
    xc = bconv_ref[...]
    for k in range(CONV_W):
        lo = CONV_PAD - (CONV_W - 1) + k
        xc = xc + xbuf[lo:lo + T, :] * wconv_ref[k:k + 1, :]
    tail = xbuf[T + CONV_PAD - (CONV_W - 1):T + CONV_PAD, :]
    cout_ref[0] = tail
    xbuf[CONV_PAD - (CONV_W - 1):CONV_PAD, :] = tail

    lam = lam_ref[...]
    softplus_neg_lam = jnp.maximum(-lam, 0.0) + jnp.log1p(jnp.exp(-jnp.abs(lam)))
    row_in_group = lax.broadcasted_iota(jnp.int32, (T, RNN_BLOCK), 0) % SUBLANES
    for n in range(N_RNN_BLOCKS):
        blk = slice(n * RNN_BLOCK, (n + 1) * RNN_BLOCK)
        xcb = xc[:, blk]
        g = _dot(xcb.astype(BF16), wgate_ref[n])
        r = jax.nn.sigmoid(g[:, :RNN_BLOCK] + br_ref[:, blk])
        i = jax.nn.sigmoid(g[:, RNN_BLOCK:] + bi_ref[:, blk])
        log_a = -LRU_C * r * softplus_neg_lam[:, blk]
        a = jnp.exp(log_a)
        b = jnp.sqrt(-jnp.tanh(log_a) * (a * a + 1.0)) * (i * xcb)
        for d in (1, 2, 4):
            keep = row_in_group >= d
            a_prev = jnp.where(keep, pltpu.roll(a, d, axis=0), 1.0)
            b_prev = jnp.where(keep, pltpu.roll(b, d, axis=0), 0.0)
            b = b + a * b_prev
            a = a * a_prev
        hprev = hcar[:, blk]
        for grp in range(T // SUBLANES):
            rows = slice(grp * SUBLANES, (grp + 1) * SUBLANES)
            hg = b[rows, :] + a[rows, :] * hprev
            b_scr[rows, blk] = hg
            hprev = hg[SUBLANES - 1:SUBLANES, :]
        hcar[:, blk] = hprev
    lout_ref[0] = hcar[...]

    gr = _dot(hb, win_ref[:, OFF_GR:OFF_GR + D_RNN])
    y_a = _dot((b_scr[...] * jax.nn.gelu(gr)).astype(BF16), wrnn_ref[...])
    ga = _dot(hb, win_ref[:, OFF_GA:OFF_GA + D_MODEL])
    ma_ref[0] = (jax.nn.sigmoid(ga) * y_a).astype(BF16)
    gb = _dot(hb, win_ref[:, OFF_GB:OFF_GB + D_MODEL])
    sgb_ref[0] = jax.nn.sigmoid(gb).astype(BF16)

    rc = rc_ref[...]
    rs = rs_ref[...]
    ql = _dot(hb, win_ref[:, OFF_QL:OFF_QL + Q_LORA])
    q = _dot(_rms(ql, gq_ref[...]).astype(BF16), wqup_ref[...])
    for hd in range(N_HEADS):
        base = hd * HEAD_K
        q_ref[0, :, base:base + QK_NOPE] = (q[:, base:base + QK_NOPE] * Q_SCALE).astype(BF16)
        pe = _rope(q[:, base + QK_NOPE:base + HEAD_K], rc, rs)
        q_ref[0, :, base + QK_NOPE:base + HEAD_K] = (pe * Q_SCALE).astype(BF16)
    kvl = _dot(hb, win_ref[:, OFF_KVL:OFF_KVL + KV_LORA])
    ckv_ref[0] = _rms(kvl, gkv_ref[...])
    kp = _rope(_dot(hb, win_ref[:, OFF_KR:OFF_KR + 2 * ROPE_DIM]), rc, rs)
    kpe_ref[0] = kp[:, :ROPE_DIM]
    kpe128_ref[0] = kp.astype(BF16)


def _mixer_in(x, sh1, sc1, conv_state, lru_state, rope_c, rope_s, W, T):
    B, S, _ = x.shape
    nt = S // T
    kern = functools.partial(_mixer_in_kernel, T=T)
    bt = lambda b, t: (b, t, 0)
    bo = lambda b, t: (b, 0, 0)
    tt = lambda b, t: (t, 0)
    in_specs = [
        pl.BlockSpec((1, T, D_MODEL), bt),
        pl.BlockSpec((1, 1, D_MODEL), bo),
        pl.BlockSpec((1, 1, D_MODEL), bo),
        _const_spec((1, D_MODEL)),
        _const_spec((D_MODEL, IN_COLS2)),
        pl.BlockSpec((1, CONV_W - 1, D_RNN), bo),
        pl.BlockSpec((1, 1, D_RNN), bo),
        _const_spec((CONV_W, D_RNN)),
        _const_spec((1, D_RNN)),
        _const_spec((N_RNN_BLOCKS, RNN_BLOCK, 2 * RNN_BLOCK)),
        _const_spec((1, D_RNN)),
        _const_spec((1, D_RNN)),
        _const_spec((1, D_RNN)),
        _const_spec((D_RNN, D_MODEL)),
        _const_spec((1, Q_LORA)),
        _const_spec((Q_LORA, N_HEADS * HEAD_K)),
        _const_spec((1, KV_LORA)),
        pl.BlockSpec((T, LANES), tt),
        pl.BlockSpec((T, LANES), tt),
    ]
    out_specs = [
        pl.BlockSpec((1, T, D_MODEL), bt),
        pl.BlockSpec((1, T, D_MODEL), bt),
        pl.BlockSpec((1, T, N_HEADS * HEAD_K), bt),
        pl.BlockSpec((1, T, KV_LORA), bt),
        pl.BlockSpec((1, T, ROPE_DIM), bt),
        pl.BlockSpec((1, T, LANES), bt),
        pl.BlockSpec((1, CONV_W - 1, D_RNN), bo),
        pl.BlockSpec((1, 1, D_RNN), bo),
    ]
    out_shape = [
        jax.ShapeDtypeStruct((B, S, D_MODEL), BF16),
        jax.ShapeDtypeStruct((B, S, D_MODEL), BF16),
        jax.ShapeDtypeStruct((B, S, N_HEADS * HEAD_K), BF16),
        jax.ShapeDtypeStruct((B, S, KV_LORA), F32),
        jax.ShapeDtypeStruct((B, S, ROPE_DIM), F32),
        jax.ShapeDtypeStruct((B, S, LANES), BF16),
        jax.ShapeDtypeStruct((B, CONV_W - 1, D_RNN), F32),
        jax.ShapeDtypeStruct((B, 1, D_RNN), F32),
    ]
    scratch = [
        pltpu.VMEM((T + CONV_PAD, D_RNN), F32),
        pltpu.VMEM((T, D_RNN), F32),
        pltpu.VMEM((1, D_RNN), F32),
    ]
    return pl.pallas_call(
        kern, grid=(B, nt), in_specs=in_specs, out_specs=out_specs, out_shape=out_shape,
        scratch_shapes=scratch, compiler_params=_cparams(("arbitrary", "arbitrary")),
        name="mixer_in",
    )(x, sh1, sc1, W["g_pre1"], W["w_in2"], conv_state, lru_state, W["w_conv"], W["b_conv"],
      W["w_gates"], W["b_rgate"], W["b_igate"], W["lam"], W["w_rnn_out"], W["g_q"], W["w_qup"],
      W["g_kv"], rope_c, rope_s)


def _kv_up_kernel(ckv_ref, kpe_ref, wk_ref, wv_ref, k_ref, v_ref, *, v_transposed):
    c = ckv_ref[...].astype(BF16)
    kn = _dot(c, wk_ref[...])
    kpe = kpe_ref[...]
    for hd in range(N_HEADS):
        base = hd * HEAD_K
        k_ref[:, base:base + QK_NOPE] = kn[:, hd * QK_NOPE:(hd + 1) * QK_NOPE].astype(BF16)
        k_ref[:, base + QK_NOPE:base + HEAD_K] = kpe
    if v_transposed:
        vt = lax.dot_general(wv_ref[...], c, (((1,), (1,)), ((), ())), preferred_element_type=F32)
        for hd in range(N_HEADS):
            v_ref[0, hd] = vt[hd * V_HEAD:(hd + 1) * V_HEAD, :].astype(BF16)
    else:
        v_ref[...] = _dot(c, wv_ref[...]).astype(BF16)


def _kv_up(ckv, kpe128, wk, wv, T, v_transposed):
    R = ckv.shape[0]
    row = lambda i: (i, 0)
    if v_transposed:
        v_spec = pl.BlockSpec((1, N_HEADS, V_HEAD, T), lambda i: (i, 0, 0, 0))
        v_shape = jax.ShapeDtypeStruct((R // T, N_HEADS, V_HEAD, T), BF16)
    else:
        v_spec = pl.BlockSpec((T, N_HEADS * V_HEAD), row)
        v_shape = jax.ShapeDtypeStruct((R, N_HEADS * V_HEAD), BF16)
    return pl.pallas_call(
        functools.partial(_kv_up_kernel, v_transposed=v_transposed), grid=(R // T,),
        in_specs=[pl.BlockSpec((T, KV_LORA), row), pl.BlockSpec((T, LANES), row),
                  _const_spec(wk.shape), _const_spec(wv.shape)],
        out_specs=[pl.BlockSpec((T, N_HEADS * HEAD_K), row), v_spec],
        out_shape=[jax.ShapeDtypeStruct((R, N_HEADS * HEAD_K), BF16), v_shape],
        compiler_params=_cparams(("arbitrary",)),
        name="kv_up",
    )(ckv, kpe128, wk, wv)


def _attn_prompt_kernel(q_ref, k_ref, vt_ref, o_ref, qt_scr, sa, sb, xa, xb, m_scr, l_scr, acc_scr, *, QB, HP):
    qi = pl.program_id(2)
    s0, s1 = (sa, xa), (sb, xb)
    for hh in range(HP):
        qt_scr[hh] = q_ref[0, :, hh * HEAD_K:(hh + 1) * HEAD_K].T
    m_scr[...] = jnp.full(m_scr.shape, NEG, F32)
    l_scr[...] = jnp.zeros(l_scr.shape, F32)
    acc_scr[...] = jnp.zeros(acc_scr.shape, F32)

    def scores(j, dst):
        start = pl.multiple_of(j * QB, QB)
        for hh in range(HP):
            s = _dot(k_ref[0, pl.ds(start, QB), hh * HEAD_K:(hh + 1) * HEAD_K], qt_scr[hh])
            dst[0][hh] = s
            dst[1][hh] = jnp.max(s, axis=0, keepdims=True)

    def update(j, src, masked):
        for hh in range(HP):
            s = src[0][hh]
            if masked:
                ck = lax.broadcasted_iota(jnp.int32, (QB, QB), 0) // CHUNK
                cq = lax.broadcasted_iota(jnp.int32, (QB, QB), 1) // CHUNK
                s = jnp.where(ck <= cq, s, NEG)
                smax = jnp.max(s, axis=0, keepdims=True)
            else:
                smax = src[1][hh]
            m_old = m_scr[hh]
            m_new = jnp.maximum(m_old, smax)
            p = jnp.exp2(s - m_new)
            alpha = jnp.exp2(m_old - m_new)
            l_scr[hh] = alpha * l_scr[hh] + jnp.sum(p, axis=0, keepdims=True)
            acc_scr[hh] = alpha * acc_scr[hh] + _dot(vt_ref[j, hh], p.astype(BF16))
            m_scr[hh] = m_new

    scores(0, s0)

    def pair(jj, c):
        j = 2 * jj
        scores(j + 1, s1)
        update(j, s0, False)
        scores(j + 2, s0)
        update(j + 1, s1, False)
        return c

    lax.fori_loop(0, qi // 2, pair, 0)

    @pl.when(qi % 2 == 0)
    def _():
        update(qi, s0, True)

    @pl.when(qi % 2 == 1)
    def _():
        scores(qi, s1)
        update(qi - 1, s0, False)
        update(qi, s1, True)

    for hh in range(HP):
        o_ref[0, :, hh * V_HEAD:(hh + 1) * V_HEAD] = (acc_scr[hh] / l_scr[hh]).T.astype(BF16)


def _attn_prompt(q, k, vt, QB, HP):
    B, S, _ = q.shape
    nkb = S // QB
    kern = functools.partial(_attn_prompt_kernel, QB=QB, HP=HP)
    return pl.pallas_call(
        kern, grid=(B, N_HEADS // HP, S // QB),
        in_specs=[pl.BlockSpec((1, QB, HP * HEAD_K), lambda b, h, i: (b, i, h)),
                  pl.BlockSpec((1, S, HP * HEAD_K), lambda b, h, i: (b, 0, h)),
                  pl.BlockSpec((nkb, HP, V_HEAD, QB), lambda b, h, i: (b, h, 0, 0))],
        out_specs=pl.BlockSpec((1, QB, HP * V_HEAD), lambda b, h, i: (b, i, h)),
        out_shape=jax.ShapeDtypeStruct((B, S, N_HEADS * V_HEAD), BF16),
        scratch_shapes=[pltpu.VMEM((HP, HEAD_K, QB), BF16),
                        pltpu.VMEM((HP, QB, QB), F32), pltpu.VMEM((HP, QB, QB), F32),
                        pltpu.VMEM((HP, 1, QB), F32), pltpu.VMEM((HP, 1, QB), F32),
                        pltpu.VMEM((HP, 1, QB), F32),
                        pltpu.VMEM((HP, 1, QB), F32), pltpu.VMEM((HP, V_HEAD, QB), F32)],
        compiler_params=_cparams(("arbitrary", "arbitrary", "arbitrary")),
        name="attn_prompt",
    )(q, k, vt)


def _attn_sample_kernel(q_ref, k_ref, v_ref, o_ref):
    for hd in range(N_HEADS):
        q = q_ref[0, :, hd * HEAD_K:(hd + 1) * HEAD_K]
        k = k_ref[0, :, hd * HEAD_K:(hd + 1) * HEAD_K]
        s = lax.dot_general(q, k, (((1,), (1,)), ((), ())), preferred_element_type=F32)
        p = jnp.exp2(s - jnp.max(s, axis=1, keepdims=True))
        l = jnp.sum(p, axis=1, keepdims=True)
        o = _dot(p.astype(BF16), v_ref[0, :, hd * V_HEAD:(hd + 1) * V_HEAD])
        o_ref[0, :, hd * V_HEAD:(hd + 1) * V_HEAD] = (o / l).astype(BF16)


def _attn_sample(q, k, v):
    B, S, _ = q.shape
    LK = k.shape[1]
    b3 = lambda b: (b, 0, 0)
    return pl.pallas_call(
        _attn_sample_kernel, grid=(B,),
        in_specs=[pl.BlockSpec((1, S, N_HEADS * HEAD_K), b3),
                  pl.BlockSpec((1, LK, N_HEADS * HEAD_K), b3),
                  pl.BlockSpec((1, LK, N_HEADS * V_HEAD), b3)],
        out_specs=pl.BlockSpec((1, S, N_HEADS * V_HEAD), b3),
        out_shape=jax.ShapeDtypeStruct((B, S, N_HEADS * V_HEAD), BF16),
        compiler_params=_cparams(("arbitrary",)),
        name="attn_sample",
    )(q, k, v)


def _lane_min_where(mask, lane):
    return jnp.min(jnp.where(mask, lane, LANES), axis=1, keepdims=True)


def _merge_kernel(o_ref, ma_ref, sgb_ref, x_ref, gt1_ref, sh2_ref, sc2_ref, gpost1_ref, gpre2_ref,
                  wao_ref, wout_ref, wr_ref, br_ref, x1_ref, h2_ref, route_ref, cnt_ref, *, T):
    nc = T // CHUNK
    y_b = _dot(o_ref[...], wao_ref[...])
    m = ma_ref[...].astype(F32) + sgb_ref[...].astype(F32) * y_b
    y = _dot(m.astype(BF16), wout_ref[...])
    yn = _rms(y, gpost1_ref[...]).reshape(nc, CHUNK, D_MODEL)
    x1 = x_ref[...].reshape(nc, CHUNK, D_MODEL) + gt1_ref[...] * yn
    x1_ref[...] = x1.reshape(T, D_MODEL)
    h2 = (_rms(x1, gpre2_ref[...]) * (1.0 + sc2_ref[...]) + sh2_ref[...]).reshape(T, D_MODEL)
    h2b = h2.astype(BF16)
    h2_ref[...] = h2b

    logits = _dot(h2b, wr_ref[...]) + br_ref[...]
    lane = lax.broadcasted_iota(jnp.int32, (T, LANES), 1)
    is_g = lane < N_GROUPS
    gmax = jnp.max(jnp.where(is_g, logits, NEG), axis=1, keepdims=True)
    gidx = _lane_min_where(is_g & (logits == gmax), lane)
    gsum = jnp.sum(jnp.where(is_g, jnp.exp(logits - gmax), 0.0), axis=1, keepdims=True)
    lo = ROUTE_OFF + EXP_PER_GROUP * gidx
    sel = (lane >= lo) & (lane < lo + EXP_PER_GROUP)
    m1 = jnp.max(jnp.where(sel, logits, NEG), axis=1, keepdims=True)
    i1 = _lane_min_where(sel & (logits == m1), lane)
    rest = sel & (lane != i1)
    m2 = jnp.max(jnp.where(rest, logits, NEG), axis=1, keepdims=True)
    i2 = _lane_min_where(rest & (logits == m2), lane)
    e2 = jnp.exp(m2 - m1)
    w1 = 1.0 / (gsum * (1.0 + e2))
    w2 = w1 * e2
    uses = jnp.where((lane == i1) | (lane == i2), 1.0, 0.0)
    earlier = lax.broadcasted_iota(jnp.int32, (T, T), 1) < lax.broadcasted_iota(jnp.int32, (T, T), 0)
    rank = _dot(jnp.where(earlier, 1.0, 0.0).astype(BF16), uses.astype(BF16))
    cnt_ref[0] = (rank[T - 1:T, :] + uses[T - 1:T, :]).astype(jnp.int32)
    rank1 = jnp.sum(jnp.where(lane == i1, rank, 0.0), axis=1, keepdims=True)
    rank2 = jnp.sum(jnp.where(lane == i2, rank, 0.0), axis=1, keepdims=True)
    fields = (i1.astype(F32), i2.astype(F32), w1, w2, rank1, rank2)
    row = jnp.zeros((T, LANES), F32)
    for f, val in enumerate(fields):
        row = jnp.where(lane == f, val, row)
    route_ref[...] = row


def _merge(o, ma, sgb, x, gt1c, sh2c, sc2c, W, T):
    N = x.shape[0]
    nc = T // CHUNK
    row = lambda i: (i, 0)
    chunk = lambda i: (i, 0, 0)
    kern = functools.partial(_merge_kernel, T=T)
    return pl.pallas_call(
        kern, grid=(N // T,),
        in_specs=[pl.BlockSpec((T, D_MODEL), row), pl.BlockSpec((T, D_MODEL), row),
                  pl.BlockSpec((T, D_MODEL), row), pl.BlockSpec((T, D_MODEL), row),
                  pl.BlockSpec((nc, 1, D_MODEL), chunk), pl.BlockSpec((nc, 1, D_MODEL), chunk),
                  pl.BlockSpec((nc, 1, D_MODEL), chunk),
                  _const_spec((1, D_MODEL)), _const_spec((1, D_MODEL)),
                  _const_spec((D_MODEL, D_MODEL)), _const_spec((D_MODEL, D_MODEL)),
                  _const_spec((D_MODEL, LANES)), _const_spec((1, LANES))],
        out_specs=[pl.BlockSpec((T, D_MODEL), row), pl.BlockSpec((T, D_MODEL), row),
                   pl.BlockSpec((T, LANES), row), pl.BlockSpec((1, 1, LANES), chunk)],
        out_shape=[jax.ShapeDtypeStruct((N, D_MODEL), F32), jax.ShapeDtypeStruct((N, D_MODEL), BF16),
                   jax.ShapeDtypeStruct((N, LANES), F32), jax.ShapeDtypeStruct((N // T, 1, LANES), jnp.int32)],
        compiler_params=_cparams(("arbitrary",)),
        name="merge",
    )(o, ma, sgb, x, gt1c, sh2c, sc2c, W["g_post1"], W["g_pre2"], W["w_attn_out"], W["w_out"],
      W["w_route"], W["b_route"])


def _moe_kernel(cnt_ref, h_ref, route_ref, x1_ref, gt2_ref, gpost2_ref, wg_ref, wu_ref, wd_ref, y_ref,
                key_l, key_s, comb_w, *, T, TR, CH, EPS):
    i = pl.program_id(0)
    s = pl.program_id(1)
    nc = T // CHUNK
    nr = T // TR

    def keys(sel, rank, tok):
        expert = sel.astype(jnp.int32) - ROUTE_OFF
        rank = rank.astype(jnp.int32)
        for r in range(1, nr):
            before = jnp.zeros_like(rank)
            for x in range(N_EXPERTS):
                before = jnp.where(expert == x, cnt_ref[(i * nr + r - 1) * N_EXPERTS + x], before)
            rank = rank + jnp.where(tok >= r * TR, before, 0)
        return expert * KEY_STRIDE + rank

    @pl.when(s == 0)
    def _():
        y_ref[...] = jnp.zeros(y_ref.shape, F32)
        rt = route_ref[...].T
        tok_l = lax.broadcasted_iota(jnp.int32, (1, T), 1)
        tok_s = lax.broadcasted_iota(jnp.int32, (T, 1), 0)
        key_l[0:1, :] = keys(rt[0:1, :], rt[4:5, :], tok_l)
        key_l[1:2, :] = keys(rt[1:2, :], rt[5:6, :], tok_l)
        key_s[0] = keys(route_ref[:, 0:1], route_ref[:, 4:5], tok_s)
        key_s[1] = keys(route_ref[:, 1:2], route_ref[:, 5:6], tok_s)
        key_s[2] = route_ref[:, 0:1].astype(jnp.int32) - ROUTE_OFF
        comb_w[0] = route_ref[:, 2:3]
        comb_w[1] = route_ref[:, 3:4]

    k1_l, k2_l = key_l[0:1, :], key_l[1:2, :]
    k1_s, k2_s = key_s[0], key_s[1]
    experts = [s * EPS + x for x in range(EPS)]
    totals = []
    for ex in experts:
        total = cnt_ref[(i * nr) * N_EXPERTS + ex]
        for r in range(1, nr):
            total = total + cnt_ref[(i * nr + r) * N_EXPERTS + ex]
        totals.append(total)
    combs = [jnp.where(key_s[2] == ex, comb_w[0], comb_w[1]) for ex in experts]

    def chunk(c, carry):
        acc = None
        for x, ex in enumerate(experts):
            base = ex * KEY_STRIDE + c * CH
            want_s = base + lax.broadcasted_iota(jnp.int32, (CH, 1), 0)
            pick = (k1_l == want_s) | (k2_l == want_s)
            xg = _dot(jnp.where(pick, 1.0, 0.0).astype(BF16), h_ref[...]).astype(BF16)
            g = _dot(xg, wg_ref[x])
            u = _dot(xg, wu_ref[x])
            hid = (g * jax.nn.sigmoid(g) * u).astype(BF16)
            out = _dot(hid, wd_ref[x]).astype(BF16)
            want_l = base + lax.broadcasted_iota(jnp.int32, (1, CH), 1)
            put = (k1_s == want_l) | (k2_s == want_l)
            part = combs[x] * _dot(jnp.where(put, 1.0, 0.0).astype(BF16), out)
            acc = part if acc is None else acc + part
        y_ref[...] += acc
        return carry

    most = totals[0]
    for total in totals[1:]:
        most = jnp.maximum(most, total)
    lax.fori_loop(0, (most + CH - 1) // CH, chunk, 0)

    @pl.when(s == N_EXPERTS // EPS - 1)
    def _():
        on = _rms(y_ref[...], gpost2_ref[...]).reshape(nc, CHUNK, D_MODEL)
        y = x1_ref[...].reshape(nc, CHUNK, D_MODEL) + gt2_ref[...] * on
        y_ref[...] = y.reshape(T, D_MODEL)


def _moe(h2, route, cnt, x1, gt2c, W, T, TR, CH, EPS):
    N = h2.shape[0]
    nc = T // CHUNK
    row = lambda i, e, c: (i, 0)
    wsel = lambda i, e, c: (e, 0, 0)
    kern = functools.partial(_moe_kernel, T=T, TR=TR, CH=CH, EPS=EPS)
    grid_spec = pltpu.PrefetchScalarGridSpec(
        num_scalar_prefetch=1, grid=(N // T, N_EXPERTS // EPS),
        in_specs=[pl.BlockSpec((T, D_MODEL), row), pl.BlockSpec((T, LANES), row),
                  pl.BlockSpec((T, D_MODEL), row),
                  pl.BlockSpec((nc, 1, D_MODEL), lambda i, e, c: (i, 0, 0)),
                  pl.BlockSpec((1, D_MODEL), lambda i, e, c: (0, 0)),
                  pl.BlockSpec((EPS, D_MODEL, D_EXPERT), wsel),
                  pl.BlockSpec((EPS, D_MODEL, D_EXPERT), wsel),
                  pl.BlockSpec((EPS, D_EXPERT, D_MODEL), wsel)],
        out_specs=pl.BlockSpec((T, D_MODEL), row),
        scratch_shapes=[pltpu.VMEM((8, T), jnp.int32), pltpu.VMEM((3, T, 1), jnp.int32),
                        pltpu.VMEM((2, T, 1), F32)])
    return pl.pallas_call(
        kern, grid_spec=grid_spec,
        out_shape=jax.ShapeDtypeStruct((N, D_MODEL), F32),
        compiler_params=_cparams(("arbitrary", "arbitrary")),
        name="moe",
    )(cnt, h2, route, x1, gt2c, W["g_post2"], W["w_exp_gate"], W["w_exp_up"], W["w_exp_down"])


def _rotate_half_cols(w):
    half = ROPE_DIM // 2
    return jnp.concatenate([-w[..., half:], w[..., :half]], axis=-1)


def _rope_tables(pos):
    inv = ROPE_THETA ** (-jnp.arange(0, ROPE_DIM, 2, dtype=F32) / ROPE_DIM)
    ang = pos.astype(F32)[:, None] * inv
    z = jnp.zeros((pos.shape[0], LANES - ROPE_DIM), F32)
    c, s = jnp.cos(ang), jnp.sin(ang)
    return jnp.concatenate([c, c, z], axis=1), jnp.concatenate([s, s, z], axis=1)


def _chunk_rows(v, seq):
    B, D = v.shape
    return jnp.broadcast_to(v[:, None, None, :], (B, seq // CHUNK, 1, D)).reshape(B * (seq // CHUNK), 1, D)


def _layer(x, ada, ckv_past, kpe_past, conv_state, lru_state, pos0, W, T_in, T_tok):
    B, S, _ = x.shape
    L = 0 if ckv_past is None else ckv_past.shape[1]
    sh1, sc1, gt1, sh2, sc2, gt2 = jnp.split(ada, 6, axis=-1)
    rope_c, rope_s = _rope_tables(pos0 + jnp.arange(S))
    ma, sgb, q, ckv, kpe, kpe128, conv_new, lru_new = _mixer_in(
        x, sh1[:, None, :], sc1[:, None, :], conv_state, lru_state[:, None, :], rope_c, rope_s, W, T_in)

    if L == 0:
        k, vt = _kv_up(ckv.reshape(B * S, KV_LORA), kpe128.reshape(B * S, LANES), W["w_k_up"], W["w_v_up_t"],
                       ATTN_BLOCK, True)
        o = _attn_prompt(q, k.reshape(B, S, -1), vt, ATTN_BLOCK, ATTN_HEADS_PER_STEP)
    else:
        ckv_all = jnp.concatenate([ckv_past, ckv], axis=1)
        kpe_past128 = jnp.pad(kpe_past, ((0, 0), (0, 0), (0, LANES - ROPE_DIM))).astype(BF16)
        kpe_all = jnp.concatenate([kpe_past128, kpe128], axis=1)
        LK = L + S
        k, v = _kv_up(ckv_all.reshape(B * LK, KV_LORA), kpe_all.reshape(B * LK, LANES), W["w_k_up"], W["w_v_up"],
                      256, False)
        o = _attn_sample(q, k.reshape(B, LK, -1), v.reshape(B, LK, -1))

    N = B * S
    x1, h2, route, cnt = _merge(o.reshape(N, D_MODEL), ma.reshape(N, D_MODEL), sgb.reshape(N, D_MODEL),
                                x.reshape(N, D_MODEL), _chunk_rows(gt1, S), _chunk_rows(sh2, S),
                                _chunk_rows(sc2, S), W, T_tok)
    cnt = cnt[:, 0, ROUTE_OFF:ROUTE_OFF + N_EXPERTS].reshape(-1)
    y = _moe(h2, route, cnt, x1, _chunk_rows(gt2, S), W, MOE_TILE, T_tok, MOE_CHUNK, MOE_EXPERTS_PER_STEP)
    return y.reshape(B, S, D_MODEL), ckv, kpe, conv_new, lru_new.reshape(B, D_RNN)


def kernel(x_prompt, x_sample, c_prompt, c_sample, cache_ckv, cache_kpe, state_conv, state_rglru, w_ada, b_ada, g_pre1, g_post1, g_pre2, g_post2, w_in, w_conv, b_conv, w_rgate, b_rgate, w_igate, b_igate, lru_lambda, w_rnn_out, g_q_lat, w_q_up, g_kv_lat, w_k_up, w_v_up, w_attn_out, w_out, w_group, b_group, w_erouter, b_erouter, w_exp_gate, w_exp_up, w_exp_down):
    assert w_in.shape[0] == 1, "single-layer trunk"
    B = x_prompt.shape[0]
    wi = w_in[0]
    sp = lambda a, b: wi[:, a:b]
    xr, gr = sp(0, D_RNN), sp(D_RNN, 2 * D_RNN)
    o = 2 * D_RNN
    ql, kvl, kr = sp(o, o + Q_LORA), sp(o + Q_LORA, o + Q_LORA + KV_LORA), \
        sp(o + Q_LORA + KV_LORA, o + Q_LORA + KV_LORA + ROPE_DIM)
    o = o + Q_LORA + KV_LORA + ROPE_DIM
    ga, gb = sp(o, o + D_MODEL), sp(o + D_MODEL, o + 2 * D_MODEL)
    wq = w_q_up[0].reshape(Q_LORA, N_HEADS, QK_NOPE + ROPE_DIM)
    wq_pe = wq[..., QK_NOPE:]
    row = lambda a: a[0].reshape(1, -1)
    W = {
        "g_pre1": row(g_pre1), "g_post1": row(g_post1), "g_pre2": row(g_pre2), "g_post2": row(g_post2),
        "w_in2": jnp.concatenate([xr, gr, ql, kvl, kr, _rotate_half_cols(kr), ga, gb], axis=1).astype(BF16),
        "w_conv": w_conv[0], "b_conv": row(b_conv),
        "w_gates": jnp.concatenate([w_rgate[0], w_igate[0]], axis=-1).astype(BF16),
        "b_rgate": row(b_rgate), "b_igate": row(b_igate), "lam": row(lru_lambda),
        "w_rnn_out": w_rnn_out[0].astype(BF16),
        "g_q": row(g_q_lat), "g_kv": row(g_kv_lat),
        "w_qup": jnp.concatenate([wq[..., :QK_NOPE], wq_pe, _rotate_half_cols(wq_pe)], axis=-1)
                 .reshape(Q_LORA, N_HEADS * HEAD_K).astype(BF16),
        "w_k_up": w_k_up[0].astype(BF16), "w_v_up": w_v_up[0].astype(BF16),
        "w_v_up_t": w_v_up[0].T.astype(BF16),
        "w_attn_out": w_attn_out[0].astype(BF16), "w_out": w_out[0].astype(BF16),
        "w_route": jnp.pad(jnp.concatenate([w_group[0], w_erouter[0]], axis=1),
                           ((0, 0), (0, LANES - N_GROUPS - N_EXPERTS))).astype(BF16),
        "b_route": jnp.pad(jnp.concatenate([b_group[0], b_erouter[0]]), (0, LANES - N_GROUPS - N_EXPERTS))
                   .reshape(1, LANES),
        "w_exp_gate": w_exp_gate[0].astype(BF16), "w_exp_up": w_exp_up[0].astype(BF16),
        "w_exp_down": w_exp_down[0].astype(BF16),
    }
    ada = _ada(jnp.concatenate([c_prompt, c_sample], axis=0), w_ada[0], b_ada[0])
    zeros_conv = jnp.zeros((B, CONV_W - 1, D_RNN), F32)
    zeros_lru = jnp.zeros((B, D_RNN), F32)
    yp, ckv_p, kpe_p, conv_p, lru_p = _layer(x_prompt, ada[:B], None, None, zeros_conv, zeros_lru, 0, W, 512, 512)
    ys, ckv_s, kpe_s, conv_s, lru_s = _layer(x_sample, ada[B:], cache_ckv[0], cache_kpe[0], state_conv[0],
                                             state_rglru[0], cache_ckv.shape[2], W, 64, 512)
    return (yp, ys, ckv_p[None], kpe_p[None], conv_p[None], lru_p[None],
            ckv_s[None], kpe_s[None], conv_s[None], lru_s[None])
```

```python
import functools

import jax
import jax.numpy as jnp
from jax import lax
from jax.experimental import pallas as pl
from jax.experimental.pallas import tpu as pltpu

F32 = jnp.float32
BF16 = jnp.bfloat16

D_MODEL = 1024
CHUNK = 64
D_RNN = 1024
N_RNN_BLOCKS = 8
RNN_BLOCK = D_RNN // N_RNN_BLOCKS
CONV_W = 4
LRU_C = 8.0
N_HEADS = 8
QK_NOPE = 128
ROPE_DIM = 64
V_HEAD = 128
Q_LORA = 384
KV_LORA = 256
ROPE_THETA = 10000.0
SM_SCALE = (QK_NOPE + ROPE_DIM) ** -0.5
LOG2E = 1.4426950408889634
Q_SCALE = SM_SCALE * LOG2E
N_GROUPS = 4
EXP_PER_GROUP = 4
N_EXPERTS = N_GROUPS * EXP_PER_GROUP
D_EXPERT = 512
EPS = 1e-6

LANES = 128
SUBLANES = 8
HEAD_K = QK_NOPE + 2 * ROPE_DIM
OFF_XR = 0
OFF_GR = OFF_XR + D_RNN
OFF_QL = OFF_GR + D_RNN
OFF_KVL = OFF_QL + Q_LORA
OFF_KR = OFF_KVL + KV_LORA
OFF_GA = OFF_KR + 2 * ROPE_DIM
OFF_GB = OFF_GA + D_MODEL
IN_COLS2 = OFF_GB + D_MODEL
ROUTE_OFF = N_GROUPS
CONV_PAD = 8
NEG = -1e30
ATTN_BLOCK = 512
MOE_TILE = 1024
KEY_STRIDE = 2048
MOE_EXPERTS_PER_STEP = 1
MOE_CHUNK = 160
ATTN_HEADS_PER_STEP = 2
VMEM_LIMIT = 56 * 1024 * 1024


def _cparams(sem):
    return pltpu.CompilerParams(dimension_semantics=sem, vmem_limit_bytes=VMEM_LIMIT)


def _const_spec(shape):
    n = len(shape)
    return pl.BlockSpec(shape, lambda *_: (0,) * n, pipeline_mode=pl.Buffered(1))


def _rms(x, g):
    return x * lax.rsqrt(jnp.mean(x * x, axis=-1, keepdims=True) + EPS) * g


def _dot(a, b):
    return jnp.dot(a, b, preferred_element_type=F32)


def _rope(v, c, s):
    return v * c + pltpu.roll(v, ROPE_DIM, axis=1) * s


def _ada_kernel(c_ref, w_ref, b_ref, o_ref):
    c = c_ref[...]
    s = c * jax.nn.sigmoid(c)
    o_ref[...] = jnp.dot(s, w_ref[...], preferred_element_type=F32,
                         precision=lax.Precision.HIGHEST) + b_ref[...]


def _ada(c, w, b):
    nb = c.shape[0]
    n = w.shape[1]
    bn = n // 6
    return pl.pallas_call(
        _ada_kernel,
        grid=(n // bn,),
        in_specs=[_const_spec((nb, D_MODEL)),
                  pl.BlockSpec((D_MODEL, bn), lambda j: (0, j)),
                  pl.BlockSpec((1, bn), lambda j: (0, j))],
        out_specs=pl.BlockSpec((nb, bn), lambda j: (0, j)),
        out_shape=jax.ShapeDtypeStruct((nb, n), F32),
        compiler_params=_cparams(("arbitrary",)),
        name="ada",
    )(c, w, b.reshape(1, n))


def _mixer_in_kernel(x_ref, sh_ref, sc_ref, gpre_ref, win_ref, cst_ref, lst_ref, wconv_ref, bconv_ref,
                     wgate_ref, br_ref, bi_ref, lam_ref, wrnn_ref, gq_ref, wqup_ref, gkv_ref,
                     rc_ref, rs_ref,
                     ma_ref, sgb_ref, q_ref, ckv_ref, kpe_ref, kpe128_ref, cout_ref, lout_ref,
                     xbuf, b_scr, hcar, *, T):
    t = pl.program_id(1)

    @pl.when(t == 0)
    def _():
        xbuf[CONV_PAD - (CONV_W - 1):CONV_PAD, :] = cst_ref[0]
        hcar[...] = lst_ref[0]

    x = x_ref[0]
    h = _rms(x, gpre_ref[...]) * (1.0 + sc_ref[0]) + sh_ref[0]
    hb = h.astype(BF16)

    xbuf[CONV_PAD:CONV_PAD + T, :] = _dot(hb, win_ref[:, OFF_XR:OFF_XR + D_RNN])
    xc = bconv_ref[...]
    for k in range(CONV_W):
        lo = CONV_PAD - (CONV_W - 1) + k
        xc = xc + xbuf[lo:lo + T, :] * wconv_ref[k:k + 1, :]
    tail = xbuf[T + CONV_PAD - (CONV_W - 1):T + CONV_PAD, :]
    cout_ref[0] = tail
    xbuf[CONV_PAD - (CONV_W - 1):CONV_PAD, :] = tail

    lam = lam_ref[...]
    softplus_neg_lam = jnp.maximum(-lam, 0.0) + jnp.log1p(jnp.exp(-jnp.abs(lam)))
    row_in_group = lax.broadcasted_iota(jnp.int32, (T, RNN_BLOCK), 0) % SUBLANES
    for n in range(N_RNN_BLOCKS):
        blk = slice(n * RNN_BLOCK, (n + 1) * RNN_BLOCK)
        xcb = xc[:, blk]
        g = _dot(xcb.astype(BF16), wgate_ref[n])
        r = jax.nn.sigmoid(g[:, :RNN_BLOCK] + br_ref[:, blk])
        i = jax.nn.sigmoid(g[:, RNN_BLOCK:] + bi_ref[:, blk])
        log_a = -LRU_C * r * softplus_neg_lam[:, blk]
        a = jnp.exp(log_a)
        b = jnp.sqrt(-jnp.tanh(log_a) * (a * a + 1.0)) * (i * xcb)
        for d in (1, 2, 4):
            keep = row_in_group >= d
            a_prev = jnp.where(keep, pltpu.roll(a, d, axis=0), 1.0)
            b_prev = jnp.where(keep, pltpu.roll(b, d, axis=0), 0.0)
            b = b + a * b_prev
            a = a * a_prev
        hprev = hcar[:, blk]
        for grp in range(T // SUBLANES):
            rows = slice(grp * SUBLANES, (grp + 1) * SUBLANES)
            hg = b[rows, :] + a[rows, :] * hprev
            b_scr[rows, blk] = hg
            hprev = hg[SUBLANES - 1:SUBLANES, :]
        hcar[:, blk] = hprev
    lout_ref[0] = hcar[...]

    gr = _dot(hb, win_ref[:, OFF_GR:OFF_GR + D_RNN])
    y_a = _dot((b_scr[...] * jax.nn.gelu(gr)).astype(BF16), wrnn_ref[...])
    ga = _dot(hb, win_ref[:, OFF_GA:OFF_GA + D_MODEL])
    ma_ref[0] = (jax.nn.sigmoid(ga) * y_a).astype(BF16)
    gb = _dot(hb, win_ref[:, OFF_GB:OFF_GB + D_MODEL])
    sgb_ref[0] = jax.nn.sigmoid(gb).astype(BF16)

    rc = rc_ref[...]
    rs = rs_ref[...]
    ql = _dot(hb, win_ref[:, OFF_QL:OFF_QL + Q_LORA])
    q = _dot(_rms(ql, gq_ref[...]).astype(BF16), wqup_ref[...])
    for hd in range(N_HEADS):
        base = hd * HEAD_K
        q_ref[0, :, base:base + QK_NOPE] = (q[:, base:base + QK_NOPE] * Q_SCALE).astype(BF16)
        pe = _rope(q[:, base + QK_NOPE:base + HEAD_K], rc, rs)
        q_ref[0, :, base + QK_NOPE:base + HEAD_K] = (pe * Q_SCALE).astype(BF16)
    kvl = _dot(hb, win_ref[:, OFF_KVL:OFF_KVL + KV_LORA])
    ckv_ref[0] = _rms(kvl, gkv_ref[...])
    kp = _rope(_dot(hb, win_ref[:, OFF_KR:OFF_KR + 2 * ROPE_DIM]), rc, rs)
    kpe_ref[0] = kp[:, :ROPE_DIM]
    kpe128_ref[0] = kp.astype(BF16)


def _mixer_in(x, sh1, sc1, conv_state, lru_state, rope_c, rope_s, W, T):
    B, S, _ = x.shape
    nt = S // T
    kern = functools.partial(_mixer_in_kernel, T=T)
    bt = lambda b, t: (b, t, 0)
    bo = lambda b, t: (b, 0, 0)
    tt = lambda b, t: (t, 0)
    in_specs = [
        pl.BlockSpec((1, T, D_MODEL), bt),
        pl.BlockSpec((1, 1, D_MODEL), bo),
        pl.BlockSpec((1, 1, D_MODEL), bo),
        _const_spec((1, D_MODEL)),
        _const_spec((D_MODEL, IN_COLS2)),
        pl.BlockSpec((1, CONV_W - 1, D_RNN), bo),
        pl.BlockSpec((1, 1, D_RNN), bo),
        _const_spec((CONV_W, D_RNN)),
        _const_spec((1, D_RNN)),
        _const_spec((N_RNN_BLOCKS, RNN_BLOCK, 2 * RNN_BLOCK)),
        _const_spec((1, D_RNN)),
        _const_spec((1, D_RNN)),
        _const_spec((1, D_RNN)),
        _const_spec((D_RNN, D_MODEL)),
        _const_spec((1, Q_LORA)),
        _const_spec((Q_LORA, N_HEADS * HEAD_K)),
        _const_spec((1, KV_LORA)),
        pl.BlockSpec((T, LANES), tt),
        pl.BlockSpec((T, LANES), tt),
    ]
    out_specs = [
        pl.BlockSpec((1, T, D_MODEL), bt),
        pl.BlockSpec((1, T, D_MODEL), bt),
        pl.BlockSpec((1, T, N_HEADS * HEAD_K), bt),
        pl.BlockSpec((1, T, KV_LORA), bt),
        pl.BlockSpec((1, T, ROPE_DIM), bt),
        pl.BlockSpec((1, T, LANES), bt),
        pl.BlockSpec((1, CONV_W - 1, D_RNN), bo),
        pl.BlockSpec((1, 1, D_RNN), bo),
    ]
    out_shape = [
        jax.ShapeDtypeStruct((B, S, D_MODEL), BF16),
        jax.ShapeDtypeStruct((B, S, D_MODEL), BF16),
        jax.ShapeDtypeStruct((B, S, N_HEADS * HEAD_K), BF16),
        jax.ShapeDtypeStruct((B, S, KV_LORA), F32),
        jax.ShapeDtypeStruct((B, S, ROPE_DIM), F32),
        jax.ShapeDtypeStruct((B, S, LANES), BF16),
        jax.ShapeDtypeStruct((B, CONV_W - 1, D_RNN), F32),
        jax.ShapeDtypeStruct((B, 1, D_RNN), F32),
    ]
    scratch = [
        pltpu.VMEM((T + CONV_PAD, D_RNN), F32),
        pltpu.VMEM((T, D_RNN), F32),
        pltpu.VMEM((1, D_RNN), F32),
    ]
    return pl.pallas_call(
        kern, grid=(B, nt), in_specs=in_specs, out_specs=out_specs, out_shape=out_shape,
        scratch_shapes=scratch, compiler_params=_cparams(("arbitrary", "arbitrary")),
        name="mixer_in",
    )(x, sh1, sc1, W["g_pre1"], W["w_in2"], conv_state, lru_state, W["w_conv"], W["b_conv"],
      W["w_gates"], W["b_rgate"], W["b_igate"], W["lam"], W["w_rnn_out"], W["g_q"], W["w_qup"],
      W["g_kv"], rope_c, rope_s)


def _kv_up_kernel(ckv_ref, kpe_ref, wk_ref, wv_ref, k_ref, v_ref, *, v_transposed):
    c = ckv_ref[...].astype(BF16)
    kn = _dot(c, wk_ref[...])
    kpe = kpe_ref[...]
    for hd in range(N_HEADS):
        base = hd * HEAD_K
        k_ref[:, base:base + QK_NOPE] = kn[:, hd * QK_NOPE:(hd + 1) * QK_NOPE].astype(BF16)
        k_ref[:, base + QK_NOPE:base + HEAD_K] = kpe
    if v_transposed:
        vt = lax.dot_general(wv_ref[...], c, (((1,), (1,)), ((), ())), preferred_element_type=F32)
        for hd in range(N_HEADS):
            v_ref[0, hd] = vt[hd * V_HEAD:(hd + 1) * V_HEAD, :].astype(BF16)
    else:
        v_ref[...] = _dot(c, wv_ref[...]).astype(BF16)


def _kv_up(ckv, kpe128, wk, wv, T, v_transposed):
    R = ckv.shape[0]
    row = lambda i: (i, 0)
    if v_transposed:
        v_spec = pl.BlockSpec((1, N_HEADS, V_HEAD, T), lambda i: (i, 0, 0, 0))
        v_shape = jax.ShapeDtypeStruct((R // T, N_HEADS, V_HEAD, T), BF16)
    else:
        v_spec = pl.BlockSpec((T, N_HEADS * V_HEAD), row)
        v_shape = jax.ShapeDtypeStruct((R, N_HEADS * V_HEAD), BF16)
    return pl.pallas_call(
        functools.partial(_kv_up_kernel, v_transposed=v_transposed), grid=(R // T,),
        in_specs=[pl.BlockSpec((T, KV_LORA), row), pl.BlockSpec((T, LANES), row),
                  _const_spec(wk.shape), _const_spec(wv.shape)],
        out_specs=[pl.BlockSpec((T, N_HEADS * HEAD_K), row), v_spec],
        out_shape=[jax.ShapeDtypeStruct((R, N_HEADS * HEAD_K), BF16), v_shape],
        compiler_params=_cparams(("arbitrary",)),
        name="kv_up",
    )(ckv, kpe128, wk, wv)


def _attn_prompt_kernel(q_ref, k_ref, vt_ref, o_ref, qt_scr, sa, sb, xa, xb, m_scr, l_scr, acc_scr, *, QB, HP):
    qi = pl.program_id(2)
    s0, s1 = (sa, xa), (sb, xb)
    for hh in range(HP):
        qt_scr[hh] = q_ref[0, :, hh * HEAD_K:(hh + 1) * HEAD_K].T
    m_scr[...] = jnp.full(m_scr.shape, NEG, F32)
    l_scr[...] = jnp.zeros(l_scr.shape, F32)
    acc_scr[...] = jnp.zeros(acc_scr.shape, F32)

    def scores(j, dst):
        start = pl.multiple_of(j * QB, QB)
        for hh in range(HP):
            s = _dot(k_ref[0, pl.ds(start, QB), hh * HEAD_K:(hh + 1) * HEAD_K], qt_scr[hh])
            dst[0][hh] = s
            dst[1][hh] = jnp.max(s, axis=0, keepdims=True)

    def update(j, src, masked):
        for hh in range(HP):
            s = src[0][hh]
            if masked:
                ck = lax.broadcasted_iota(jnp.int32, (QB, QB), 0) // CHUNK
                cq = lax.broadcasted_iota(jnp.int32, (QB, QB), 1) // CHUNK
                s = jnp.where(ck <= cq, s, NEG)
                smax = jnp.max(s, axis=0, keepdims=True)
            else:
                smax = src[1][hh]
            m_old = m_scr[hh]
            m_new = jnp.maximum(m_old, smax)
            p = jnp.exp2(s - m_new)
            alpha = jnp.exp2(m_old - m_new)
            l_scr[hh] = alpha * l_scr[hh] + jnp.sum(p, axis=0, keepdims=True)
            acc_scr[hh] = alpha * acc_scr[hh] + _dot(vt_ref[j, hh], p.astype(BF16))
            m_scr[hh] = m_new

    scores(0, s0)

    def pair(jj, c):
        j = 2 * jj
        scores(j + 1, s1)
        update(j, s0, False)
        scores(j + 2, s0)
        update(j + 1, s1, False)
        return c

    lax.fori_loop(0, qi // 2, pair, 0)

    @pl.when(qi % 2 == 0)
    def _():
        update(qi, s0, True)

    @pl.when(qi % 2 == 1)
    def _():
        scores(qi, s1)
        update(qi - 1, s0, False)
        update(qi, s1, True)

    for hh in range(HP):
        o_ref[0, :, hh * V_HEAD:(hh + 1) * V_HEAD] = (acc_scr[hh] / l_scr[hh]).T.astype(BF16)


def _attn_prompt(q, k, vt, QB, HP):
    B, S, _ = q.shape
    nkb = S // QB
    kern = functools.partial(_attn_prompt_kernel, QB=QB, HP=HP)
    return pl.pallas_call(
        kern, grid=(B, N_HEADS // HP, S // QB),
        in_specs=[pl.BlockSpec((1, QB, HP * HEAD_K), lambda b, h, i: (b, i, h)),
                  pl.BlockSpec((1, S, HP * HEAD_K), lambda b, h, i: (b, 0, h)),
                  pl.BlockSpec((nkb, HP, V_HEAD, QB), lambda b, h, i: (b, h, 0, 0))],
        out_specs=pl.BlockSpec((1, QB, HP * V_HEAD), lambda b, h, i: (b, i, h)),
        out_shape=jax.ShapeDtypeStruct((B, S, N_HEADS * V_HEAD), BF16),
        scratch_shapes=[pltpu.VMEM((HP, HEAD_K, QB), BF16),
                        pltpu.VMEM((HP, QB, QB), F32), pltpu.VMEM((HP, QB, QB), F32),
                        pltpu.VMEM((HP, 1, QB), F32), pltpu.VMEM((HP, 1, QB), F32),
                        pltpu.VMEM((HP, 1, QB), F32),
                        pltpu.VMEM((HP, 1, QB), F32), pltpu.VMEM((HP, V_HEAD, QB), F32)],
        compiler_params=_cparams(("arbitrary", "arbitrary", "arbitrary")),
        name="attn_prompt",
    )(q, k, vt)


def _attn_sample_kernel(q_ref, k_ref, v_ref, o_ref):
    for hd in range(N_HEADS):
        q = q_ref[0, :, hd * HEAD_K:(hd + 1) * HEAD_K]
        k = k_ref[0, :, hd * HEAD_K:(hd + 1) * HEAD_K]
        s = lax.dot_general(q, k, (((1,), (1,)), ((), ())), preferred_element_type=F32)
        p = jnp.exp2(s - jnp.max(s, axis=1, keepdims=True))
        l = jnp.sum(p, axis=1, keepdims=True)
        o = _dot(p.astype(BF16), v_ref[0, :, hd * V_HEAD:(hd + 1) * V_HEAD])
        o_ref[0, :, hd * V_HEAD:(hd + 1) * V_HEAD] = (o / l).astype(BF16)


def _attn_sample(q, k, v):
    B, S, _ = q.shape
    LK = k.shape[1]
    b3 = lambda b: (b, 0, 0)
    return pl.pallas_call(
        _attn_sample_kernel, grid=(B,),
        in_specs=[pl.BlockSpec((1, S, N_HEADS * HEAD_K), b3),
                  pl.BlockSpec((1, LK, N_HEADS * HEAD_K), b3),
                  pl.BlockSpec((1, LK, N_HEADS * V_HEAD), b3)],
        out_specs=pl.BlockSpec((1, S, N_HEADS * V_HEAD), b3),
        out_shape=jax.ShapeDtypeStruct((B, S, N_HEADS * V_HEAD), BF16),
        compiler_params=_cparams(("arbitrary",)),
        name="attn_sample",
    )(q, k, v)


def _lane_min_where(mask, lane):
    return jnp.min(jnp.where(mask, lane, LANES), axis=1, keepdims=True)


def _merge_kernel(o_ref, ma_ref, sgb_ref, x_ref, gt1_ref, sh2_ref, sc2_ref, gpost1_ref, gpre2_ref,
                  wao_ref, wout_ref, wr_ref, br_ref, x1_ref, h2_ref, route_ref, cnt_ref, *, T):
    nc = T // CHUNK
    y_b = _dot(o_ref[...], wao_ref[...])
    m = ma_ref[...].astype(F32) + sgb_ref[...].astype(F32) * y_b
    y = _dot(m.astype(BF16), wout_ref[...])
    yn = _rms(y, gpost1_ref[...]).reshape(nc, CHUNK, D_MODEL)
    x1 = x_ref[...].reshape(nc, CHUNK, D_MODEL) + gt1_ref[...] * yn
    x1_ref[...] = x1.reshape(T, D_MODEL)
    h2 = (_rms(x1, gpre2_ref[...]) * (1.0 + sc2_ref[...]) + sh2_ref[...]).reshape(T, D_MODEL)
    h2b = h2.astype(BF16)
    h2_ref[...] = h2b

    logits = _dot(h2b, wr_ref[...]) + br_ref[...]
    lane = lax.broadcasted_iota(jnp.int32, (T, LANES), 1)
    is_g = lane < N_GROUPS
    gmax = jnp.max(jnp.where(is_g, logits, NEG), axis=1, keepdims=True)
    gidx = _lane_min_where(is_g & (logits == gmax), lane)
    gsum = jnp.sum(jnp.where(is_g, jnp.exp(logits - gmax), 0.0), axis=1, keepdims=True)
    lo = ROUTE_OFF + EXP_PER_GROUP * gidx
    sel = (lane >= lo) & (lane < lo + EXP_PER_GROUP)
    m1 = jnp.max(jnp.where(sel, logits, NEG), axis=1, keepdims=True)
    i1 = _lane_min_where(sel & (logits == m1), lane)
    rest = sel & (lane != i1)
    m2 = jnp.max(jnp.where(rest, logits, NEG), axis=1, keepdims=True)
    i2 = _lane_min_where(rest & (logits == m2), lane)
    e2 = jnp.exp(m2 - m1)
    w1 = 1.0 / (gsum * (1.0 + e2))
    w2 = w1 * e2
    uses = jnp.where((lane == i1) | (lane == i2), 1.0, 0.0)
    earlier = lax.broadcasted_iota(jnp.int32, (T, T), 1) < lax.broadcasted_iota(jnp.int32, (T, T), 0)
    rank = _dot(jnp.where(earlier, 1.0, 0.0).astype(BF16), uses.astype(BF16))
    cnt_ref[0] = (rank[T - 1:T, :] + uses[T - 1:T, :]).astype(jnp.int32)
    rank1 = jnp.sum(jnp.where(lane == i1, rank, 0.0), axis=1, keepdims=True)
    rank2 = jnp.sum(jnp.where(lane == i2, rank, 0.0), axis=1, keepdims=True)
    fields = (i1.astype(F32), i2.astype(F32), w1, w2, rank1, rank2)
    row = jnp.zeros((T, LANES), F32)
    for f, val in enumerate(fields):
        row = jnp.where(lane == f, val, row)
    route_ref[...] = row


def _merge(o, ma, sgb, x, gt1c, sh2c, sc2c, W, T):
    N = x.shape[0]
    nc = T // CHUNK
    row = lambda i: (i, 0)
    chunk = lambda i: (i, 0, 0)
    kern = functools.partial(_merge_kernel, T=T)
    return pl.pallas_call(
        kern, grid=(N // T,),
        in_specs=[pl.BlockSpec((T, D_MODEL), row), pl.BlockSpec((T, D_MODEL), row),
                  pl.BlockSpec((T, D_MODEL), row), pl.BlockSpec((T, D_MODEL), row),
                  pl.BlockSpec((nc, 1, D_MODEL), chunk), pl.BlockSpec((nc, 1, D_MODEL), chunk),
                  pl.BlockSpec((nc, 1, D_MODEL), chunk),
                  _const_spec((1, D_MODEL)), _const_spec((1, D_MODEL)),
                  _const_spec((D_MODEL, D_MODEL)), _const_spec((D_MODEL, D_MODEL)),
                  _const_spec((D_MODEL, LANES)), _const_spec((1, LANES))],
        out_specs=[pl.BlockSpec((T, D_MODEL), row), pl.BlockSpec((T, D_MODEL), row),
                   pl.BlockSpec((T, LANES), row), pl.BlockSpec((1, 1, LANES), chunk)],
        out_shape=[jax.ShapeDtypeStruct((N, D_MODEL), F32), jax.ShapeDtypeStruct((N, D_MODEL), BF16),
                   jax.ShapeDtypeStruct((N, LANES), F32), jax.ShapeDtypeStruct((N // T, 1, LANES), jnp.int32)],
        compiler_params=_cparams(("arbitrary",)),
        name="merge",
    )(o, ma, sgb, x, gt1c, sh2c, sc2c, W["g_post1"], W["g_pre2"], W["w_attn_out"], W["w_out"],
      W["w_route"], W["b_route"])


def _moe_kernel(cnt_ref, h_ref, route_ref, x1_ref, gt2_ref, gpost2_ref, wg_ref, wu_ref, wd_ref, y_ref,
                key_l, key_s, comb_w, *, T, TR, CH, NE):
    i = pl.program_id(0)
    s = pl.program_id(1)
    nc = T // CHUNK
    nr = T // TR

    def keys(sel, rank, tok):
        expert = sel.astype(jnp.int32) - ROUTE_OFF
        rank = rank.astype(jnp.int32)
        for r in range(1, nr):
            before = jnp.zeros_like(rank)
            for x in range(N_EXPERTS):
                before = jnp.where(expert == x, cnt_ref[(i * nr + r - 1) * N_EXPERTS + x], before)
            rank = rank + jnp.where(tok >= r * TR, before, 0)
        return expert * KEY_STRIDE + rank

    @pl.when(s == 0)
    def _():
        y_ref[...] = jnp.zeros(y_ref.shape, F32)
        rt = route_ref[...].T
        tok_l = lax.broadcasted_iota(jnp.int32, (1, T), 1)
        tok_s = lax.broadcasted_iota(jnp.int32, (T, 1), 0)
        key_l[0:1, :] = keys(rt[0:1, :], rt[4:5, :], tok_l)
        key_l[1:2, :] = keys(rt[1:2, :], rt[5:6, :], tok_l)
        key_s[0] = keys(route_ref[:, 0:1], route_ref[:, 4:5], tok_s)
        key_s[1] = keys(route_ref[:, 1:2], route_ref[:, 5:6], tok_s)
        key_s[2] = route_ref[:, 0:1].astype(jnp.int32) - ROUTE_OFF
        comb_w[0] = route_ref[:, 2:3]
        comb_w[1] = route_ref[:, 3:4]

    k1_l, k2_l = key_l[0:1, :], key_l[1:2, :]
    k1_s, k2_s = key_s[0], key_s[1]
    experts = [s * NE + x for x in range(NE)]
    totals = []
    for ex in experts:
        total = cnt_ref[(i * nr) * N_EXPERTS + ex]
        for r in range(1, nr):
            total = total + cnt_ref[(i * nr + r) * N_EXPERTS + ex]
        totals.append(total)
    combs = [jnp.where(key_s[2] == ex, comb_w[0], comb_w[1]) for ex in experts]

    def chunk(c, carry):
        acc = None
        for x, ex in enumerate(experts):
            base = ex * KEY_STRIDE + c * CH
            want_s = base + lax.broadcasted_iota(jnp.int32, (CH, 1), 0)
            pick = (k1_l == want_s) | (k2_l == want_s)
            xg = _dot(jnp.where(pick, 1.0, 0.0).astype(BF16), h_ref[...]).astype(BF16)
            g = _dot(xg, wg_ref[x])
            u = _dot(xg, wu_ref[x])
            hid = (g * jax.nn.sigmoid(g) * u).astype(BF16)
            out = _dot(hid, wd_ref[x]).astype(BF16)
            want_l = base + lax.broadcasted_iota(jnp.int32, (1, CH), 1)
            put = (k1_s == want_l) | (k2_s == want_l)
            part = combs[x] * _dot(jnp.where(put, 1.0, 0.0).astype(BF16), out)
            acc = part if acc is None else acc + part
        y_ref[...] += acc
        return carry

    most = totals[0]
    for total in totals[1:]:
        most = jnp.maximum(most, total)
    lax.fori_loop(0, (most + CH - 1) // CH, chunk, 0)

    @pl.when(s == N_EXPERTS // NE - 1)
    def _():
        on = _rms(y_ref[...], gpost2_ref[...]).reshape(nc, CHUNK, D_MODEL)
        y = x1_ref[...].reshape(nc, CHUNK, D_MODEL) + gt2_ref[...] * on
        y_ref[...] = y.reshape(T, D_MODEL)


def _moe(h2, route, cnt, x1, gt2c, W, T, TR, CH, NE):
    N = h2.shape[0]
    nc = T // CHUNK
    row = lambda i, e, c: (i, 0)
    wsel = lambda i, e, c: (e, 0, 0)
    kern = functools.partial(_moe_kernel, T=T, TR=TR, CH=CH, NE=NE)
    grid_spec = pltpu.PrefetchScalarGridSpec(
        num_scalar_prefetch=1, grid=(N // T, N_EXPERTS // NE),
        in_specs=[pl.BlockSpec((T, D_MODEL), row), pl.BlockSpec((T, LANES), row),
                  pl.BlockSpec((T, D_MODEL), row),
                  pl.BlockSpec((nc, 1, D_MODEL), lambda i, e, c: (i, 0, 0)),
                  pl.BlockSpec((1, D_MODEL), lambda i, e, c: (0, 0)),
                  pl.BlockSpec((NE, D_MODEL, D_EXPERT), wsel),
                  pl.BlockSpec((NE, D_MODEL, D_EXPERT), wsel),
                  pl.BlockSpec((NE, D_EXPERT, D_MODEL), wsel)],
        out_specs=pl.BlockSpec((T, D_MODEL), row),
        scratch_shapes=[pltpu.VMEM((8, T), jnp.int32), pltpu.VMEM((3, T, 1), jnp.int32),
                        pltpu.VMEM((2, T, 1), F32)])
    return pl.pallas_call(
        kern, grid_spec=grid_spec,
        out_shape=jax.ShapeDtypeStruct((N, D_MODEL), F32),
        compiler_params=_cparams(("arbitrary", "arbitrary")),
        name="moe",
    )(cnt, h2, route, x1, gt2c, W["g_post2"], W["w_exp_gate"], W["w_exp_up"], W["w_exp_down"])


def _rotate_half_cols(w):
    half = ROPE_DIM // 2
    return jnp.concatenate([-w[..., half:], w[..., :half]], axis=-1)


def _rope_tables(pos):
    inv = ROPE_THETA ** (-jnp.arange(0, ROPE_DIM, 2, dtype=F32) / ROPE_DIM)
    ang = pos.astype(F32)[:, None] * inv
    z = jnp.zeros((pos.shape[0], LANES - ROPE_DIM), F32)
    c, s = jnp.cos(ang), jnp.sin(ang)
    return jnp.concatenate([c, c, z], axis=1), jnp.concatenate([s, s, z], axis=1)


def _chunk_rows(v, seq):
    B, D = v.shape
    return jnp.broadcast_to(v[:, None, None, :], (B, seq // CHUNK, 1, D)).reshape(B * (seq // CHUNK), 1, D)


def _layer(x, ada, ckv_past, kpe_past, conv_state, lru_state, pos0, W, T_in, T_tok):
    B, S, _ = x.shape
    L = 0 if ckv_past is None else ckv_past.shape[1]
    sh1, sc1, gt1, sh2, sc2, gt2 = jnp.split(ada, 6, axis=-1)
    rope_c, rope_s = _rope_tables(pos0 + jnp.arange(S))
    ma, sgb, q, ckv, kpe, kpe128, conv_new, lru_new = _mixer_in(
        x, sh1[:, None, :], sc1[:, None, :], conv_state, lru_state[:, None, :], rope_c, rope_s, W, T_in)

    if L == 0:
        k, vt = _kv_up(ckv.reshape(B * S, KV_LORA), kpe128.reshape(B * S, LANES), W["w_k_up"], W["w_v_up_t"],
                       ATTN_BLOCK, True)
        o = _attn_prompt(q, k.reshape(B, S, -1), vt, ATTN_BLOCK, ATTN_HEADS_PER_STEP)
    else:
        ckv_all = jnp.concatenate([ckv_past, ckv], axis=1)
        kpe_past128 = jnp.pad(kpe_past, ((0, 0), (0, 0), (0, LANES - ROPE_DIM))).astype(BF16)
        kpe_all = jnp.concatenate([kpe_past128, kpe128], axis=1)
        LK = L + S
        k, v = _kv_up(ckv_all.reshape(B * LK, KV_LORA), kpe_all.reshape(B * LK, LANES), W["w_k_up"], W["w_v_up"],
                      256, False)
        o = _attn_sample(q, k.reshape(B, LK, -1), v.reshape(B, LK, -1))

    N = B * S
    x1, h2, route, cnt = _merge(o.reshape(N, D_MODEL), ma.reshape(N, D_MODEL), sgb.reshape(N, D_MODEL),
                                x.reshape(N, D_MODEL), _chunk_rows(gt1, S), _chunk_rows(sh2, S),
                                _chunk_rows(sc2, S), W, T_tok)
    cnt = cnt[:, 0, ROUTE_OFF:ROUTE_OFF + N_EXPERTS].reshape(-1)
    y = _moe(h2, route, cnt, x1, _chunk_rows(gt2, S), W, MOE_TILE, T_tok, MOE_CHUNK, MOE_EXPERTS_PER_STEP)
    return y.reshape(B, S, D_MODEL), ckv, kpe, conv_new, lru_new.reshape(B, D_RNN)


def kernel(x_prompt, x_sample, c_prompt, c_sample, cache_ckv, cache_kpe, state_conv, state_rglru, w_ada, b_ada, g_pre1, g_post1, g_pre2, g_post2, w_in, w_conv, b_conv, w_rgate, b_rgate, w_igate, b_igate, lru_lambda, w_rnn_out, g_q_lat, w_q_up, g_kv_lat, w_k_up, w_v_up, w_attn_out, w_out, w_group, b_group, w_erouter, b_erouter, w_exp_gate, w_exp_up, w_exp_down):
    assert w_in.shape[0] == 1, "single-layer trunk"
    B = x_prompt.shape[0]
    wi = w_in[0]
    sp = lambda a, b: wi[:, a:b]
    xr, gr = sp(0, D_RNN), sp(D_RNN, 2 * D_RNN)
    o = 2 * D_RNN
    ql, kvl, kr = sp(o, o + Q_LORA), sp(o + Q_LORA, o + Q_LORA + KV_LORA), \
        sp(o + Q_LORA + KV_LORA, o + Q_LORA + KV_LORA + ROPE_DIM)
    o = o + Q_LORA + KV_LORA + ROPE_DIM
    ga, gb = sp(o, o + D_MODEL), sp(o + D_MODEL, o + 2 * D_MODEL)
    wq = w_q_up[0].reshape(Q_LORA, N_HEADS, QK_NOPE + ROPE_DIM)
    wq_pe = wq[..., QK_NOPE:]
    row = lambda a: a[0].reshape(1, -1)
    W = {
        "g_pre1": row(g_pre1), "g_post1": row(g_post1), "g_pre2": row(g_pre2), "g_post2": row(g_post2),
        "w_in2": jnp.concatenate([xr, gr, ql, kvl, kr, _rotate_half_cols(kr), ga, gb], axis=1).astype(BF16),
        "w_conv": w_conv[0], "b_conv": row(b_conv),
        "w_gates": jnp.concatenate([w_rgate[0], w_igate[0]], axis=-1).astype(BF16),
        "b_rgate": row(b_rgate), "b_igate": row(b_igate), "lam": row(lru_lambda),
        "w_rnn_out": w_rnn_out[0].astype(BF16),
        "g_q": row(g_q_lat), "g_kv": row(g_kv_lat),
        "w_qup": jnp.concatenate([wq[..., :QK_NOPE], wq_pe, _rotate_half_cols(wq_pe)], axis=-1)
                 .reshape(Q_LORA, N_HEADS * HEAD_K).astype(BF16),
        "w_k_up": w_k_up[0].astype(BF16), "w_v_up": w_v_up[0].astype(BF16),
        "w_v_up_t": w_v_up[0].T.astype(BF16),
        "w_attn_out": w_attn_out[0].astype(BF16), "w_out": w_out[0].astype(BF16),
        "w_route": jnp.pad(jnp.concatenate([w_group[0], w_erouter[0]], axis=1),
                           ((0, 0), (0, LANES - N_GROUPS - N_EXPERTS))).astype(BF16),
        "b_route": jnp.pad(jnp.concatenate([b_group[0], b_erouter[0]]), (0, LANES - N_GROUPS - N_EXPERTS))
                   .reshape(1, LANES),
        "w_exp_gate": w_exp_gate[0].astype(BF16), "w_exp_up": w_exp_up[0].astype(BF16),
        "w_exp_down": w_exp_down[0].astype(BF16),
    }
    ada = _ada(jnp.concatenate([c_prompt, c_sample], axis=0), w_ada[0], b_ada[0])
    zeros_conv = jnp.zeros((B, CONV_W - 1, D_RNN), F32)
    zeros_lru = jnp.zeros((B, D_RNN), F32)
    yp, ckv_p, kpe_p, conv_p, lru_p = _layer(x_prompt, ada[:B], None, None, zeros_conv, zeros_lru, 0, W, 512, 512)
    ys, ckv_s, kpe_s, conv_s, lru_s = _layer(x_sample, ada[B:], cache_ckv[0], cache_kpe[0], state_conv[0],
                                             state_rglru[0], cache_ckv.shape[2], W, 64, 512)
    return (yp, ys, ckv_p[None], kpe_p[None], conv_p[None], lru_p[None],
            ckv_s[None], kpe_s[None], conv_s[None], lru_s[None])
```

```python
import functools

import jax
import jax.numpy as jnp
from jax import lax
from jax.experimental import pallas as pl
from jax.experimental.pallas import tpu as pltpu

F32 = jnp.float32
BF16 = jnp.bfloat16

D_MODEL = 1024
CHUNK = 64
D_RNN = 1024
N_RNN_BLOCKS = 8
RNN_BLOCK = D_RNN // N_RNN_BLOCKS
CONV_W = 4
LRU_C = 8.0
N_HEADS = 8
QK_NOPE = 128
ROPE_DIM = 64
V_HEAD = 128
Q_LORA = 384
KV_LORA = 256
ROPE_THETA = 10000.0
SM_SCALE = (QK_NOPE + ROPE_DIM) ** -0.5
LOG2E = 1.4426950408889634
Q_SCALE = SM_SCALE * LOG2E
N_GROUPS = 4
EXP_PER_GROUP = 4
N_EXPERTS = N_GROUPS * EXP_PER_GROUP
D_EXPERT = 512
EPS = 1e-6

LANES = 128
SUBLANES = 8
HEAD_K = QK_NOPE + 2 * ROPE_DIM
OFF_XR = 0
OFF_GR = OFF_XR + D_RNN
OFF_QL = OFF_GR + D_RNN
OFF_KVL = OFF_QL + Q_LORA
OFF_KR = OFF_KVL + KV_LORA
OFF_GA = OFF_KR + 2 * ROPE_DIM
OFF_GB = OFF_GA + D_MODEL
IN_COLS2 = OFF_GB + D_MODEL
ROUTE_OFF = N_GROUPS
CONV_PAD = 8
NEG = -1e30
GELU_C0 = 0.7978845608028654
GELU_C1 = GELU_C0 * 0.044715
ATTN_BLOCK = 512
MOE_TILE = 1024
KEY_STRIDE = 2048
MOE_EXPERTS_PER_STEP = 1
MOE_CHUNK = 192
ATTN_HEADS_PER_STEP = 2
VMEM_LIMIT = 56 * 1024 * 1024


def _cparams(sem):
    return pltpu.CompilerParams(dimension_semantics=sem, vmem_limit_bytes=VMEM_LIMIT)


def _const_spec(shape):
    n = len(shape)
    return pl.BlockSpec(shape, lambda *_: (0,) * n, pipeline_mode=pl.Buffered(1))


def _rms(x, g):
    return x * lax.rsqrt(jnp.mean(x * x, axis=-1, keepdims=True) + EPS) * g


def _dot(a, b):
    return jnp.dot(a, b, preferred_element_type=F32)


def _sigmoid(x):
    return 0.5 * jnp.tanh(0.5 * x) + 0.5


def _gelu_tanh(x):
    hx = 0.5 * x
    return hx + hx * jnp.tanh(x * (GELU_C0 + GELU_C1 * (x * x)))


def _rope(v, c, s):
    return v * c + pltpu.roll(v, ROPE_DIM, axis=1) * s


def _ada_kernel(c_ref, w_ref, b_ref, o_ref):
    c = c_ref[...]
    s = c * jax.nn.sigmoid(c)
    o_ref[...] = jnp.dot(s, w_ref[...], preferred_element_type=F32,
                         precision=lax.Precision.HIGHEST) + b_ref[...]


def _ada(c, w, b):
    nb = c.shape[0]
    n = w.shape[1]
    bn = n // 6
    return pl.pallas_call(
        _ada_kernel,
        grid=(n // bn,),
        in_specs=[_const_spec((nb, D_MODEL)),
                  pl.BlockSpec((D_MODEL, bn), lambda j: (0, j)),
                  pl.BlockSpec((1, bn), lambda j: (0, j))],
        out_specs=pl.BlockSpec((nb, bn), lambda j: (0, j)),
        out_shape=jax.ShapeDtypeStruct((nb, n), F32),
        compiler_params=_cparams(("arbitrary",)),
        name="ada",
    )(c, w, b.reshape(1, n))


def _mixer_in_kernel(x_ref, sh_ref, sc_ref, gpre_ref, win_ref, cst_ref, lst_ref, wconv_ref, bconv_ref,
                     wgate_ref, br_ref, bi_ref, lam_ref, wrnn_ref, gq_ref, wqup_ref, gkv_ref,
                     rc_ref, rs_ref,
                     ma_ref, sgb_ref, q_ref, ckv_ref, kpe_ref, kpe128_ref, cout_ref, lout_ref,
                     xbuf, b_scr, hcar, *, T):
    t = pl.program_id(1)

    @pl.when(t == 0)
    def _():
        xbuf[0:CONV_PAD, :] = jnp.zeros((CONV_PAD, D_RNN), F32)
        xbuf[CONV_PAD - (CONV_W - 1):CONV_PAD, :] = cst_ref[0]
        hcar[...] = lst_ref[0]

    x = x_ref[0]
    h = _rms(x, gpre_ref[...]) * (1.0 + sc_ref[0]) + sh_ref[0]
    hb = h.astype(BF16)

    groups = T // SUBLANES
    xbuf[CONV_PAD:CONV_PAD + T, :] = _dot(hb, win_ref[:, OFF_XR:OFF_XR + D_RNN])
    tail = xbuf[T + CONV_PAD - (CONV_W - 1):T + CONV_PAD, :]
    xall = xbuf[...].reshape(groups + 1, SUBLANES, D_RNN)
    row_wide = lax.broadcasted_iota(jnp.int32, (groups, SUBLANES, D_RNN), 1)
    xc = bconv_ref[...] + xall[1:] * wconv_ref[CONV_W - 1:CONV_W, :]
    for shift in range(1, CONV_W):
        rot = pltpu.roll(xall, shift, axis=1)
        shifted = jnp.where(row_wide >= shift, rot[1:], rot[:-1])
        xc = xc + shifted * wconv_ref[CONV_W - 1 - shift:CONV_W - shift, :]
    xc = xc.reshape(T, D_RNN)
    cout_ref[0] = tail
    xbuf[CONV_PAD - (CONV_W - 1):CONV_PAD, :] = tail

    lam = lam_ref[...]
    softplus_neg_lam = jnp.maximum(-lam, 0.0) + jnp.log1p(jnp.exp(-jnp.abs(lam)))
    row_in_group = lax.broadcasted_iota(jnp.int32, (groups, SUBLANES, RNN_BLOCK), 1)
    keeps = [row_in_group >= d for d in (1, 2, 4)]
    for n in range(N_RNN_BLOCKS):
        blk = slice(n * RNN_BLOCK, (n + 1) * RNN_BLOCK)
        xcb = xc[:, blk]
        g = _dot(xcb.astype(BF16), wgate_ref[n])
        r = _sigmoid(g[:, :RNN_BLOCK] + br_ref[:, blk])
        i = _sigmoid(g[:, RNN_BLOCK:] + bi_ref[:, blk])
        log_a = -LRU_C * r * softplus_neg_lam[:, blk]
        a = jnp.exp(log_a)
        z = -jnp.tanh(log_a) * (a * a + 1.0)
        b = jnp.where(z > 0.0, z * lax.rsqrt(z), 0.0) * (i * xcb)
        a = a.reshape(groups, SUBLANES, RNN_BLOCK)
        b = b.reshape(groups, SUBLANES, RNN_BLOCK)
        for keep, d in zip(keeps, (1, 2, 4)):
            a_prev = jnp.where(keep, pltpu.roll(a, d, axis=1), 1.0)
            b_prev = jnp.where(keep, pltpu.roll(b, d, axis=1), 0.0)
            b = b + a * b_prev
            a = a * a_prev
        hprev = hcar[:, blk]
        for grp in range(groups):
            rows = slice(grp * SUBLANES, (grp + 1) * SUBLANES)
            hg = b[grp] + a[grp] * hprev
            b_scr[rows, blk] = hg
            hprev = hg[SUBLANES - 1:SUBLANES, :]
        hcar[:, blk] = hprev
    lout_ref[0] = hcar[...]

    gr = _dot(hb, win_ref[:, OFF_GR:OFF_GR + D_RNN])
    y_a = _dot((b_scr[...] * _gelu_tanh(gr)).astype(BF16), wrnn_ref[...])
    ga = _dot(hb, win_ref[:, OFF_GA:OFF_GA + D_MODEL])
    ma_ref[0] = (_sigmoid(ga) * y_a).astype(BF16)
    gb = _dot(hb, win_ref[:, OFF_GB:OFF_GB + D_MODEL])
    sgb_ref[0] = _sigmoid(gb).astype(BF16)

    rc = rc_ref[...]
    rs = rs_ref[...]
    ql = _dot(hb, win_ref[:, OFF_QL:OFF_QL + Q_LORA])
    q = _dot(_rms(ql, gq_ref[...]).astype(BF16), wqup_ref[...])
    for hd in range(N_HEADS):
        base = hd * HEAD_K
        q_ref[0, :, base:base + QK_NOPE] = (q[:, base:base + QK_NOPE] * Q_SCALE).astype(BF16)
        pe = _rope(q[:, base + QK_NOPE:base + HEAD_K], rc, rs)
        q_ref[0, :, base + QK_NOPE:base + HEAD_K] = (pe * Q_SCALE).astype(BF16)
    kvl = _dot(hb, win_ref[:, OFF_KVL:OFF_KVL + KV_LORA])
    ckv_ref[0] = _rms(kvl, gkv_ref[...])
    kp = _rope(_dot(hb, win_ref[:, OFF_KR:OFF_KR + 2 * ROPE_DIM]), rc, rs)
    kpe_ref[0] = kp[:, :ROPE_DIM]
    kpe128_ref[0] = kp.astype(BF16)


def _mixer_in(x, sh1, sc1, conv_state, lru_state, rope_c, rope_s, W, T):
    B, S, _ = x.shape
    nt = S // T
    kern = functools.partial(_mixer_in_kernel, T=T)
    bt = lambda b, t: (b, t, 0)
    bo = lambda b, t: (b, 0, 0)
    tt = lambda b, t: (t, 0)
    in_specs = [
        pl.BlockSpec((1, T, D_MODEL), bt),
        pl.BlockSpec((1, 1, D_MODEL), bo),
        pl.BlockSpec((1, 1, D_MODEL), bo),
        _const_spec((1, D_MODEL)),
        _const_spec((D_MODEL, IN_COLS2)),
        pl.BlockSpec((1, CONV_W - 1, D_RNN), bo),
        pl.BlockSpec((1, 1, D_RNN), bo),
        _const_spec((CONV_W, D_RNN)),
        _const_spec((1, D_RNN)),
        _const_spec((N_RNN_BLOCKS, RNN_BLOCK, 2 * RNN_BLOCK)),
        _const_spec((1, D_RNN)),
        _const_spec((1, D_RNN)),
        _const_spec((1, D_RNN)),
        _const_spec((D_RNN, D_MODEL)),
        _const_spec((1, Q_LORA)),
        _const_spec((Q_LORA, N_HEADS * HEAD_K)),
        _const_spec((1, KV_LORA)),
        pl.BlockSpec((T, LANES), tt),
        pl.BlockSpec((T, LANES), tt),
    ]
    out_specs = [
        pl.BlockSpec((1, T, D_MODEL), bt),
        pl.BlockSpec((1, T, D_MODEL), bt),
        pl.BlockSpec((1, T, N_HEADS * HEAD_K), bt),
        pl.BlockSpec((1, T, KV_LORA), bt),
        pl.BlockSpec((1, T, ROPE_DIM), bt),
        pl.BlockSpec((1, T, LANES), bt),
        pl.BlockSpec((1, CONV_W - 1, D_RNN), bo),
        pl.BlockSpec((1, 1, D_RNN), bo),
    ]
    out_shape = [
        jax.ShapeDtypeStruct((B, S, D_MODEL), BF16),
        jax.ShapeDtypeStruct((B, S, D_MODEL), BF16),
        jax.ShapeDtypeStruct((B, S, N_HEADS * HEAD_K), BF16),
        jax.ShapeDtypeStruct((B, S, KV_LORA), F32),
        jax.ShapeDtypeStruct((B, S, ROPE_DIM), F32),
        jax.ShapeDtypeStruct((B, S, LANES), BF16),
        jax.ShapeDtypeStruct((B, CONV_W - 1, D_RNN), F32),
        jax.ShapeDtypeStruct((B, 1, D_RNN), F32),
    ]
    scratch = [
        pltpu.VMEM((T + CONV_PAD, D_RNN), F32),
        pltpu.VMEM((T, D_RNN), F32),
        pltpu.VMEM((1, D_RNN), F32),
    ]
    return pl.pallas_call(
        kern, grid=(B, nt), in_specs=in_specs, out_specs=out_specs, out_shape=out_shape,
        scratch_shapes=scratch, compiler_params=_cparams(("arbitrary", "arbitrary")),
        name="mixer_in",
    )(x, sh1, sc1, W["g_pre1"], W["w_in2"], conv_state, lru_state, W["w_conv"], W["b_conv"],
      W["w_gates"], W["b_rgate"], W["b_igate"], W["lam"], W["w_rnn_out"], W["g_q"], W["w_qup"],
      W["g_kv"], rope_c, rope_s)


def _kv_up_kernel(ckv_ref, kpe_ref, wk_ref, wv_ref, k_ref, v_ref, *, v_transposed):
    c = ckv_ref[...].astype(BF16)
    kn = _dot(c, wk_ref[...])
    kpe = kpe_ref[...]
    for hd in range(N_HEADS):
        base = hd * HEAD_K
        k_ref[:, base:base + QK_NOPE] = kn[:, hd * QK_NOPE:(hd + 1) * QK_NOPE].astype(BF16)
        k_ref[:, base + QK_NOPE:base + HEAD_K] = kpe
    if v_transposed:
        vt = lax.dot_general(wv_ref[...], c, (((1,), (1,)), ((), ())), preferred_element_type=F32)
        for hd in range(N_HEADS):
            v_ref[0, hd] = vt[hd * V_HEAD:(hd + 1) * V_HEAD, :].astype(BF16)
    else:
        v_ref[...] = _dot(c, wv_ref[...]).astype(BF16)


def _kv_up(ckv, kpe128, wk, wv, T, v_transposed):
    R = ckv.shape[0]
    row = lambda i: (i, 0)
    if v_transposed:
        v_spec = pl.BlockSpec((1, N_HEADS, V_HEAD, T), lambda i: (i, 0, 0, 0))
        v_shape = jax.ShapeDtypeStruct((R // T, N_HEADS, V_HEAD, T), BF16)
    else:
        v_spec = pl.BlockSpec((T, N_HEADS * V_HEAD), row)
        v_shape = jax.ShapeDtypeStruct((R, N_HEADS * V_HEAD), BF16)
    return pl.pallas_call(
        functools.partial(_kv_up_kernel, v_transposed=v_transposed), grid=(R // T,),
        in_specs=[pl.BlockSpec((T, KV_LORA), row), pl.BlockSpec((T, LANES), row),
                  _const_spec(wk.shape), _const_spec(wv.shape)],
        out_specs=[pl.BlockSpec((T, N_HEADS * HEAD_K), row), v_spec],
        out_shape=[jax.ShapeDtypeStruct((R, N_HEADS * HEAD_K), BF16), v_shape],
        compiler_params=_cparams(("arbitrary",)),
        name="kv_up",
    )(ckv, kpe128, wk, wv)


def _attn_prompt_kernel(q_ref, k_ref, vt_ref, o_ref, qt_scr, sa, sb, xa, xb, m_scr, l_scr, acc_scr, *, QB, HP):
    qi = pl.program_id(2)
    s0, s1 = (sa, xa), (sb, xb)
    for hh in range(HP):
        qt_scr[hh] = q_ref[0, :, hh * HEAD_K:(hh + 1) * HEAD_K].T
    m_scr[...] = jnp.full(m_scr.shape, NEG, F32)
    l_scr[...] = jnp.zeros(l_scr.shape, F32)
    acc_scr[...] = jnp.zeros(acc_scr.shape, F32)

    def scores(j, dst):
        start = pl.multiple_of(j * QB, QB)
        for hh in range(HP):
            s = _dot(k_ref[0, pl.ds(start, QB), hh * HEAD_K:(hh + 1) * HEAD_K], qt_scr[hh])
            dst[0][hh] = s
            dst[1][hh] = jnp.max(s, axis=0, keepdims=True)

    def update(j, src, masked):
        for hh in range(HP):
            s = src[0][hh]
            if masked:
                ck = lax.broadcasted_iota(jnp.int32, (QB, QB), 0) // CHUNK
                cq = lax.broadcasted_iota(jnp.int32, (QB, QB), 1) // CHUNK
                s = jnp.where(ck <= cq, s, NEG)
                smax = jnp.max(s, axis=0, keepdims=True)
            else:
                smax = src[1][hh]
            m_old = m_scr[hh]
            m_new = jnp.maximum(m_old, smax)
            p = jnp.exp2(s - m_new)
            alpha = jnp.exp2(m_old - m_new)
            l_scr[hh] = alpha * l_scr[hh] + jnp.sum(p, axis=0, keepdims=True)
            acc_scr[hh] = alpha * acc_scr[hh] + _dot(vt_ref[j, hh], p.astype(BF16))
            m_scr[hh] = m_new

    scores(0, s0)

    def pair(jj, c):
        j = 2 * jj
        scores(j + 1, s1)
        update(j, s0, False)
        scores(j + 2, s0)
        update(j + 1, s1, False)
        return c

    lax.fori_loop(0, qi // 2, pair, 0)

    @pl.when(qi % 2 == 0)
    def _():
        update(qi, s0, True)

    @pl.when(qi % 2 == 1)
    def _():
        scores(qi, s1)
        update(qi - 1, s0, False)
        update(qi, s1, True)

    for hh in range(HP):
        o_ref[0, :, hh * V_HEAD:(hh + 1) * V_HEAD] = (acc_scr[hh] / l_scr[hh]).T.astype(BF16)


def _attn_prompt(q, k, vt, QB, HP):
    B, S, _ = q.shape
    nkb = S // QB
    kern = functools.partial(_attn_prompt_kernel, QB=QB, HP=HP)
    return pl.pallas_call(
        kern, grid=(B, N_HEADS // HP, S // QB),
        in_specs=[pl.BlockSpec((1, QB, HP * HEAD_K), lambda b, h, i: (b, i, h)),
                  pl.BlockSpec((1, S, HP * HEAD_K), lambda b, h, i: (b, 0, h)),
                  pl.BlockSpec((nkb, HP, V_HEAD, QB), lambda b, h, i: (b, h, 0, 0))],
        out_specs=pl.BlockSpec((1, QB, HP * V_HEAD), lambda b, h, i: (b, i, h)),
        out_shape=jax.ShapeDtypeStruct((B, S, N_HEADS * V_HEAD), BF16),
        scratch_shapes=[pltpu.VMEM((HP, HEAD_K, QB), BF16),
                        pltpu.VMEM((HP, QB, QB), F32), pltpu.VMEM((HP, QB, QB), F32),
                        pltpu.VMEM((HP, 1, QB), F32), pltpu.VMEM((HP, 1, QB), F32),
                        pltpu.VMEM((HP, 1, QB), F32),
                        pltpu.VMEM((HP, 1, QB), F32), pltpu.VMEM((HP, V_HEAD, QB), F32)],
        compiler_params=_cparams(("arbitrary", "arbitrary", "arbitrary")),
        name="attn_prompt",
    )(q, k, vt)


def _attn_sample_kernel(q_ref, k_ref, v_ref, o_ref):
    for hd in range(N_HEADS):
        q = q_ref[0, :, hd * HEAD_K:(hd + 1) * HEAD_K]
        k = k_ref[0, :, hd * HEAD_K:(hd + 1) * HEAD_K]
        s = lax.dot_general(q, k, (((1,), (1,)), ((), ())), preferred_element_type=F32)
        p = jnp.exp2(s - jnp.max(s, axis=1, keepdims=True))
        l = jnp.sum(p, axis=1, keepdims=True)
        o = _dot(p.astype(BF16), v_ref[0, :, hd * V_HEAD:(hd + 1) * V_HEAD])
        o_ref[0, :, hd * V_HEAD:(hd + 1) * V_HEAD] = (o / l).astype(BF16)


def _attn_sample(q, k, v):
    B, S, _ = q.shape
    LK = k.shape[1]
    b3 = lambda b: (b, 0, 0)
    return pl.pallas_call(
        _attn_sample_kernel, grid=(B,),
        in_specs=[pl.BlockSpec((1, S, N_HEADS * HEAD_K), b3),
                  pl.BlockSpec((1, LK, N_HEADS * HEAD_K), b3),
                  pl.BlockSpec((1, LK, N_HEADS * V_HEAD), b3)],
        out_specs=pl.BlockSpec((1, S, N_HEADS * V_HEAD), b3),
        out_shape=jax.ShapeDtypeStruct((B, S, N_HEADS * V_HEAD), BF16),
        compiler_params=_cparams(("arbitrary",)),
        name="attn_sample",
    )(q, k, v)


def _lane_min_where(mask, lane):
    return jnp.min(jnp.where(mask, lane, LANES), axis=1, keepdims=True)


def _merge_kernel(o_ref, ma_ref, sgb_ref, x_ref, gt1_ref, sh2_ref, sc2_ref, gpost1_ref, gpre2_ref,
                  wao_ref, wout_ref, wr_ref, br_ref, x1_ref, h2_ref, route_ref, cnt_ref, *, T):
    nc = T // CHUNK
    y_b = _dot(o_ref[...], wao_ref[...])
    m = ma_ref[...].astype(F32) + sgb_ref[...].astype(F32) * y_b
    y = _dot(m.astype(BF16), wout_ref[...])
    yn = _rms(y, gpost1_ref[...]).reshape(nc, CHUNK, D_MODEL)
    x1 = x_ref[...].reshape(nc, CHUNK, D_MODEL) + gt1_ref[...] * yn
    x1_ref[...] = x1.reshape(T, D_MODEL)
    h2 = (_rms(x1, gpre2_ref[...]) * (1.0 + sc2_ref[...]) + sh2_ref[...]).reshape(T, D_MODEL)
    h2b = h2.astype(BF16)
    h2_ref[...] = h2b

    logits = _dot(h2b, wr_ref[...]) + br_ref[...]
    lane = lax.broadcasted_iota(jnp.int32, (T, LANES), 1)
    is_g = lane < N_GROUPS
    gmax = jnp.max(jnp.where(is_g, logits, NEG), axis=1, keepdims=True)
    gidx = _lane_min_where(is_g & (logits == gmax), lane)
    gsum = jnp.sum(jnp.where(is_g, jnp.exp(logits - gmax), 0.0), axis=1, keepdims=True)
    lo = ROUTE_OFF + EXP_PER_GROUP * gidx
    sel = (lane >= lo) & (lane < lo + EXP_PER_GROUP)
    m1 = jnp.max(jnp.where(sel, logits, NEG), axis=1, keepdims=True)
    i1 = _lane_min_where(sel & (logits == m1), lane)
    rest = sel & (lane != i1)
    m2 = jnp.max(jnp.where(rest, logits, NEG), axis=1, keepdims=True)
    i2 = _lane_min_where(rest & (logits == m2), lane)
    e2 = jnp.exp(m2 - m1)
    w1 = 1.0 / (gsum * (1.0 + e2))
    w2 = w1 * e2
    uses = jnp.where((lane == i1) | (lane == i2), 1.0, 0.0)
    earlier = lax.broadcasted_iota(jnp.int32, (T, T), 1) < lax.broadcasted_iota(jnp.int32, (T, T), 0)
    rank = _dot(jnp.where(earlier, 1.0, 0.0).astype(BF16), uses.astype(BF16))
    cnt_ref[0] = (rank[T - 1:T, :] + uses[T - 1:T, :]).astype(jnp.int32)
    rank1 = jnp.sum(jnp.where(lane == i1, rank, 0.0), axis=1, keepdims=True)
    rank2 = jnp.sum(jnp.where(lane == i2, rank, 0.0), axis=1, keepdims=True)
    fields = (i1.astype(F32), i2.astype(F32), w1, w2, rank1, rank2)
    row = jnp.zeros((T, LANES), F32)
    for f, val in enumerate(fields):
        row = jnp.where(lane == f, val, row)
    route_ref[...] = row


def _merge(o, ma, sgb, x, gt1c, sh2c, sc2c, W, T):
    N = x.shape[0]
    nc = T // CHUNK
    row = lambda i: (i, 0)
    chunk = lambda i: (i, 0, 0)
    kern = functools.partial(_merge_kernel, T=T)
    return pl.pallas_call(
        kern, grid=(N // T,),
        in_specs=[pl.BlockSpec((T, D_MODEL), row), pl.BlockSpec((T, D_MODEL), row),
                  pl.BlockSpec((T, D_MODEL), row), pl.BlockSpec((T, D_MODEL), row),
                  pl.BlockSpec((nc, 1, D_MODEL), chunk), pl.BlockSpec((nc, 1, D_MODEL), chunk),
                  pl.BlockSpec((nc, 1, D_MODEL), chunk),
                  _const_spec((1, D_MODEL)), _const_spec((1, D_MODEL)),
                  _const_spec((D_MODEL, D_MODEL)), _const_spec((D_MODEL, D_MODEL)),
                  _const_spec((D_MODEL, LANES)), _const_spec((1, LANES))],
        out_specs=[pl.BlockSpec((T, D_MODEL), row), pl.BlockSpec((T, D_MODEL), row),
                   pl.BlockSpec((T, LANES), row), pl.BlockSpec((1, 1, LANES), chunk)],
        out_shape=[jax.ShapeDtypeStruct((N, D_MODEL), F32), jax.ShapeDtypeStruct((N, D_MODEL), BF16),
                   jax.ShapeDtypeStruct((N, LANES), F32), jax.ShapeDtypeStruct((N // T, 1, LANES), jnp.int32)],
        compiler_params=_cparams(("arbitrary",)),
        name="merge",
    )(o, ma, sgb, x, gt1c, sh2c, sc2c, W["g_post1"], W["g_pre2"], W["w_attn_out"], W["w_out"],
      W["w_route"], W["b_route"])


def _moe_kernel(cnt_ref, h_ref, route_ref, x1_ref, gt2_ref, gpost2_ref, wg_ref, wu_ref, wd_ref, y_ref,
                key_l, key_s, comb_w, *, T, TR, CH, NE):
    i = pl.program_id(0)
    s = pl.program_id(1)
    nc = T // CHUNK
    nr = T // TR

    def keys(sel, rank, tok):
        expert = sel.astype(jnp.int32) - ROUTE_OFF
        rank = rank.astype(jnp.int32)
        for r in range(1, nr):
            before = jnp.zeros_like(rank)
            for x in range(N_EXPERTS):
                before = jnp.where(expert == x, cnt_ref[(i * nr + r - 1) * N_EXPERTS + x], before)
            rank = rank + jnp.where(tok >= r * TR, before, 0)
        return expert * KEY_STRIDE + rank

    @pl.when(s == 0)
    def _():
        y_ref[...] = jnp.zeros(y_ref.shape, F32)
        rt = route_ref[...].T
        tok_l = lax.broadcasted_iota(jnp.int32, (1, T), 1)
        tok_s = lax.broadcasted_iota(jnp.int32, (T, 1), 0)
        key_l[0:1, :] = keys(rt[0:1, :], rt[4:5, :], tok_l)
        key_l[1:2, :] = keys(rt[1:2, :], rt[5:6, :], tok_l)
        key_s[0] = keys(route_ref[:, 0:1], route_ref[:, 4:5], tok_s)
        key_s[1] = keys(route_ref[:, 1:2], route_ref[:, 5:6], tok_s)
        key_s[2] = route_ref[:, 0:1].astype(jnp.int32) - ROUTE_OFF
        comb_w[0] = route_ref[:, 2:3]
        comb_w[1] = route_ref[:, 3:4]

    k1_l, k2_l = key_l[0:1, :], key_l[1:2, :]
    k1_s, k2_s = key_s[0], key_s[1]
    experts = [s * NE + x for x in range(NE)]
    totals = []
    for ex in experts:
        total = cnt_ref[(i * nr) * N_EXPERTS + ex]
        for r in range(1, nr):
            total = total + cnt_ref[(i * nr + r) * N_EXPERTS + ex]
        totals.append(total)
    combs = [jnp.where(key_s[2] == ex, comb_w[0], comb_w[1]) for ex in experts]

    def chunk(c, carry):
        acc = None
        for x, ex in enumerate(experts):
            base = ex * KEY_STRIDE + c * CH
            want_s = base + lax.broadcasted_iota(jnp.int32, (CH, 1), 0)
            pick = (k1_l == want_s) | (k2_l == want_s)
            xg = _dot(jnp.where(pick, 1.0, 0.0).astype(BF16), h_ref[...]).astype(BF16)
            g = _dot(xg, wg_ref[x])
            u = _dot(xg, wu_ref[x])
            hid = (g * jax.nn.sigmoid(g) * u).astype(BF16)
            out = _dot(hid, wd_ref[x]).astype(BF16)
            want_l = base + lax.broadcasted_iota(jnp.int32, (1, CH), 1)
            put = (k1_s == want_l) | (k2_s == want_l)
            part = combs[x] * _dot(jnp.where(put, 1.0, 0.0).astype(BF16), out)
            acc = part if acc is None else acc + part
        y_ref[...] += acc
        return carry

    most = totals[0]
    for total in totals[1:]:
        most = jnp.maximum(most, total)
    lax.fori_loop(0, (most + CH - 1) // CH, chunk, 0)

    @pl.when(s == N_EXPERTS // NE - 1)
    def _():
        on = _rms(y_ref[...], gpost2_ref[...]).reshape(nc, CHUNK, D_MODEL)
        y = x1_ref[...].reshape(nc, CHUNK, D_MODEL) + gt2_ref[...] * on
        y_ref[...] = y.reshape(T, D_MODEL)


def _moe(h2, route, cnt, x1, gt2c, W, T, TR, CH, NE):
    N = h2.shape[0]
    nc = T // CHUNK
    row = lambda i, e, c: (i, 0)
    wsel = lambda i, e, c: (e, 0, 0)
    kern = functools.partial(_moe_kernel, T=T, TR=TR, CH=CH, NE=NE)
    grid_spec = pltpu.PrefetchScalarGridSpec(
        num_scalar_prefetch=1, grid=(N // T, N_EXPERTS // NE),
        in_specs=[pl.BlockSpec((T, D_MODEL), row), pl.BlockSpec((T, LANES), row),
                  pl.BlockSpec((T, D_MODEL), row),
                  pl.BlockSpec((nc, 1, D_MODEL), lambda i, e, c: (i, 0, 0)),
                  pl.BlockSpec((1, D_MODEL), lambda i, e, c: (0, 0)),
                  pl.BlockSpec((NE, D_MODEL, D_EXPERT), wsel),
                  pl.BlockSpec((NE, D_MODEL, D_EXPERT), wsel),
                  pl.BlockSpec((NE, D_EXPERT, D_MODEL), wsel)],
        out_specs=pl.BlockSpec((T, D_MODEL), row),
        scratch_shapes=[pltpu.VMEM((8, T), jnp.int32), pltpu.VMEM((3, T, 1), jnp.int32),
                        pltpu.VMEM((2, T, 1), F32)])
    return pl.pallas_call(
        kern, grid_spec=grid_spec,
        out_shape=jax.ShapeDtypeStruct((N, D_MODEL), F32),
        compiler_params=_cparams(("arbitrary", "arbitrary")),
        name="moe",
    )(cnt, h2, route, x1, gt2c, W["g_post2"], W["w_exp_gate"], W["w_exp_up"], W["w_exp_down"])


def _rotate_half_cols(w):
    half = ROPE_DIM // 2
    return jnp.concatenate([-w[..., half:], w[..., :half]], axis=-1)


def _rope_tables(pos):
    inv = ROPE_THETA ** (-jnp.arange(0, ROPE_DIM, 2, dtype=F32) / ROPE_DIM)
    ang = pos.astype(F32)[:, None] * inv
    z = jnp.zeros((pos.shape[0], LANES - ROPE_DIM), F32)
    c, s = jnp.cos(ang), jnp.sin(ang)
    return jnp.concatenate([c, c, z], axis=1), jnp.concatenate([s, s, z], axis=1)


def _chunk_rows(v, seq):
    B, D = v.shape
    return jnp.broadcast_to(v[:, None, None, :], (B, seq // CHUNK, 1, D)).reshape(B * (seq // CHUNK), 1, D)


def _layer(x, ada, ckv_past, kpe_past, conv_state, lru_state, pos0, W, T_in, T_tok):
    B, S, _ = x.shape
    L = 0 if ckv_past is None else ckv_past.shape[1]
    sh1, sc1, gt1, sh2, sc2, gt2 = jnp.split(ada, 6, axis=-1)
    rope_c, rope_s = _rope_tables(pos0 + jnp.arange(S))
    ma, sgb, q, ckv, kpe, kpe128, conv_new, lru_new = _mixer_in(
        x, sh1[:, None, :], sc1[:, None, :], conv_state, lru_state[:, None, :], rope_c, rope_s, W, T_in)

    if L == 0:
        k, vt = _kv_up(ckv.reshape(B * S, KV_LORA), kpe128.reshape(B * S, LANES), W["w_k_up"], W["w_v_up_t"],
                       ATTN_BLOCK, True)
        o = _attn_prompt(q, k.reshape(B, S, -1), vt, ATTN_BLOCK, ATTN_HEADS_PER_STEP)
    else:
        ckv_all = jnp.concatenate([ckv_past, ckv], axis=1)
        kpe_past128 = jnp.pad(kpe_past, ((0, 0), (0, 0), (0, LANES - ROPE_DIM))).astype(BF16)
        kpe_all = jnp.concatenate([kpe_past128, kpe128], axis=1)
        LK = L + S
        k, v = _kv_up(ckv_all.reshape(B * LK, KV_LORA), kpe_all.reshape(B * LK, LANES), W["w_k_up"], W["w_v_up"],
                      256, False)
        o = _attn_sample(q, k.reshape(B, LK, -1), v.reshape(B, LK, -1))

    N = B * S
    x1, h2, route, cnt = _merge(o.reshape(N, D_MODEL), ma.reshape(N, D_MODEL), sgb.reshape(N, D_MODEL),
                                x.reshape(N, D_MODEL), _chunk_rows(gt1, S), _chunk_rows(sh2, S),
                                _chunk_rows(sc2, S), W, T_tok)
    cnt = cnt[:, 0, ROUTE_OFF:ROUTE_OFF + N_EXPERTS].reshape(-1)
    y = _moe(h2, route, cnt, x1, _chunk_rows(gt2, S), W, MOE_TILE, T_tok, MOE_CHUNK, MOE_EXPERTS_PER_STEP)
    return y.reshape(B, S, D_MODEL), ckv, kpe, conv_new, lru_new.reshape(B, D_RNN)


def kernel(x_prompt, x_sample, c_prompt, c_sample, cache_ckv, cache_kpe, state_conv, state_rglru, w_ada, b_ada, g_pre1, g_post1, g_pre2, g_post2, w_in, w_conv, b_conv, w_rgate, b_rgate, w_igate, b_igate, lru_lambda, w_rnn_out, g_q_lat, w_q_up, g_kv_lat, w_k_up, w_v_up, w_attn_out, w_out, w_group, b_group, w_erouter, b_erouter, w_exp_gate, w_exp_up, w_exp_down):
    assert w_in.shape[0] == 1, "single-layer trunk"
    B = x_prompt.shape[0]
    wi = w_in[0]
    sp = lambda a, b: wi[:, a:b]
    xr, gr = sp(0, D_RNN), sp(D_RNN, 2 * D_RNN)
    o = 2 * D_RNN
    ql, kvl, kr = sp(o, o + Q_LORA), sp(o + Q_LORA, o + Q_LORA + KV_LORA), \
        sp(o + Q_LORA + KV_LORA, o + Q_LORA + KV_LORA + ROPE_DIM)
    o = o + Q_LORA + KV_LORA + ROPE_DIM
    ga, gb = sp(o, o + D_MODEL), sp(o + D_MODEL, o + 2 * D_MODEL)
    wq = w_q_up[0].reshape(Q_LORA, N_HEADS, QK_NOPE + ROPE_DIM)
    wq_pe = wq[..., QK_NOPE:]
    row = lambda a: a[0].reshape(1, -1)
    W = {
        "g_pre1": row(g_pre1), "g_post1": row(g_post1), "g_pre2": row(g_pre2), "g_post2": row(g_post2),
        "w_in2": jnp.concatenate([xr, gr, ql, kvl, kr, _rotate_half_cols(kr), ga, gb], axis=1).astype(BF16),
        "w_conv": w_conv[0], "b_conv": row(b_conv),
        "w_gates": jnp.concatenate([w_rgate[0], w_igate[0]], axis=-1).astype(BF16),
        "b_rgate": row(b_rgate), "b_igate": row(b_igate), "lam": row(lru_lambda),
        "w_rnn_out": w_rnn_out[0].astype(BF16),
        "g_q": row(g_q_lat), "g_kv": row(g_kv_lat),
        "w_qup": jnp.concatenate([wq[..., :QK_NOPE], wq_pe, _rotate_half_cols(wq_pe)], axis=-1)
                 .reshape(Q_LORA, N_HEADS * HEAD_K).astype(BF16),
        "w_k_up": w_k_up[0].astype(BF16), "w_v_up": w_v_up[0].astype(BF16),
        "w_v_up_t": w_v_up[0].T.astype(BF16),
        "w_attn_out": w_attn_out[0].astype(BF16), "w_out": w_out[0].astype(BF16),
        "w_route": jnp.pad(jnp.concatenate([w_group[0], w_erouter[0]], axis=1),
                           ((0, 0), (0, LANES - N_GROUPS - N_EXPERTS))).astype(BF16),
        "b_route": jnp.pad(jnp.concatenate([b_group[0], b_erouter[0]]), (0, LANES - N_GROUPS - N_EXPERTS))
                   .reshape(1, LANES),
        "w_exp_gate": w_exp_gate[0].astype(BF16), "w_exp_up": w_exp_up[0].astype(BF16),
        "w_exp_down": w_exp_down[0].astype(BF16),
    }
    ada = _ada(jnp.concatenate([c_prompt, c_sample], axis=0), w_ada[0], b_ada[0])
    zeros_conv = jnp.zeros((B, CONV_W - 1, D_RNN), F32)
    zeros_lru = jnp.zeros((B, D_RNN), F32)
    yp, ckv_p, kpe_p, conv_p, lru_p = _layer(x_prompt, ada[:B], None, None, zeros_conv, zeros_lru, 0, W, 512, 512)
    ys, ckv_s, kpe_s, conv_s, lru_s = _layer(x_sample, ada[B:], cache_ckv[0], cache_kpe[0], state_conv[0],
                                             state_rglru[0], cache_ckv.shape[2], W, 64, 512)
    return (yp, ys, ckv_p[None], kpe_p[None], conv_p[None], lru_p[None],
            ckv_s[None], kpe_s[None], conv_s[None], lru_s[None])
```

```python
import functools

import jax
import jax.numpy as jnp
from jax import lax
from jax.experimental import pallas as pl
from jax.experimental.pallas import tpu as pltpu

F32 = jnp.float32
BF16 = jnp.bfloat16

D_MODEL = 1024
CHUNK = 64
D_RNN = 1024
N_RNN_BLOCKS = 8
RNN_BLOCK = D_RNN // N_RNN_BLOCKS
CONV_W = 4
LRU_C = 8.0
N_HEADS = 8
QK_NOPE = 128
ROPE_DIM = 64
V_HEAD = 128
Q_LORA = 384
KV_LORA = 256
ROPE_THETA = 10000.0
SM_SCALE = (QK_NOPE + ROPE_DIM) ** -0.5
LOG2E = 1.4426950408889634
Q_SCALE = SM_SCALE * LOG2E
N_GROUPS = 4
EXP_PER_GROUP = 4
N_EXPERTS = N_GROUPS * EXP_PER_GROUP
D_EXPERT = 512
EPS = 1e-6

LANES = 128
SUBLANES = 8
HEAD_K = QK_NOPE + 2 * ROPE_DIM
OFF_XR = 0
OFF_GR = OFF_XR + D_RNN
OFF_QL = OFF_GR + D_RNN
OFF_KVL = OFF_QL + Q_LORA
OFF_KR = OFF_KVL + KV_LORA
OFF_GA = OFF_KR + 2 * ROPE_DIM
OFF_GB = OFF_GA + D_MODEL
IN_COLS2 = OFF_GB + D_MODEL
ROUTE_OFF = N_GROUPS
ROUTE_ROWS = 32
CONV_PAD = 8
NEG = -1e30
GELU_C0 = 0.7978845608028654
GELU_C1 = GELU_C0 * 0.044715
ATTN_BLOCK = 512
MOE_TILE = 1024
KEY_STRIDE = 2048
MOE_EXPERTS_PER_STEP = 1
MOE_CHUNK = 192
ATTN_HEADS_PER_STEP = 2
VMEM_LIMIT = 56 * 1024 * 1024


def _cparams(sem):
    return pltpu.CompilerParams(dimension_semantics=sem, vmem_limit_bytes=VMEM_LIMIT)


def _const_spec(shape):
    n = len(shape)
    return pl.BlockSpec(shape, lambda *_: (0,) * n, pipeline_mode=pl.Buffered(1))


def _rms(x, g):
    return x * lax.rsqrt(jnp.mean(x * x, axis=-1, keepdims=True) + EPS) * g


def _dot(a, b):
    return jnp.dot(a, b, preferred_element_type=F32)


def _sigmoid(x):
    return 0.5 * jnp.tanh(0.5 * x) + 0.5


def _gelu_tanh(x):
    hx = 0.5 * x
    return hx + hx * jnp.tanh(x * (GELU_C0 + GELU_C1 * (x * x)))


def _rope(v, c, s):
    return v * c + pltpu.roll(v, ROPE_DIM, axis=1) * s


def _ada_kernel(c_ref, w_ref, b_ref, o_ref):
    c = c_ref[...]
    s = c * jax.nn.sigmoid(c)
    o_ref[...] = jnp.dot(s, w_ref[...], preferred_element_type=F32,
                         precision=lax.Precision.HIGHEST) + b_ref[...]


def _ada(c, w, b):
    nb = c.shape[0]
    n = w.shape[1]
    bn = n // 6
    return pl.pallas_call(
        _ada_kernel,
        grid=(n // bn,),
        in_specs=[_const_spec((nb, D_MODEL)),
                  pl.BlockSpec((D_MODEL, bn), lambda j: (0, j)),
                  pl.BlockSpec((1, bn), lambda j: (0, j))],
        out_specs=pl.BlockSpec((nb, bn), lambda j: (0, j)),
        out_shape=jax.ShapeDtypeStruct((nb, n), F32),
        compiler_params=_cparams(("arbitrary",)),
        name="ada",
    )(c, w, b.reshape(1, n))


def _mixer_in_kernel(x_ref, sh_ref, sc_ref, gpre_ref, win_ref, cst_ref, lst_ref, wconv_ref, bconv_ref,
                     wgate_ref, br_ref, bi_ref, lam_ref, wrnn_ref, gq_ref, wqup_ref, gkv_ref,
                     rc_ref, rs_ref,
                     ma_ref, sgb_ref, q_ref, ckv_ref, kpe_ref, kpe128_ref, cout_ref, lout_ref,
                     xbuf, b_scr, hcar, *, T):
    t = pl.program_id(1)

    @pl.when(t == 0)
    def _():
        xbuf[0:CONV_PAD, :] = jnp.zeros((CONV_PAD, D_RNN), F32)
        xbuf[CONV_PAD - (CONV_W - 1):CONV_PAD, :] = cst_ref[0]
        hcar[...] = lst_ref[0]

    x = x_ref[0]
    h = _rms(x, gpre_ref[...]) * (1.0 + sc_ref[0]) + sh_ref[0]
    hb = h.astype(BF16)

    groups = T // SUBLANES
    xbuf[CONV_PAD:CONV_PAD + T, :] = _dot(hb, win_ref[:, OFF_XR:OFF_XR + D_RNN])
    tail = xbuf[T + CONV_PAD - (CONV_W - 1):T + CONV_PAD, :]
    xall = xbuf[...].reshape(groups + 1, SUBLANES, D_RNN)
    row_wide = lax.broadcasted_iota(jnp.int32, (groups, SUBLANES, D_RNN), 1)
    xc = bconv_ref[...] + xall[1:] * wconv_ref[CONV_W - 1:CONV_W, :]
    for shift in range(1, CONV_W):
        rot = pltpu.roll(xall, shift, axis=1)
        shifted = jnp.where(row_wide >= shift, rot[1:], rot[:-1])
        xc = xc + shifted * wconv_ref[CONV_W - 1 - shift:CONV_W - shift, :]
    xc = xc.reshape(T, D_RNN)
    cout_ref[0] = tail
    xbuf[CONV_PAD - (CONV_W - 1):CONV_PAD, :] = tail

    lam = lam_ref[...]
    softplus_neg_lam = jnp.maximum(-lam, 0.0) + jnp.log1p(jnp.exp(-jnp.abs(lam)))
    row_in_group = lax.broadcasted_iota(jnp.int32, (groups, SUBLANES, RNN_BLOCK), 1)
    keeps = [row_in_group >= d for d in (1, 2, 4)]
    for n in range(N_RNN_BLOCKS):
        blk = slice(n * RNN_BLOCK, (n + 1) * RNN_BLOCK)
        xcb = xc[:, blk]
        g = _dot(xcb.astype(BF16), wgate_ref[n])
        r = _sigmoid(g[:, :RNN_BLOCK] + br_ref[:, blk])
        i = _sigmoid(g[:, RNN_BLOCK:] + bi_ref[:, blk])
        log_a = -LRU_C * r * softplus_neg_lam[:, blk]
        a = jnp.exp(log_a)
        z = -jnp.tanh(log_a) * (a * a + 1.0)
        b = jnp.where(z > 0.0, z * lax.rsqrt(z), 0.0) * (i * xcb)
        a = a.reshape(groups, SUBLANES, RNN_BLOCK)
        b = b.reshape(groups, SUBLANES, RNN_BLOCK)
        for keep, d in zip(keeps, (1, 2, 4)):
            a_prev = jnp.where(keep, pltpu.roll(a, d, axis=1), 1.0)
            b_prev = jnp.where(keep, pltpu.roll(b, d, axis=1), 0.0)
            b = b + a * b_prev
            a = a * a_prev
        hprev = hcar[:, blk]
        for grp in range(groups):
            rows = slice(grp * SUBLANES, (grp + 1) * SUBLANES)
            hg = b[grp] + a[grp] * hprev
            b_scr[rows, blk] = hg
            hprev = hg[SUBLANES - 1:SUBLANES, :]
        hcar[:, blk] = hprev
    lout_ref[0] = hcar[...]

    gr = _dot(hb, win_ref[:, OFF_GR:OFF_GR + D_RNN])
    y_a = _dot((b_scr[...] * _gelu_tanh(gr)).astype(BF16), wrnn_ref[...])
    ga = _dot(hb, win_ref[:, OFF_GA:OFF_GA + D_MODEL])
    ma_ref[0] = (_sigmoid(ga) * y_a).astype(BF16)
    gb = _dot(hb, win_ref[:, OFF_GB:OFF_GB + D_MODEL])
    sgb_ref[0] = _sigmoid(gb).astype(BF16)

    rc = rc_ref[...]
    rs = rs_ref[...]
    ql = _dot(hb, win_ref[:, OFF_QL:OFF_QL + Q_LORA])
    q = _dot(_rms(ql, gq_ref[...]).astype(BF16), wqup_ref[...])
    for hd in range(N_HEADS):
        base = hd * HEAD_K
        q_ref[0, :, base:base + QK_NOPE] = (q[:, base:base + QK_NOPE] * Q_SCALE).astype(BF16)
        pe = _rope(q[:, base + QK_NOPE:base + HEAD_K], rc, rs)
        q_ref[0, :, base + QK_NOPE:base + HEAD_K] = (pe * Q_SCALE).astype(BF16)
    kvl = _dot(hb, win_ref[:, OFF_KVL:OFF_KVL + KV_LORA])
    ckv_ref[0] = _rms(kvl, gkv_ref[...])
    kp = _rope(_dot(hb, win_ref[:, OFF_KR:OFF_KR + 2 * ROPE_DIM]), rc, rs)
    kpe_ref[0] = kp[:, :ROPE_DIM]
    kpe128_ref[0] = kp.astype(BF16)


def _mixer_in(x, sh1, sc1, conv_state, lru_state, rope_c, rope_s, W, T):
    B, S, _ = x.shape
    nt = S // T
    kern = functools.partial(_mixer_in_kernel, T=T)
    bt = lambda b, t: (b, t, 0)
    bo = lambda b, t: (b, 0, 0)
    tt = lambda b, t: (t, 0)
    in_specs = [
        pl.BlockSpec((1, T, D_MODEL), bt),
        pl.BlockSpec((1, 1, D_MODEL), bo),
        pl.BlockSpec((1, 1, D_MODEL), bo),
        _const_spec((1, D_MODEL)),
        _const_spec((D_MODEL, IN_COLS2)),
        pl.BlockSpec((1, CONV_W - 1, D_RNN), bo),
        pl.BlockSpec((1, 1, D_RNN), bo),
        _const_spec((CONV_W, D_RNN)),
        _const_spec((1, D_RNN)),
        _const_spec((N_RNN_BLOCKS, RNN_BLOCK, 2 * RNN_BLOCK)),
        _const_spec((1, D_RNN)),
        _const_spec((1, D_RNN)),
        _const_spec((1, D_RNN)),
        _const_spec((D_RNN, D_MODEL)),
        _const_spec((1, Q_LORA)),
        _const_spec((Q_LORA, N_HEADS * HEAD_K)),
        _const_spec((1, KV_LORA)),
        pl.BlockSpec((T, LANES), tt),
        pl.BlockSpec((T, LANES), tt),
    ]
    out_specs = [
        pl.BlockSpec((1, T, D_MODEL), bt),
        pl.BlockSpec((1, T, D_MODEL), bt),
        pl.BlockSpec((1, T, N_HEADS * HEAD_K), bt),
        pl.BlockSpec((1, T, KV_LORA), bt),
        pl.BlockSpec((1, T, ROPE_DIM), bt),
        pl.BlockSpec((1, T, LANES), bt),
        pl.BlockSpec((1, CONV_W - 1, D_RNN), bo),
        pl.BlockSpec((1, 1, D_RNN), bo),
    ]
    out_shape = [
        jax.ShapeDtypeStruct((B, S, D_MODEL), BF16),
        jax.ShapeDtypeStruct((B, S, D_MODEL), BF16),
        jax.ShapeDtypeStruct((B, S, N_HEADS * HEAD_K), BF16),
        jax.ShapeDtypeStruct((B, S, KV_LORA), F32),
        jax.ShapeDtypeStruct((B, S, ROPE_DIM), F32),
        jax.ShapeDtypeStruct((B, S, LANES), BF16),
        jax.ShapeDtypeStruct((B, CONV_W - 1, D_RNN), F32),
        jax.ShapeDtypeStruct((B, 1, D_RNN), F32),
    ]
    scratch = [
        pltpu.VMEM((T + CONV_PAD, D_RNN), F32),
        pltpu.VMEM((T, D_RNN), F32),
        pltpu.VMEM((1, D_RNN), F32),
    ]
    return pl.pallas_call(
        kern, grid=(B, nt), in_specs=in_specs, out_specs=out_specs, out_shape=out_shape,
        scratch_shapes=scratch, compiler_params=_cparams(("arbitrary", "arbitrary")),
        name="mixer_in",
    )(x, sh1, sc1, W["g_pre1"], W["w_in2"], conv_state, lru_state, W["w_conv"], W["b_conv"],
      W["w_gates"], W["b_rgate"], W["b_igate"], W["lam"], W["w_rnn_out"], W["g_q"], W["w_qup"],
      W["g_kv"], rope_c, rope_s)


def _kv_up_kernel(ckv_ref, kpe_ref, wk_ref, wv_ref, k_ref, v_ref, *, v_transposed):
    c = ckv_ref[...].astype(BF16)
    kn = _dot(c, wk_ref[...])
    kpe = kpe_ref[...]
    for hd in range(N_HEADS):
        base = hd * HEAD_K
        k_ref[:, base:base + QK_NOPE] = kn[:, hd * QK_NOPE:(hd + 1) * QK_NOPE].astype(BF16)
        k_ref[:, base + QK_NOPE:base + HEAD_K] = kpe
    if v_transposed:
        vt = lax.dot_general(wv_ref[...], c, (((1,), (1,)), ((), ())), preferred_element_type=F32)
        for hd in range(N_HEADS):
            v_ref[0, hd] = vt[hd * V_HEAD:(hd + 1) * V_HEAD, :].astype(BF16)
    else:
        v_ref[...] = _dot(c, wv_ref[...]).astype(BF16)


def _kv_up(ckv, kpe128, wk, wv, T, v_transposed):
    R = ckv.shape[0]
    row = lambda i: (i, 0)
    if v_transposed:
        v_spec = pl.BlockSpec((1, N_HEADS, V_HEAD, T), lambda i: (i, 0, 0, 0))
        v_shape = jax.ShapeDtypeStruct((R // T, N_HEADS, V_HEAD, T), BF16)
    else:
        v_spec = pl.BlockSpec((T, N_HEADS * V_HEAD), row)
        v_shape = jax.ShapeDtypeStruct((R, N_HEADS * V_HEAD), BF16)
    return pl.pallas_call(
        functools.partial(_kv_up_kernel, v_transposed=v_transposed), grid=(R // T,),
        in_specs=[pl.BlockSpec((T, KV_LORA), row), pl.BlockSpec((T, LANES), row),
                  _const_spec(wk.shape), _const_spec(wv.shape)],
        out_specs=[pl.BlockSpec((T, N_HEADS * HEAD_K), row), v_spec],
        out_shape=[jax.ShapeDtypeStruct((R, N_HEADS * HEAD_K), BF16), v_shape],
        compiler_params=_cparams(("arbitrary",)),
        name="kv_up",
    )(ckv, kpe128, wk, wv)


def _attn_prompt_kernel(q_ref, k_ref, vt_ref, o_ref, qt_scr, sa, sb, xa, xb, m_scr, l_scr, acc_scr, *, QB, HP):
    qi = pl.program_id(2)
    s0, s1 = (sa, xa), (sb, xb)
    for hh in range(HP):
        qt_scr[hh] = q_ref[0, :, hh * HEAD_K:(hh + 1) * HEAD_K].T
    m_scr[...] = jnp.full(m_scr.shape, NEG, F32)
    l_scr[...] = jnp.zeros(l_scr.shape, F32)
    acc_scr[...] = jnp.zeros(acc_scr.shape, F32)

    def scores(j, dst):
        start = pl.multiple_of(j * QB, QB)
        for hh in range(HP):
            s = _dot(k_ref[0, pl.ds(start, QB), hh * HEAD_K:(hh + 1) * HEAD_K], qt_scr[hh])
            dst[0][hh] = s
            dst[1][hh] = jnp.max(s, axis=0, keepdims=True)

    def update(j, src, masked):
        for hh in range(HP):
            s = src[0][hh]
            if masked:
                ck = lax.broadcasted_iota(jnp.int32, (QB, QB), 0) // CHUNK
                cq = lax.broadcasted_iota(jnp.int32, (QB, QB), 1) // CHUNK
                s = jnp.where(ck <= cq, s, NEG)
                smax = jnp.max(s, axis=0, keepdims=True)
            else:
                smax = src[1][hh]
            m_old = m_scr[hh]
            m_new = jnp.maximum(m_old, smax)
            p = jnp.exp2(s - m_new)
            alpha = jnp.exp2(m_old - m_new)
            l_scr[hh] = alpha * l_scr[hh] + jnp.sum(p, axis=0, keepdims=True)
            acc_scr[hh] = alpha * acc_scr[hh] + _dot(vt_ref[j, hh], p.astype(BF16))
            m_scr[hh] = m_new

    scores(0, s0)

    def pair(jj, c):
        j = 2 * jj
        scores(j + 1, s1)
        update(j, s0, False)
        scores(j + 2, s0)
        update(j + 1, s1, False)
        return c

    lax.fori_loop(0, qi // 2, pair, 0)

    @pl.when(qi % 2 == 0)
    def _():
        update(qi, s0, True)

    @pl.when(qi % 2 == 1)
    def _():
        scores(qi, s1)
        update(qi - 1, s0, False)
        update(qi, s1, True)

    for hh in range(HP):
        o_ref[0, :, hh * V_HEAD:(hh + 1) * V_HEAD] = (acc_scr[hh] / l_scr[hh]).T.astype(BF16)


def _attn_prompt(q, k, vt, QB, HP):
    B, S, _ = q.shape
    nkb = S // QB
    kern = functools.partial(_attn_prompt_kernel, QB=QB, HP=HP)
    return pl.pallas_call(
        kern, grid=(B, N_HEADS // HP, S // QB),
        in_specs=[pl.BlockSpec((1, QB, HP * HEAD_K), lambda b, h, i: (b, i, h)),
                  pl.BlockSpec((1, S, HP * HEAD_K), lambda b, h, i: (b, 0, h)),
                  pl.BlockSpec((nkb, HP, V_HEAD, QB), lambda b, h, i: (b, h, 0, 0))],
        out_specs=pl.BlockSpec((1, QB, HP * V_HEAD), lambda b, h, i: (b, i, h)),
        out_shape=jax.ShapeDtypeStruct((B, S, N_HEADS * V_HEAD), BF16),
        scratch_shapes=[pltpu.VMEM((HP, HEAD_K, QB), BF16),
                        pltpu.VMEM((HP, QB, QB), F32), pltpu.VMEM((HP, QB, QB), F32),
                        pltpu.VMEM((HP, 1, QB), F32), pltpu.VMEM((HP, 1, QB), F32),
                        pltpu.VMEM((HP, 1, QB), F32),
                        pltpu.VMEM((HP, 1, QB), F32), pltpu.VMEM((HP, V_HEAD, QB), F32)],
        compiler_params=_cparams(("arbitrary", "arbitrary", "arbitrary")),
        name="attn_prompt",
    )(q, k, vt)


def _attn_sample_kernel(q_ref, k_ref, v_ref, o_ref):
    for hd in range(N_HEADS):
        q = q_ref[0, :, hd * HEAD_K:(hd + 1) * HEAD_K]
        k = k_ref[0, :, hd * HEAD_K:(hd + 1) * HEAD_K]
        s = lax.dot_general(q, k, (((1,), (1,)), ((), ())), preferred_element_type=F32)
        p = jnp.exp2(s - jnp.max(s, axis=1, keepdims=True))
        l = jnp.sum(p, axis=1, keepdims=True)
        o = _dot(p.astype(BF16), v_ref[0, :, hd * V_HEAD:(hd + 1) * V_HEAD])
        o_ref[0, :, hd * V_HEAD:(hd + 1) * V_HEAD] = (o / l).astype(BF16)


def _attn_sample(q, k, v):
    B, S, _ = q.shape
    LK = k.shape[1]
    b3 = lambda b: (b, 0, 0)
    return pl.pallas_call(
        _attn_sample_kernel, grid=(B,),
        in_specs=[pl.BlockSpec((1, S, N_HEADS * HEAD_K), b3),
                  pl.BlockSpec((1, LK, N_HEADS * HEAD_K), b3),
                  pl.BlockSpec((1, LK, N_HEADS * V_HEAD), b3)],
        out_specs=pl.BlockSpec((1, S, N_HEADS * V_HEAD), b3),
        out_shape=jax.ShapeDtypeStruct((B, S, N_HEADS * V_HEAD), BF16),
        compiler_params=_cparams(("arbitrary",)),
        name="attn_sample",
    )(q, k, v)


def _first_max(rows):
    best = rows[0]
    for v in rows[1:]:
        best = jnp.maximum(best, v)
    idx = jnp.full(best.shape, len(rows) - 1, jnp.int32)
    for j in range(len(rows) - 2, -1, -1):
        idx = jnp.where(rows[j] == best, j, idx)
    return best, idx


def _merge_kernel(o_ref, ma_ref, sgb_ref, x_ref, gt1_ref, sh2_ref, sc2_ref, gpost1_ref, gpre2_ref,
                  wao_ref, wout_ref, wr_ref, br_ref, x1_ref, h2_ref, route_ref, cnt_ref, *, T):
    nc = T // CHUNK
    y_b = _dot(o_ref[...], wao_ref[...])
    m = ma_ref[...].astype(F32) + sgb_ref[...].astype(F32) * y_b
    y = _dot(m.astype(BF16), wout_ref[...])
    yn = _rms(y, gpost1_ref[...]).reshape(nc, CHUNK, D_MODEL)
    x1 = x_ref[...].reshape(nc, CHUNK, D_MODEL) + gt1_ref[...] * yn
    x1_ref[...] = x1.reshape(T, D_MODEL)
    h2 = (_rms(x1, gpre2_ref[...]) * (1.0 + sc2_ref[...]) + sh2_ref[...]).reshape(T, D_MODEL)
    h2b = h2.astype(BF16)
    h2_ref[...] = h2b

    logits = lax.dot_general(wr_ref[...], h2b, (((1,), (1,)), ((), ())), preferred_element_type=F32)
    logits = logits + br_ref[...]
    row = lambda r: logits[r:r + 1, :]
    gmax, gidx = _first_max([row(g) for g in range(N_GROUPS)])
    gsum = jnp.exp(row(0) - gmax)
    for g in range(1, N_GROUPS):
        gsum = gsum + jnp.exp(row(g) - gmax)
    sel = []
    for j in range(EXP_PER_GROUP):
        v = row(ROUTE_OFF + (N_GROUPS - 1) * EXP_PER_GROUP + j)
        for g in range(N_GROUPS - 2, -1, -1):
            v = jnp.where(gidx == g, row(ROUTE_OFF + g * EXP_PER_GROUP + j), v)
        sel.append(v)
    m1, j1 = _first_max(sel)
    m2, j2 = _first_max([jnp.where(j1 == j, NEG, sel[j]) for j in range(EXP_PER_GROUP)])
    e1 = gidx * EXP_PER_GROUP + j1
    e2 = gidx * EXP_PER_GROUP + j2
    ex = jnp.exp(m2 - m1)
    w1 = 1.0 / (gsum * (1.0 + ex))
    w2 = w1 * ex
    erow = lax.broadcasted_iota(jnp.int32, (N_EXPERTS, T), 0)
    uses = jnp.where((erow == e1) | (erow == e2), 1.0, 0.0)
    later = lax.broadcasted_iota(jnp.int32, (T, T), 0) < lax.broadcasted_iota(jnp.int32, (T, T), 1)
    rank = _dot(uses.astype(BF16), jnp.where(later, 1.0, 0.0).astype(BF16))
    cnt_ref[0] = jnp.sum(uses, axis=1, keepdims=True).astype(jnp.int32)
    rank1 = jnp.sum(jnp.where(erow == e1, rank, 0.0), axis=0, keepdims=True)
    rank2 = jnp.sum(jnp.where(erow == e2, rank, 0.0), axis=0, keepdims=True)
    fields = ((e1 + ROUTE_OFF).astype(F32), (e2 + ROUTE_OFF).astype(F32), w1, w2, rank1, rank2)
    frow = lax.broadcasted_iota(jnp.int32, (SUBLANES, T), 0)
    out = jnp.zeros((SUBLANES, T), F32)
    for f, val in enumerate(fields):
        out = jnp.where(frow == f, val, out)
    route_ref[...] = out


def _merge(o, ma, sgb, x, gt1c, sh2c, sc2c, W, T):
    N = x.shape[0]
    nc = T // CHUNK
    row = lambda i: (i, 0)
    chunk = lambda i: (i, 0, 0)
    kern = functools.partial(_merge_kernel, T=T)
    return pl.pallas_call(
        kern, grid=(N // T,),
        in_specs=[pl.BlockSpec((T, D_MODEL), row), pl.BlockSpec((T, D_MODEL), row),
                  pl.BlockSpec((T, D_MODEL), row), pl.BlockSpec((T, D_MODEL), row),
                  pl.BlockSpec((nc, 1, D_MODEL), chunk), pl.BlockSpec((nc, 1, D_MODEL), chunk),
                  pl.BlockSpec((nc, 1, D_MODEL), chunk),
                  _const_spec((1, D_MODEL)), _const_spec((1, D_MODEL)),
                  _const_spec((D_MODEL, D_MODEL)), _const_spec((D_MODEL, D_MODEL)),
                  _const_spec((ROUTE_ROWS, D_MODEL)), _const_spec((ROUTE_ROWS, 1))],
        out_specs=[pl.BlockSpec((T, D_MODEL), row), pl.BlockSpec((T, D_MODEL), row),
                   pl.BlockSpec((SUBLANES, T), lambda i: (0, i)), pl.BlockSpec((1, N_EXPERTS, 1), chunk)],
        out_shape=[jax.ShapeDtypeStruct((N, D_MODEL), F32), jax.ShapeDtypeStruct((N, D_MODEL), BF16),
                   jax.ShapeDtypeStruct((SUBLANES, N), F32),
                   jax.ShapeDtypeStruct((N // T, N_EXPERTS, 1), jnp.int32)],
        compiler_params=_cparams(("arbitrary",)),
        name="merge",
    )(o, ma, sgb, x, gt1c, sh2c, sc2c, W["g_post1"], W["g_pre2"], W["w_attn_out"], W["w_out"],
      W["w_route"], W["b_route"])


def _moe_kernel(cnt_ref, h_ref, route_ref, x1_ref, gt2_ref, gpost2_ref, wg_ref, wu_ref, wd_ref, y_ref,
                key_l, key_s, comb_w, *, T, TR, CH, NE):
    i = pl.program_id(0)
    s = pl.program_id(1)
    nc = T // CHUNK
    nr = T // TR

    def keys(sel, rank, tok):
        expert = sel.astype(jnp.int32) - ROUTE_OFF
        rank = rank.astype(jnp.int32)
        for r in range(1, nr):
            before = jnp.zeros_like(rank)
            for x in range(N_EXPERTS):
                before = jnp.where(expert == x, cnt_ref[(i * nr + r - 1) * N_EXPERTS + x], before)
            rank = rank + jnp.where(tok >= r * TR, before, 0)
        return expert * KEY_STRIDE + rank

    @pl.when(s == 0)
    def _():
        y_ref[...] = jnp.zeros(y_ref.shape, F32)
        rt = route_ref[...]
        rc = jnp.concatenate([rt, jnp.zeros((LANES - SUBLANES, T), F32)], axis=0).T
        tok_l = lax.broadcasted_iota(jnp.int32, (1, T), 1)
        tok_s = lax.broadcasted_iota(jnp.int32, (T, 1), 0)
        key_l[0:1, :] = keys(rt[0:1, :], rt[4:5, :], tok_l)
        key_l[1:2, :] = keys(rt[1:2, :], rt[5:6, :], tok_l)
        key_s[0] = keys(rc[:, 0:1], rc[:, 4:5], tok_s)
        key_s[1] = keys(rc[:, 1:2], rc[:, 5:6], tok_s)
        key_s[2] = rc[:, 0:1].astype(jnp.int32) - ROUTE_OFF
        comb_w[0] = rc[:, 2:3]
        comb_w[1] = rc[:, 3:4]

    k1_l, k2_l = key_l[0:1, :], key_l[1:2, :]
    k1_s, k2_s = key_s[0], key_s[1]
    experts = [s * NE + x for x in range(NE)]
    totals = []
    for ex in experts:
        total = cnt_ref[(i * nr) * N_EXPERTS + ex]
        for r in range(1, nr):
            total = total + cnt_ref[(i * nr + r) * N_EXPERTS + ex]
        totals.append(total)
    combs = [jnp.where(key_s[2] == ex, comb_w[0], comb_w[1]) for ex in experts]

    def chunk(c, carry):
        acc = None
        for x, ex in enumerate(experts):
            base = ex * KEY_STRIDE + c * CH
            want_s = base + lax.broadcasted_iota(jnp.int32, (CH, 1), 0)
            pick = (k1_l == want_s) | (k2_l == want_s)
            xg = _dot(jnp.where(pick, 1.0, 0.0).astype(BF16), h_ref[...]).astype(BF16)
            g = _dot(xg, wg_ref[x])
            u = _dot(xg, wu_ref[x])
            hid = (g * jax.nn.sigmoid(g) * u).astype(BF16)
            out = _dot(hid, wd_ref[x]).astype(BF16)
            want_l = base + lax.broadcasted_iota(jnp.int32, (1, CH), 1)
            put = (k1_s == want_l) | (k2_s == want_l)
            part = combs[x] * _dot(jnp.where(put, 1.0, 0.0).astype(BF16), out)
            acc = part if acc is None else acc + part
        y_ref[...] += acc
        return carry

    most = totals[0]
    for total in totals[1:]:
        most = jnp.maximum(most, total)
    lax.fori_loop(0, (most + CH - 1) // CH, chunk, 0)

    @pl.when(s == N_EXPERTS // NE - 1)
    def _():
        on = _rms(y_ref[...], gpost2_ref[...]).reshape(nc, CHUNK, D_MODEL)
        y = x1_ref[...].reshape(nc, CHUNK, D_MODEL) + gt2_ref[...] * on
        y_ref[...] = y.reshape(T, D_MODEL)


def _moe(h2, route, cnt, x1, gt2c, W, T, TR, CH, NE):
    N = h2.shape[0]
    nc = T // CHUNK
    row = lambda i, e, c: (i, 0)
    wsel = lambda i, e, c: (e, 0, 0)
    kern = functools.partial(_moe_kernel, T=T, TR=TR, CH=CH, NE=NE)
    grid_spec = pltpu.PrefetchScalarGridSpec(
        num_scalar_prefetch=1, grid=(N // T, N_EXPERTS // NE),
        in_specs=[pl.BlockSpec((T, D_MODEL), row), pl.BlockSpec((SUBLANES, T), lambda i, e, c: (0, i)),
                  pl.BlockSpec((T, D_MODEL), row),
                  pl.BlockSpec((nc, 1, D_MODEL), lambda i, e, c: (i, 0, 0)),
                  pl.BlockSpec((1, D_MODEL), lambda i, e, c: (0, 0)),
                  pl.BlockSpec((NE, D_MODEL, D_EXPERT), wsel),
                  pl.BlockSpec((NE, D_MODEL, D_EXPERT), wsel),
                  pl.BlockSpec((NE, D_EXPERT, D_MODEL), wsel)],
        out_specs=pl.BlockSpec((T, D_MODEL), row),
        scratch_shapes=[pltpu.VMEM((8, T), jnp.int32), pltpu.VMEM((3, T, 1), jnp.int32),
                        pltpu.VMEM((2, T, 1), F32)])
    return pl.pallas_call(
        kern, grid_spec=grid_spec,
        out_shape=jax.ShapeDtypeStruct((N, D_MODEL), F32),
        compiler_params=_cparams(("arbitrary", "arbitrary")),
        name="moe",
    )(cnt, h2, route, x1, gt2c, W["g_post2"], W["w_exp_gate"], W["w_exp_up"], W["w_exp_down"])


def _rotate_half_cols(w):
    half = ROPE_DIM // 2
    return jnp.concatenate([-w[..., half:], w[..., :half]], axis=-1)


def _rope_tables(pos):
    inv = ROPE_THETA ** (-jnp.arange(0, ROPE_DIM, 2, dtype=F32) / ROPE_DIM)
    ang = pos.astype(F32)[:, None] * inv
    z = jnp.zeros((pos.shape[0], LANES - ROPE_DIM), F32)
    c, s = jnp.cos(ang), jnp.sin(ang)
    return jnp.concatenate([c, c, z], axis=1), jnp.concatenate([s, s, z], axis=1)


def _chunk_rows(v, seq):
    B, D = v.shape
    return jnp.broadcast_to(v[:, None, None, :], (B, seq // CHUNK, 1, D)).reshape(B * (seq // CHUNK), 1, D)


def _layer(x, ada, ckv_past, kpe_past, conv_state, lru_state, pos0, W, T_in, T_tok):
    B, S, _ = x.shape
    L = 0 if ckv_past is None else ckv_past.shape[1]
    sh1, sc1, gt1, sh2, sc2, gt2 = jnp.split(ada, 6, axis=-1)
    rope_c, rope_s = _rope_tables(pos0 + jnp.arange(S))
    ma, sgb, q, ckv, kpe, kpe128, conv_new, lru_new = _mixer_in(
        x, sh1[:, None, :], sc1[:, None, :], conv_state, lru_state[:, None, :], rope_c, rope_s, W, T_in)

    if L == 0:
        k, vt = _kv_up(ckv.reshape(B * S, KV_LORA), kpe128.reshape(B * S, LANES), W["w_k_up"], W["w_v_up_t"],
                       ATTN_BLOCK, True)
        o = _attn_prompt(q, k.reshape(B, S, -1), vt, ATTN_BLOCK, ATTN_HEADS_PER_STEP)
    else:
        ckv_all = jnp.concatenate([ckv_past, ckv], axis=1)
        kpe_past128 = jnp.pad(kpe_past, ((0, 0), (0, 0), (0, LANES - ROPE_DIM))).astype(BF16)
        kpe_all = jnp.concatenate([kpe_past128, kpe128], axis=1)
        LK = L + S
        k, v = _kv_up(ckv_all.reshape(B * LK, KV_LORA), kpe_all.reshape(B * LK, LANES), W["w_k_up"], W["w_v_up"],
                      256, False)
        o = _attn_sample(q, k.reshape(B, LK, -1), v.reshape(B, LK, -1))

    N = B * S
    x1, h2, route, cnt = _merge(o.reshape(N, D_MODEL), ma.reshape(N, D_MODEL), sgb.reshape(N, D_MODEL),
                                x.reshape(N, D_MODEL), _chunk_rows(gt1, S), _chunk_rows(sh2, S),
                                _chunk_rows(sc2, S), W, T_tok)
    cnt = cnt.reshape(-1)
    y = _moe(h2, route, cnt, x1, _chunk_rows(gt2, S), W, MOE_TILE, T_tok, MOE_CHUNK, MOE_EXPERTS_PER_STEP)
    return y.reshape(B, S, D_MODEL), ckv, kpe, conv_new, lru_new.reshape(B, D_RNN)


def kernel(x_prompt, x_sample, c_prompt, c_sample, cache_ckv, cache_kpe, state_conv, state_rglru, w_ada, b_ada, g_pre1, g_post1, g_pre2, g_post2, w_in, w_conv, b_conv, w_rgate, b_rgate, w_igate, b_igate, lru_lambda, w_rnn_out, g_q_lat, w_q_up, g_kv_lat, w_k_up, w_v_up, w_attn_out, w_out, w_group, b_group, w_erouter, b_erouter, w_exp_gate, w_exp_up, w_exp_down):
    assert w_in.shape[0] == 1, "single-layer trunk"
    B = x_prompt.shape[0]
    wi = w_in[0]
    sp = lambda a, b: wi[:, a:b]
    xr, gr = sp(0, D_RNN), sp(D_RNN, 2 * D_RNN)
    o = 2 * D_RNN
    ql, kvl, kr = sp(o, o + Q_LORA), sp(o + Q_LORA, o + Q_LORA + KV_LORA), \
        sp(o + Q_LORA + KV_LORA, o + Q_LORA + KV_LORA + ROPE_DIM)
    o = o + Q_LORA + KV_LORA + ROPE_DIM
    ga, gb = sp(o, o + D_MODEL), sp(o + D_MODEL, o + 2 * D_MODEL)
    wq = w_q_up[0].reshape(Q_LORA, N_HEADS, QK_NOPE + ROPE_DIM)
    wq_pe = wq[..., QK_NOPE:]
    row = lambda a: a[0].reshape(1, -1)
    W = {
        "g_pre1": row(g_pre1), "g_post1": row(g_post1), "g_pre2": row(g_pre2), "g_post2": row(g_post2),
        "w_in2": jnp.concatenate([xr, gr, ql, kvl, kr, _rotate_half_cols(kr), ga, gb], axis=1).astype(BF16),
        "w_conv": w_conv[0], "b_conv": row(b_conv),
        "w_gates": jnp.concatenate([w_rgate[0], w_igate[0]], axis=-1).astype(BF16),
        "b_rgate": row(b_rgate), "b_igate": row(b_igate), "lam": row(lru_lambda),
        "w_rnn_out": w_rnn_out[0].astype(BF16),
        "g_q": row(g_q_lat), "g_kv": row(g_kv_lat),
        "w_qup": jnp.concatenate([wq[..., :QK_NOPE], wq_pe, _rotate_half_cols(wq_pe)], axis=-1)
                 .reshape(Q_LORA, N_HEADS * HEAD_K).astype(BF16),
        "w_k_up": w_k_up[0].astype(BF16), "w_v_up": w_v_up[0].astype(BF16),
        "w_v_up_t": w_v_up[0].T.astype(BF16),
        "w_attn_out": w_attn_out[0].astype(BF16), "w_out": w_out[0].astype(BF16),
        "w_route": jnp.pad(jnp.concatenate([w_group[0], w_erouter[0]], axis=1).T,
                           ((0, ROUTE_ROWS - N_GROUPS - N_EXPERTS), (0, 0))).astype(BF16),
        "b_route": jnp.pad(jnp.concatenate([b_group[0], b_erouter[0]]), (0, ROUTE_ROWS - N_GROUPS - N_EXPERTS))
                   .reshape(ROUTE_ROWS, 1),
        "w_exp_gate": w_exp_gate[0].astype(BF16), "w_exp_up": w_exp_up[0].astype(BF16),
        "w_exp_down": w_exp_down[0].astype(BF16),
    }
    ada = _ada(jnp.concatenate([c_prompt, c_sample], axis=0), w_ada[0], b_ada[0])
    zeros_conv = jnp.zeros((B, CONV_W - 1, D_RNN), F32)
    zeros_lru = jnp.zeros((B, D_RNN), F32)
    yp, ckv_p, kpe_p, conv_p, lru_p = _layer(x_prompt, ada[:B], None, None, zeros_conv, zeros_lru, 0, W, 512, 512)
    ys, ckv_s, kpe_s, conv_s, lru_s = _layer(x_sample, ada[B:], cache_ckv[0], cache_kpe[0], state_conv[0],
                                             state_rglru[0], cache_ckv.shape[2], W, 64, 512)
    return (yp, ys, ckv_p[None], kpe_p[None], conv_p[None], lru_p[None],
            ckv_s[None], kpe_s[None], conv_s[None], lru_s[None])
```

```python
import functools

import jax
import jax.numpy as jnp
from jax import lax
from jax.experimental import pallas as pl
from jax.experimental.pallas import tpu as pltpu

F32 = jnp.float32
BF16 = jnp.bfloat16

D_MODEL = 1024
CHUNK = 64
D_RNN = 1024
N_RNN_BLOCKS = 8
RNN_BLOCK = D_RNN // N_RNN_BLOCKS
CONV_W = 4
LRU_C = 8.0
N_HEADS = 8
QK_NOPE = 128
ROPE_DIM = 64
V_HEAD = 128
Q_LORA = 384
KV_LORA = 256
ROPE_THETA = 10000.0
SM_SCALE = (QK_NOPE + ROPE_DIM) ** -0.5
LOG2E = 1.4426950408889634
Q_SCALE = SM_SCALE * LOG2E
N_GROUPS = 4
EXP_PER_GROUP = 4
N_EXPERTS = N_GROUPS * EXP_PER_GROUP
D_EXPERT = 512
EPS = 1e-6

LANES = 128
SUBLANES = 8
HEAD_K = QK_NOPE + 2 * ROPE_DIM
OFF_XR = 0
OFF_GR = OFF_XR + D_RNN
OFF_QL = OFF_GR + D_RNN
OFF_KVL = OFF_QL + Q_LORA
OFF_KR = OFF_KVL + KV_LORA
OFF_GA = OFF_KR + 2 * ROPE_DIM
OFF_GB = OFF_GA + D_MODEL
IN_COLS2 = OFF_GB + D_MODEL
ROUTE_OFF = N_GROUPS
ROUTE_ROWS = 32
CONV_PAD = 8
NEG = -1e30
GELU_C0 = 0.7978845608028654
GELU_C1 = GELU_C0 * 0.044715
ATTN_BLOCK = 512
MOE_TILE = 1024
KEY_STRIDE = 2048
MOE_EXPERTS_PER_STEP = 1
MOE_CHUNK = 192
ATTN_HEADS_PER_STEP = 2
VMEM_LIMIT = 56 * 1024 * 1024


def _cparams(sem):
    return pltpu.CompilerParams(dimension_semantics=sem, vmem_limit_bytes=VMEM_LIMIT)


def _const_spec(shape):
    n = len(shape)
    return pl.BlockSpec(shape, lambda *_: (0,) * n, pipeline_mode=pl.Buffered(1))


def _rms(x, g):
    return x * lax.rsqrt(jnp.mean(x * x, axis=-1, keepdims=True) + EPS) * g


def _dot(a, b):
    return jnp.dot(a, b, preferred_element_type=F32)


def _sigmoid(x):
    return 0.5 * jnp.tanh(0.5 * x) + 0.5


def _gelu_tanh(x):
    hx = 0.5 * x
    return hx + hx * jnp.tanh(x * (GELU_C0 + GELU_C1 * (x * x)))


def _rope(v, c, s):
    return v * c + pltpu.roll(v, ROPE_DIM, axis=1) * s


def _ada_kernel(c_ref, w_ref, b_ref, o_ref):
    c = c_ref[...]
    s = c * jax.nn.sigmoid(c)
    o_ref[...] = jnp.dot(s, w_ref[...], preferred_element_type=F32,
                         precision=lax.Precision.HIGHEST) + b_ref[...]


def _ada(c, w, b):
    nb = c.shape[0]
    n = w.shape[1]
    bn = n // 6
    return pl.pallas_call(
        _ada_kernel,
        grid=(n // bn,),
        in_specs=[_const_spec((nb, D_MODEL)),
                  pl.BlockSpec((D_MODEL, bn), lambda j: (0, j)),
                  pl.BlockSpec((1, bn), lambda j: (0, j))],
        out_specs=pl.BlockSpec((nb, bn), lambda j: (0, j)),
        out_shape=jax.ShapeDtypeStruct((nb, n), F32),
        compiler_params=_cparams(("arbitrary",)),
        name="ada",
    )(c, w, b.reshape(1, n))


def _mixer_in_kernel(x_ref, sh_ref, sc_ref, gpre_ref, win_ref, cst_ref, lst_ref, wconv_ref, bconv_ref,
                     wgate_ref, br_ref, bi_ref, lam_ref, wrnn_ref, gq_ref, wqup_ref, gkv_ref,
                     rc_ref, rs_ref, *rest, T, emit_kv):
    if emit_kv:
        wk_ref, wvt_ref = rest[:2]
        rest = rest[2:]
    ma_ref, sgb_ref, q_ref, ckv_ref, kpe_ref, kpe128_ref, cout_ref, lout_ref = rest[:8]
    rest = rest[8:]
    if emit_kv:
        k_ref, vt_ref = rest[:2]
        rest = rest[2:]
    xbuf, b_scr, hcar = rest
    t = pl.program_id(1)

    @pl.when(t == 0)
    def _():
        xbuf[0:CONV_PAD, :] = jnp.zeros((CONV_PAD, D_RNN), F32)
        xbuf[CONV_PAD - (CONV_W - 1):CONV_PAD, :] = cst_ref[0]
        hcar[...] = lst_ref[0]

    x = x_ref[0]
    h = _rms(x, gpre_ref[...]) * (1.0 + sc_ref[0]) + sh_ref[0]
    hb = h.astype(BF16)

    groups = T // SUBLANES
    xbuf[CONV_PAD:CONV_PAD + T, :] = _dot(hb, win_ref[:, OFF_XR:OFF_XR + D_RNN])
    tail = xbuf[T + CONV_PAD - (CONV_W - 1):T + CONV_PAD, :]
    xall = xbuf[...].reshape(groups + 1, SUBLANES, D_RNN)
    row_wide = lax.broadcasted_iota(jnp.int32, (groups, SUBLANES, D_RNN), 1)
    xc = bconv_ref[...] + xall[1:] * wconv_ref[CONV_W - 1:CONV_W, :]
    for shift in range(1, CONV_W):
        rot = pltpu.roll(xall, shift, axis=1)
        shifted = jnp.where(row_wide >= shift, rot[1:], rot[:-1])
        xc = xc + shifted * wconv_ref[CONV_W - 1 - shift:CONV_W - shift, :]
    xc = xc.reshape(T, D_RNN)
    cout_ref[0] = tail
    xbuf[CONV_PAD - (CONV_W - 1):CONV_PAD, :] = tail

    lam = lam_ref[...]
    softplus_neg_lam = jnp.maximum(-lam, 0.0) + jnp.log1p(jnp.exp(-jnp.abs(lam)))
    row_in_group = lax.broadcasted_iota(jnp.int32, (groups, SUBLANES, RNN_BLOCK), 1)
    keeps = [row_in_group >= d for d in (1, 2, 4)]
    for n in range(N_RNN_BLOCKS):
        blk = slice(n * RNN_BLOCK, (n + 1) * RNN_BLOCK)
        xcb = xc[:, blk]
        g = _dot(xcb.astype(BF16), wgate_ref[n])
        r = _sigmoid(g[:, :RNN_BLOCK] + br_ref[:, blk])
        i = _sigmoid(g[:, RNN_BLOCK:] + bi_ref[:, blk])
        log_a = -LRU_C * r * softplus_neg_lam[:, blk]
        a = jnp.exp(log_a)
        z = -jnp.tanh(log_a) * (a * a + 1.0)
        b = jnp.where(z > 0.0, z * lax.rsqrt(z), 0.0) * (i * xcb)
        a = a.reshape(groups, SUBLANES, RNN_BLOCK)
        b = b.reshape(groups, SUBLANES, RNN_BLOCK)
        for keep, d in zip(keeps, (1, 2, 4)):
            a_prev = jnp.where(keep, pltpu.roll(a, d, axis=1), 1.0)
            b_prev = jnp.where(keep, pltpu.roll(b, d, axis=1), 0.0)
            b = b + a * b_prev
            a = a * a_prev
        hprev = hcar[:, blk]
        for grp in range(groups):
            rows = slice(grp * SUBLANES, (grp + 1) * SUBLANES)
            hg = b[grp] + a[grp] * hprev
            b_scr[rows, blk] = hg
            hprev = hg[SUBLANES - 1:SUBLANES, :]
        hcar[:, blk] = hprev
    lout_ref[0] = hcar[...]

    gr = _dot(hb, win_ref[:, OFF_GR:OFF_GR + D_RNN])
    y_a = _dot((b_scr[...] * _gelu_tanh(gr)).astype(BF16), wrnn_ref[...])
    ga = _dot(hb, win_ref[:, OFF_GA:OFF_GA + D_MODEL])
    ma_ref[0] = (_sigmoid(ga) * y_a).astype(BF16)
    gb = _dot(hb, win_ref[:, OFF_GB:OFF_GB + D_MODEL])
    sgb_ref[0] = _sigmoid(gb).astype(BF16)

    rc = rc_ref[...]
    rs = rs_ref[...]
    ql = _dot(hb, win_ref[:, OFF_QL:OFF_QL + Q_LORA])
    q = _dot(_rms(ql, gq_ref[...]).astype(BF16), wqup_ref[...])
    for hd in range(N_HEADS):
        base = hd * HEAD_K
        q_ref[0, :, base:base + QK_NOPE] = (q[:, base:base + QK_NOPE] * Q_SCALE).astype(BF16)
        pe = _rope(q[:, base + QK_NOPE:base + HEAD_K], rc, rs)
        q_ref[0, :, base + QK_NOPE:base + HEAD_K] = (pe * Q_SCALE).astype(BF16)
    kvl = _dot(hb, win_ref[:, OFF_KVL:OFF_KVL + KV_LORA])
    ckv = _rms(kvl, gkv_ref[...])
    ckv_ref[0] = ckv
    kp = _rope(_dot(hb, win_ref[:, OFF_KR:OFF_KR + 2 * ROPE_DIM]), rc, rs)
    kpe_ref[0] = kp[:, :ROPE_DIM]
    kpb = kp.astype(BF16)
    kpe128_ref[0] = kpb
    if emit_kv:
        _emit_kv(ckv.astype(BF16), kpb, wk_ref, wvt_ref, k_ref.at[0], vt_ref, True)


def _mixer_in(x, sh1, sc1, conv_state, lru_state, rope_c, rope_s, W, T, emit_kv):
    B, S, _ = x.shape
    nt = S // T
    kern = functools.partial(_mixer_in_kernel, T=T, emit_kv=emit_kv)
    bt = lambda b, t: (b, t, 0)
    bo = lambda b, t: (b, 0, 0)
    tt = lambda b, t: (t, 0)
    in_specs = [
        pl.BlockSpec((1, T, D_MODEL), bt),
        pl.BlockSpec((1, 1, D_MODEL), bo),
        pl.BlockSpec((1, 1, D_MODEL), bo),
        _const_spec((1, D_MODEL)),
        _const_spec((D_MODEL, IN_COLS2)),
        pl.BlockSpec((1, CONV_W - 1, D_RNN), bo),
        pl.BlockSpec((1, 1, D_RNN), bo),
        _const_spec((CONV_W, D_RNN)),
        _const_spec((1, D_RNN)),
        _const_spec((N_RNN_BLOCKS, RNN_BLOCK, 2 * RNN_BLOCK)),
        _const_spec((1, D_RNN)),
        _const_spec((1, D_RNN)),
        _const_spec((1, D_RNN)),
        _const_spec((D_RNN, D_MODEL)),
        _const_spec((1, Q_LORA)),
        _const_spec((Q_LORA, N_HEADS * HEAD_K)),
        _const_spec((1, KV_LORA)),
        pl.BlockSpec((T, LANES), tt),
        pl.BlockSpec((T, LANES), tt),
    ]
    out_specs = [
        pl.BlockSpec((1, T, D_MODEL), bt),
        pl.BlockSpec((1, T, D_MODEL), bt),
        pl.BlockSpec((1, T, N_HEADS * HEAD_K), bt),
        pl.BlockSpec((1, T, KV_LORA), bt),
        pl.BlockSpec((1, T, ROPE_DIM), bt),
        pl.BlockSpec((1, T, LANES), bt),
        pl.BlockSpec((1, CONV_W - 1, D_RNN), bo),
        pl.BlockSpec((1, 1, D_RNN), bo),
    ]
    out_shape = [
        jax.ShapeDtypeStruct((B, S, D_MODEL), BF16),
        jax.ShapeDtypeStruct((B, S, D_MODEL), BF16),
        jax.ShapeDtypeStruct((B, S, N_HEADS * HEAD_K), BF16),
        jax.ShapeDtypeStruct((B, S, KV_LORA), F32),
        jax.ShapeDtypeStruct((B, S, ROPE_DIM), F32),
        jax.ShapeDtypeStruct((B, S, LANES), BF16),
        jax.ShapeDtypeStruct((B, CONV_W - 1, D_RNN), F32),
        jax.ShapeDtypeStruct((B, 1, D_RNN), F32),
    ]
    scratch = [
        pltpu.VMEM((T + CONV_PAD, D_RNN), F32),
        pltpu.VMEM((T, D_RNN), F32),
        pltpu.VMEM((1, D_RNN), F32),
    ]
    args = [x, sh1, sc1, W["g_pre1"], W["w_in2"], conv_state, lru_state, W["w_conv"], W["b_conv"],
            W["w_gates"], W["b_rgate"], W["b_igate"], W["lam"], W["w_rnn_out"], W["g_q"], W["w_qup"],
            W["g_kv"], rope_c, rope_s]
    if emit_kv:
        args += [W["w_k_up"], W["w_v_up_t"]]
        in_specs += [_const_spec(W["w_k_up"].shape), _const_spec(W["w_v_up_t"].shape)]
        out_specs += [pl.BlockSpec((1, T, N_HEADS * HEAD_K), bt),
                      pl.BlockSpec((1, N_HEADS, V_HEAD, T), lambda b, t: (b * nt + t, 0, 0, 0))]
        out_shape += [jax.ShapeDtypeStruct((B, S, N_HEADS * HEAD_K), BF16),
                      jax.ShapeDtypeStruct((B * nt, N_HEADS, V_HEAD, T), BF16)]
    return pl.pallas_call(
        kern, grid=(B, nt), in_specs=in_specs, out_specs=out_specs, out_shape=out_shape,
        scratch_shapes=scratch, compiler_params=_cparams(("arbitrary", "arbitrary")),
        name="mixer_in",
    )(*args)


def _emit_kv(c, kpe, wk_ref, wv_ref, k_ref, v_ref, v_transposed):
    kn = _dot(c, wk_ref[...])
    for hd in range(N_HEADS):
        base = hd * HEAD_K
        k_ref[:, base:base + QK_NOPE] = kn[:, hd * QK_NOPE:(hd + 1) * QK_NOPE].astype(BF16)
        k_ref[:, base + QK_NOPE:base + HEAD_K] = kpe
    if v_transposed:
        vt = lax.dot_general(wv_ref[...], c, (((1,), (1,)), ((), ())), preferred_element_type=F32)
        for hd in range(N_HEADS):
            v_ref[0, hd] = vt[hd * V_HEAD:(hd + 1) * V_HEAD, :].astype(BF16)
    else:
        v_ref[...] = _dot(c, wv_ref[...]).astype(BF16)


def _kv_up_kernel(ckv_ref, kpe_ref, wk_ref, wv_ref, k_ref, v_ref, *, v_transposed):
    _emit_kv(ckv_ref[...].astype(BF16), kpe_ref[...], wk_ref, wv_ref, k_ref, v_ref, v_transposed)


def _kv_up(ckv, kpe128, wk, wv, T, v_transposed):
    R = ckv.shape[0]
    row = lambda i: (i, 0)
    if v_transposed:
        v_spec = pl.BlockSpec((1, N_HEADS, V_HEAD, T), lambda i: (i, 0, 0, 0))
        v_shape = jax.ShapeDtypeStruct((R // T, N_HEADS, V_HEAD, T), BF16)
    else:
        v_spec = pl.BlockSpec((T, N_HEADS * V_HEAD), row)
        v_shape = jax.ShapeDtypeStruct((R, N_HEADS * V_HEAD), BF16)
    return pl.pallas_call(
        functools.partial(_kv_up_kernel, v_transposed=v_transposed), grid=(R // T,),
        in_specs=[pl.BlockSpec((T, KV_LORA), row), pl.BlockSpec((T, LANES), row),
                  _const_spec(wk.shape), _const_spec(wv.shape)],
        out_specs=[pl.BlockSpec((T, N_HEADS * HEAD_K), row), v_spec],
        out_shape=[jax.ShapeDtypeStruct((R, N_HEADS * HEAD_K), BF16), v_shape],
        compiler_params=_cparams(("arbitrary",)),
        name="kv_up",
    )(ckv, kpe128, wk, wv)


def _attn_prompt_kernel(q_ref, k_ref, vt_ref, o_ref, qt_scr, sa, sb, xa, xb, m_scr, l_scr, acc_scr, *, QB, HP):
    qi = pl.program_id(2)
    s0, s1 = (sa, xa), (sb, xb)
    for hh in range(HP):
        qt_scr[hh] = q_ref[0, :, hh * HEAD_K:(hh + 1) * HEAD_K].T
    m_scr[...] = jnp.full(m_scr.shape, NEG, F32)
    l_scr[...] = jnp.zeros(l_scr.shape, F32)
    acc_scr[...] = jnp.zeros(acc_scr.shape, F32)

    def scores(j, dst):
        start = pl.multiple_of(j * QB, QB)
        for hh in range(HP):
            s = _dot(k_ref[0, pl.ds(start, QB), hh * HEAD_K:(hh + 1) * HEAD_K], qt_scr[hh])
            dst[0][hh] = s
            dst[1][hh] = jnp.max(s, axis=0, keepdims=True)

    def update(j, src, masked):
        for hh in range(HP):
            s = src[0][hh]
            if masked:
                ck = lax.broadcasted_iota(jnp.int32, (QB, QB), 0) // CHUNK
                cq = lax.broadcasted_iota(jnp.int32, (QB, QB), 1) // CHUNK
                s = jnp.where(ck <= cq, s, NEG)
                smax = jnp.max(s, axis=0, keepdims=True)
            else:
                smax = src[1][hh]
            m_old = m_scr[hh]
            m_new = jnp.maximum(m_old, smax)
            p = jnp.exp2(s - m_new)
            alpha = jnp.exp2(m_old - m_new)
            l_scr[hh] = alpha * l_scr[hh] + jnp.sum(p, axis=0, keepdims=True)
            acc_scr[hh] = alpha * acc_scr[hh] + _dot(vt_ref[j, hh], p.astype(BF16))
            m_scr[hh] = m_new

    scores(0, s0)

    def pair(jj, c):
        j = 2 * jj
        scores(j + 1, s1)
        update(j, s0, False)
        scores(j + 2, s0)
        update(j + 1, s1, False)
        return c

    lax.fori_loop(0, qi // 2, pair, 0)

    @pl.when(qi % 2 == 0)
    def _():
        update(qi, s0, True)

    @pl.when(qi % 2 == 1)
    def _():
        scores(qi, s1)
        update(qi - 1, s0, False)
        update(qi, s1, True)

    for hh in range(HP):
        o_ref[0, :, hh * V_HEAD:(hh + 1) * V_HEAD] = (acc_scr[hh] / l_scr[hh]).T.astype(BF16)


def _attn_prompt(q, k, vt, QB, HP):
    B, S, _ = q.shape
    nkb = S // QB
    kern = functools.partial(_attn_prompt_kernel, QB=QB, HP=HP)
    return pl.pallas_call(
        kern, grid=(B, N_HEADS // HP, S // QB),
        in_specs=[pl.BlockSpec((1, QB, HP * HEAD_K), lambda b, h, i: (b, i, h)),
                  pl.BlockSpec((1, S, HP * HEAD_K), lambda b, h, i: (b, 0, h)),
                  pl.BlockSpec((nkb, HP, V_HEAD, QB), lambda b, h, i: (b, h, 0, 0))],
        out_specs=pl.BlockSpec((1, QB, HP * V_HEAD), lambda b, h, i: (b, i, h)),
        out_shape=jax.ShapeDtypeStruct((B, S, N_HEADS * V_HEAD), BF16),
        scratch_shapes=[pltpu.VMEM((HP, HEAD_K, QB), BF16),
                        pltpu.VMEM((HP, QB, QB), F32), pltpu.VMEM((HP, QB, QB), F32),
                        pltpu.VMEM((HP, 1, QB), F32), pltpu.VMEM((HP, 1, QB), F32),
                        pltpu.VMEM((HP, 1, QB), F32),
                        pltpu.VMEM((HP, 1, QB), F32), pltpu.VMEM((HP, V_HEAD, QB), F32)],
        compiler_params=_cparams(("arbitrary", "arbitrary", "arbitrary")),
        name="attn_prompt",
    )(q, k, vt)


def _attn_sample_kernel(q_ref, k_ref, v_ref, o_ref):
    for hd in range(N_HEADS):
        q = q_ref[0, :, hd * HEAD_K:(hd + 1) * HEAD_K]
        k = k_ref[0, :, hd * HEAD_K:(hd + 1) * HEAD_K]
        s = lax.dot_general(q, k, (((1,), (1,)), ((), ())), preferred_element_type=F32)
        p = jnp.exp2(s - jnp.max(s, axis=1, keepdims=True))
        l = jnp.sum(p, axis=1, keepdims=True)
        o = _dot(p.astype(BF16), v_ref[0, :, hd * V_HEAD:(hd + 1) * V_HEAD])
        o_ref[0, :, hd * V_HEAD:(hd + 1) * V_HEAD] = (o / l).astype(BF16)


def _attn_sample(q, k, v):
    B, S, _ = q.shape
    LK = k.shape[1]
    b3 = lambda b: (b, 0, 0)
    return pl.pallas_call(
        _attn_sample_kernel, grid=(B,),
        in_specs=[pl.BlockSpec((1, S, N_HEADS * HEAD_K), b3),
                  pl.BlockSpec((1, LK, N_HEADS * HEAD_K), b3),
                  pl.BlockSpec((1, LK, N_HEADS * V_HEAD), b3)],
        out_specs=pl.BlockSpec((1, S, N_HEADS * V_HEAD), b3),
        out_shape=jax.ShapeDtypeStruct((B, S, N_HEADS * V_HEAD), BF16),
        compiler_params=_cparams(("arbitrary",)),
        name="attn_sample",
    )(q, k, v)


def _first_max(rows):
    best = rows[0]
    for v in rows[1:]:
        best = jnp.maximum(best, v)
    idx = jnp.full(best.shape, len(rows) - 1, jnp.int32)
    for j in range(len(rows) - 2, -1, -1):
        idx = jnp.where(rows[j] == best, j, idx)
    return best, idx


def _merge_kernel(o_ref, ma_ref, sgb_ref, x_ref, gt1_ref, sh2_ref, sc2_ref, gpost1_ref, gpre2_ref,
                  wao_ref, wout_ref, wr_ref, br_ref, x1_ref, h2_ref, route_ref, cnt_ref, *, T):
    nc = T // CHUNK
    y_b = _dot(o_ref[...], wao_ref[...])
    m = ma_ref[...].astype(F32) + sgb_ref[...].astype(F32) * y_b
    y = _dot(m.astype(BF16), wout_ref[...])
    yn = _rms(y, gpost1_ref[...]).reshape(nc, CHUNK, D_MODEL)
    x1 = x_ref[...].reshape(nc, CHUNK, D_MODEL) + gt1_ref[...] * yn
    x1_ref[...] = x1.reshape(T, D_MODEL)
    h2 = (_rms(x1, gpre2_ref[...]) * (1.0 + sc2_ref[...]) + sh2_ref[...]).reshape(T, D_MODEL)
    h2b = h2.astype(BF16)
    h2_ref[...] = h2b

    logits = lax.dot_general(wr_ref[...], h2b, (((1,), (1,)), ((), ())), preferred_element_type=F32)
    logits = logits + br_ref[...]
    row = lambda r: logits[r:r + 1, :]
    gmax, gidx = _first_max([row(g) for g in range(N_GROUPS)])
    gsum = jnp.exp(row(0) - gmax)
    for g in range(1, N_GROUPS):
        gsum = gsum + jnp.exp(row(g) - gmax)
    sel = []
    for j in range(EXP_PER_GROUP):
        v = row(ROUTE_OFF + (N_GROUPS - 1) * EXP_PER_GROUP + j)
        for g in range(N_GROUPS - 2, -1, -1):
            v = jnp.where(gidx == g, row(ROUTE_OFF + g * EXP_PER_GROUP + j), v)
        sel.append(v)
    m1, j1 = _first_max(sel)
    m2, j2 = _first_max([jnp.where(j1 == j, NEG, sel[j]) for j in range(EXP_PER_GROUP)])
    e1 = gidx * EXP_PER_GROUP + j1
    e2 = gidx * EXP_PER_GROUP + j2
    ex = jnp.exp(m2 - m1)
    w1 = 1.0 / (gsum * (1.0 + ex))
    w2 = w1 * ex
    erow = lax.broadcasted_iota(jnp.int32, (N_EXPERTS, T), 0)
    uses = jnp.where((erow == e1) | (erow == e2), 1.0, 0.0)
    later = lax.broadcasted_iota(jnp.int32, (T, T), 0) < lax.broadcasted_iota(jnp.int32, (T, T), 1)
    rank = _dot(uses.astype(BF16), jnp.where(later, 1.0, 0.0).astype(BF16))
    cnt_ref[0] = jnp.sum(uses, axis=1, keepdims=True).astype(jnp.int32)
    rank1 = jnp.sum(jnp.where(erow == e1, rank, 0.0), axis=0, keepdims=True)
    rank2 = jnp.sum(jnp.where(erow == e2, rank, 0.0), axis=0, keepdims=True)
    fields = ((e1 + ROUTE_OFF).astype(F32), (e2 + ROUTE_OFF).astype(F32), w1, w2, rank1, rank2)
    frow = lax.broadcasted_iota(jnp.int32, (SUBLANES, T), 0)
    out = jnp.zeros((SUBLANES, T), F32)
    for f, val in enumerate(fields):
        out = jnp.where(frow == f, val, out)
    route_ref[...] = out


def _merge(o, ma, sgb, x, gt1c, sh2c, sc2c, W, T):
    N = x.shape[0]
    nc = T // CHUNK
    row = lambda i: (i, 0)
    chunk = lambda i: (i, 0, 0)
    kern = functools.partial(_merge_kernel, T=T)
    return pl.pallas_call(
        kern, grid=(N // T,),
        in_specs=[pl.BlockSpec((T, D_MODEL), row), pl.BlockSpec((T, D_MODEL), row),
                  pl.BlockSpec((T, D_MODEL), row), pl.BlockSpec((T, D_MODEL), row),
                  pl.BlockSpec((nc, 1, D_MODEL), chunk), pl.BlockSpec((nc, 1, D_MODEL), chunk),
                  pl.BlockSpec((nc, 1, D_MODEL), chunk),
                  _const_spec((1, D_MODEL)), _const_spec((1, D_MODEL)),
                  _const_spec((D_MODEL, D_MODEL)), _const_spec((D_MODEL, D_MODEL)),
                  _const_spec((ROUTE_ROWS, D_MODEL)), _const_spec((ROUTE_ROWS, 1))],
        out_specs=[pl.BlockSpec((T, D_MODEL), row), pl.BlockSpec((T, D_MODEL), row),
                   pl.BlockSpec((SUBLANES, T), lambda i: (0, i)), pl.BlockSpec((1, N_EXPERTS, 1), chunk)],
        out_shape=[jax.ShapeDtypeStruct((N, D_MODEL), F32), jax.ShapeDtypeStruct((N, D_MODEL), BF16),
                   jax.ShapeDtypeStruct((SUBLANES, N), F32),
                   jax.ShapeDtypeStruct((N // T, N_EXPERTS, 1), jnp.int32)],
        compiler_params=_cparams(("arbitrary",)),
        name="merge",
    )(o, ma, sgb, x, gt1c, sh2c, sc2c, W["g_post1"], W["g_pre2"], W["w_attn_out"], W["w_out"],
      W["w_route"], W["b_route"])


def _moe_kernel(cnt_ref, h_ref, route_ref, x1_ref, gt2_ref, gpost2_ref, wg_ref, wu_ref, wd_ref, y_ref,
                key_l, key_s, comb_w, *, T, TR, CH, NE):
    i = pl.program_id(0)
    s = pl.program_id(1)
    nc = T // CHUNK
    nr = T // TR

    def keys(sel, rank, tok):
        expert = sel.astype(jnp.int32) - ROUTE_OFF
        rank = rank.astype(jnp.int32)
        for r in range(1, nr):
            before = jnp.zeros_like(rank)
            for x in range(N_EXPERTS):
                before = jnp.where(expert == x, cnt_ref[(i * nr + r - 1) * N_EXPERTS + x], before)
            rank = rank + jnp.where(tok >= r * TR, before, 0)
        return expert * KEY_STRIDE + rank

    @pl.when(s == 0)
    def _():
        y_ref[...] = jnp.zeros(y_ref.shape, F32)
        rt = route_ref[...]
        tok_l = lax.broadcasted_iota(jnp.int32, (1, T), 1)
        k1 = keys(rt[0:1, :], rt[4:5, :], tok_l)
        k2 = keys(rt[1:2, :], rt[5:6, :], tok_l)
        key_l[0:1, :] = k1
        key_l[1:2, :] = k2
        fields = (k1.astype(F32), k2.astype(F32), rt[0:1, :] - ROUTE_OFF, rt[2:3, :], rt[3:4, :])
        frow = lax.broadcasted_iota(jnp.int32, (SUBLANES, T), 0)
        rows = jnp.zeros((SUBLANES, T), F32)
        for f, val in enumerate(fields):
            rows = jnp.where(frow == f, val, rows)
        cols = jnp.concatenate([rows, jnp.zeros((LANES - SUBLANES, T), F32)], axis=0).T
        for f in range(3):
            key_s[f] = cols[:, f:f + 1].astype(jnp.int32)
        comb_w[0] = cols[:, 3:4]
        comb_w[1] = cols[:, 4:5]

    k1_l, k2_l = key_l[0:1, :], key_l[1:2, :]
    k1_s, k2_s = key_s[0], key_s[1]
    experts = [s * NE + x for x in range(NE)]
    totals = []
    for ex in experts:
        total = cnt_ref[(i * nr) * N_EXPERTS + ex]
        for r in range(1, nr):
            total = total + cnt_ref[(i * nr + r) * N_EXPERTS + ex]
        totals.append(total)
    combs = [jnp.where(key_s[2] == ex, comb_w[0], comb_w[1]) for ex in experts]

    def chunk(c, carry):
        acc = None
        for x, ex in enumerate(experts):
            base = ex * KEY_STRIDE + c * CH
            want_s = base + lax.broadcasted_iota(jnp.int32, (CH, 1), 0)
            pick = (k1_l == want_s) | (k2_l == want_s)
            xg = _dot(jnp.where(pick, 1.0, 0.0).astype(BF16), h_ref[...]).astype(BF16)
            g = _dot(xg, wg_ref[x])
            u = _dot(xg, wu_ref[x])
            hid = (g * jax.nn.sigmoid(g) * u).astype(BF16)
            out = _dot(hid, wd_ref[x]).astype(BF16)
            want_l = base + lax.broadcasted_iota(jnp.int32, (1, CH), 1)
            put = (k1_s == want_l) | (k2_s == want_l)
            part = combs[x] * _dot(jnp.where(put, 1.0, 0.0).astype(BF16), out)
            acc = part if acc is None else acc + part
        y_ref[...] += acc
        return carry

    most = totals[0]
    for total in totals[1:]:
        most = jnp.maximum(most, total)
    lax.fori_loop(0, (most + CH - 1) // CH, chunk, 0)

    @pl.when(s == N_EXPERTS // NE - 1)
    def _():
        on = _rms(y_ref[...], gpost2_ref[...]).reshape(nc, CHUNK, D_MODEL)
        y = x1_ref[...].reshape(nc, CHUNK, D_MODEL) + gt2_ref[...] * on
        y_ref[...] = y.reshape(T, D_MODEL)


def _moe(h2, route, cnt, x1, gt2c, W, T, TR, CH, NE):
    N = h2.shape[0]
    nc = T // CHUNK
    row = lambda i, e, c: (i, 0)
    wsel = lambda i, e, c: (e, 0, 0)
    kern = functools.partial(_moe_kernel, T=T, TR=TR, CH=CH, NE=NE)
    grid_spec = pltpu.PrefetchScalarGridSpec(
        num_scalar_prefetch=1, grid=(N // T, N_EXPERTS // NE),
        in_specs=[pl.BlockSpec((T, D_MODEL), row), pl.BlockSpec((SUBLANES, T), lambda i, e, c: (0, i)),
                  pl.BlockSpec((T, D_MODEL), row),
                  pl.BlockSpec((nc, 1, D_MODEL), lambda i, e, c: (i, 0, 0)),
                  pl.BlockSpec((1, D_MODEL), lambda i, e, c: (0, 0)),
                  pl.BlockSpec((NE, D_MODEL, D_EXPERT), wsel),
                  pl.BlockSpec((NE, D_MODEL, D_EXPERT), wsel),
                  pl.BlockSpec((NE, D_EXPERT, D_MODEL), wsel)],
        out_specs=pl.BlockSpec((T, D_MODEL), row),
        scratch_shapes=[pltpu.VMEM((8, T), jnp.int32), pltpu.VMEM((3, T, 1), jnp.int32),
                        pltpu.VMEM((2, T, 1), F32)])
    return pl.pallas_call(
        kern, grid_spec=grid_spec,
        out_shape=jax.ShapeDtypeStruct((N, D_MODEL), F32),
        compiler_params=_cparams(("arbitrary", "arbitrary")),
        name="moe",
    )(cnt, h2, route, x1, gt2c, W["g_post2"], W["w_exp_gate"], W["w_exp_up"], W["w_exp_down"])


def _rotate_half_cols(w):
    half = ROPE_DIM // 2
    return jnp.concatenate([-w[..., half:], w[..., :half]], axis=-1)


def _rope_tables(pos):
    inv = ROPE_THETA ** (-jnp.arange(0, ROPE_DIM, 2, dtype=F32) / ROPE_DIM)
    ang = pos.astype(F32)[:, None] * inv
    z = jnp.zeros((pos.shape[0], LANES - ROPE_DIM), F32)
    c, s = jnp.cos(ang), jnp.sin(ang)
    return jnp.concatenate([c, c, z], axis=1), jnp.concatenate([s, s, z], axis=1)


def _chunk_rows(v, seq):
    B, D = v.shape
    return jnp.broadcast_to(v[:, None, None, :], (B, seq // CHUNK, 1, D)).reshape(B * (seq // CHUNK), 1, D)


def _layer(x, ada, ckv_past, kpe_past, conv_state, lru_state, pos0, W, T_in, T_tok):
    B, S, _ = x.shape
    L = 0 if ckv_past is None else ckv_past.shape[1]
    sh1, sc1, gt1, sh2, sc2, gt2 = jnp.split(ada, 6, axis=-1)
    rope_c, rope_s = _rope_tables(pos0 + jnp.arange(S))
    outs = _mixer_in(x, sh1[:, None, :], sc1[:, None, :], conv_state, lru_state[:, None, :], rope_c, rope_s,
                     W, T_in, L == 0)
    ma, sgb, q, ckv, kpe, kpe128, conv_new, lru_new = outs[:8]

    if L == 0:
        assert T_in == ATTN_BLOCK, "mixer tiles are the attention key blocks"
        k, vt = outs[8:]
        o = _attn_prompt(q, k, vt, ATTN_BLOCK, ATTN_HEADS_PER_STEP)
    else:
        ckv_all = jnp.concatenate([ckv_past, ckv], axis=1)
        kpe_past128 = jnp.pad(kpe_past, ((0, 0), (0, 0), (0, LANES - ROPE_DIM))).astype(BF16)
        kpe_all = jnp.concatenate([kpe_past128, kpe128], axis=1)
        LK = L + S
        k, v = _kv_up(ckv_all.reshape(B * LK, KV_LORA), kpe_all.reshape(B * LK, LANES), W["w_k_up"], W["w_v_up"],
                      256, False)
        o = _attn_sample(q, k.reshape(B, LK, -1), v.reshape(B, LK, -1))

    N = B * S
    x1, h2, route, cnt = _merge(o.reshape(N, D_MODEL), ma.reshape(N, D_MODEL), sgb.reshape(N, D_MODEL),
                                x.reshape(N, D_MODEL), _chunk_rows(gt1, S), _chunk_rows(sh2, S),
                                _chunk_rows(sc2, S), W, T_tok)
    cnt = cnt.reshape(-1)
    y = _moe(h2, route, cnt, x1, _chunk_rows(gt2, S), W, MOE_TILE, T_tok, MOE_CHUNK, MOE_EXPERTS_PER_STEP)
    return y.reshape(B, S, D_MODEL), ckv, kpe, conv_new, lru_new.reshape(B, D_RNN)


def kernel(x_prompt, x_sample, c_prompt, c_sample, cache_ckv, cache_kpe, state_conv, state_rglru, w_ada, b_ada, g_pre1, g_post1, g_pre2, g_post2, w_in, w_conv, b_conv, w_rgate, b_rgate, w_igate, b_igate, lru_lambda, w_rnn_out, g_q_lat, w_q_up, g_kv_lat, w_k_up, w_v_up, w_attn_out, w_out, w_group, b_group, w_erouter, b_erouter, w_exp_gate, w_exp_up, w_exp_down):
    assert w_in.shape[0] == 1, "single-layer trunk"
    B = x_prompt.shape[0]
    wi = w_in[0]
    sp = lambda a, b: wi[:, a:b]
    xr, gr = sp(0, D_RNN), sp(D_RNN, 2 * D_RNN)
    o = 2 * D_RNN
    ql, kvl, kr = sp(o, o + Q_LORA), sp(o + Q_LORA, o + Q_LORA + KV_LORA), \
        sp(o + Q_LORA + KV_LORA, o + Q_LORA + KV_LORA + ROPE_DIM)
    o = o + Q_LORA + KV_LORA + ROPE_DIM
    ga, gb = sp(o, o + D_MODEL), sp(o + D_MODEL, o + 2 * D_MODEL)
    wq = w_q_up[0].reshape(Q_LORA, N_HEADS, QK_NOPE + ROPE_DIM)
    wq_pe = wq[..., QK_NOPE:]
    row = lambda a: a[0].reshape(1, -1)
    W = {
        "g_pre1": row(g_pre1), "g_post1": row(g_post1), "g_pre2": row(g_pre2), "g_post2": row(g_post2),
        "w_in2": jnp.concatenate([xr, gr, ql, kvl, kr, _rotate_half_cols(kr), ga, gb], axis=1).astype(BF16),
        "w_conv": w_conv[0], "b_conv": row(b_conv),
        "w_gates": jnp.concatenate([w_rgate[0], w_igate[0]], axis=-1).astype(BF16),
        "b_rgate": row(b_rgate), "b_igate": row(b_igate), "lam": row(lru_lambda),
        "w_rnn_out": w_rnn_out[0].astype(BF16),
        "g_q": row(g_q_lat), "g_kv": row(g_kv_lat),
        "w_qup": jnp.concatenate([wq[..., :QK_NOPE], wq_pe, _rotate_half_cols(wq_pe)], axis=-1)
                 .reshape(Q_LORA, N_HEADS * HEAD_K).astype(BF16),
        "w_k_up": w_k_up[0].astype(BF16), "w_v_up": w_v_up[0].astype(BF16),
        "w_v_up_t": w_v_up[0].T.astype(BF16),
        "w_attn_out": w_attn_out[0].astype(BF16), "w_out": w_out[0].astype(BF16),
        "w_route": jnp.pad(jnp.concatenate([w_group[0], w_erouter[0]], axis=1).T,
                           ((0, ROUTE_ROWS - N_GROUPS - N_EXPERTS), (0, 0))).astype(BF16),
        "b_route": jnp.pad(jnp.concatenate([b_group[0], b_erouter[0]]), (0, ROUTE_ROWS - N_GROUPS - N_EXPERTS))
                   .reshape(ROUTE_ROWS, 1),
        "w_exp_gate": w_exp_gate[0].astype(BF16), "w_exp_up": w_exp_up[0].astype(BF16),
        "w_exp_down": w_exp_down[0].astype(BF16),
    }
    ada = _ada(jnp.concatenate([c_prompt, c_sample], axis=0), w_ada[0], b_ada[0])
    zeros_conv = jnp.zeros((B, CONV_W - 1, D_RNN), F32)
    zeros_lru = jnp.zeros((B, D_RNN), F32)
    yp, ckv_p, kpe_p, conv_p, lru_p = _layer(x_prompt, ada[:B], None, None, zeros_conv, zeros_lru, 0, W, 512, 512)
    ys, ckv_s, kpe_s, conv_s, lru_s = _layer(x_sample, ada[B:], cache_ckv[0], cache_kpe[0], state_conv[0],
                                             state_rglru[0], cache_ckv.shape[2], W, 64, 512)
    return (yp, ys, ckv_p[None], kpe_p[None], conv_p[None], lru_p[None],
            ckv_s[None], kpe_s[None], conv_s[None], lru_s[None])
```

```python
import functools

import jax
import jax.numpy as jnp
from jax import lax
from jax.experimental import pallas as pl
from jax.experimental.pallas import tpu as pltpu

F32 = jnp.float32
BF16 = jnp.bfloat16

D_MODEL = 1024
CHUNK = 64
D_RNN = 1024
N_RNN_BLOCKS = 8
RNN_BLOCK = D_RNN // N_RNN_BLOCKS
CONV_W = 4
LRU_C = 8.0
N_HEADS = 8
QK_NOPE = 128
ROPE_DIM = 64
V_HEAD = 128
Q_LORA = 384
KV_LORA = 256
ROPE_THETA = 10000.0
SM_SCALE = (QK_NOPE + ROPE_DIM) ** -0.5
LOG2E = 1.4426950408889634
Q_SCALE = SM_SCALE * LOG2E
N_GROUPS = 4
EXP_PER_GROUP = 4
N_EXPERTS = N_GROUPS * EXP_PER_GROUP
D_EXPERT = 512
EPS = 1e-6

LANES = 128
SUBLANES = 8
HEAD_K = QK_NOPE + 2 * ROPE_DIM
OFF_XR = 0
OFF_GR = OFF_XR + D_RNN
OFF_QL = OFF_GR + D_RNN
OFF_KVL = OFF_QL + Q_LORA
OFF_KR = OFF_KVL + KV_LORA
OFF_GA = OFF_KR + 2 * ROPE_DIM
OFF_GB = OFF_GA + D_MODEL
IN_COLS2 = OFF_GB + D_MODEL
ROUTE_OFF = N_GROUPS
ROUTE_ROWS = 32
CONV_PAD = 8
NEG = -1e30
GELU_C0 = 0.7978845608028654
GELU_C1 = GELU_C0 * 0.044715
ATTN_BLOCK = 512
MOE_TILE = 1024
KEY_STRIDE = 2048
MOE_SCATTER_GROUP = 4
MOE_CHUNK = 192
ATTN_HEADS_PER_STEP = 2
VMEM_LIMIT = 56 * 1024 * 1024


def _cparams(sem):
    return pltpu.CompilerParams(dimension_semantics=sem, vmem_limit_bytes=VMEM_LIMIT)


def _const_spec(shape):
    n = len(shape)
    return pl.BlockSpec(shape, lambda *_: (0,) * n, pipeline_mode=pl.Buffered(1))


def _rms(x, g):
    return x * lax.rsqrt(jnp.mean(x * x, axis=-1, keepdims=True) + EPS) * g


def _dot(a, b):
    return jnp.dot(a, b, preferred_element_type=F32)


def _sigmoid(x):
    return 0.5 * jnp.tanh(0.5 * x) + 0.5


def _gelu_tanh(x):
    hx = 0.5 * x
    return hx + hx * jnp.tanh(x * (GELU_C0 + GELU_C1 * (x * x)))


def _rope(v, c, s):
    return v * c + pltpu.roll(v, ROPE_DIM, axis=1) * s


def _ada_kernel(c_ref, w_ref, b_ref, o_ref):
    c = c_ref[...]
    s = c * jax.nn.sigmoid(c)
    o_ref[...] = jnp.dot(s, w_ref[...], preferred_element_type=F32,
                         precision=lax.Precision.HIGHEST) + b_ref[...]


def _ada(c, w, b):
    nb = c.shape[0]
    n = w.shape[1]
    bn = n // 6
    return pl.pallas_call(
        _ada_kernel,
        grid=(n // bn,),
        in_specs=[_const_spec((nb, D_MODEL)),
                  pl.BlockSpec((D_MODEL, bn), lambda j: (0, j)),
                  pl.BlockSpec((1, bn), lambda j: (0, j))],
        out_specs=pl.BlockSpec((nb, bn), lambda j: (0, j)),
        out_shape=jax.ShapeDtypeStruct((nb, n), F32),
        compiler_params=_cparams(("arbitrary",)),
        name="ada",
    )(c, w, b.reshape(1, n))


def _mixer_in_kernel(x_ref, sh_ref, sc_ref, gpre_ref, win_ref, cst_ref, lst_ref, wconv_ref, bconv_ref,
                     wgate_ref, br_ref, bi_ref, lam_ref, wrnn_ref, gq_ref, wqup_ref, gkv_ref,
                     rc_ref, rs_ref, *rest, T, emit_kv):
    if emit_kv:
        wk_ref, wvt_ref = rest[:2]
        rest = rest[2:]
    ma_ref, sgb_ref, q_ref, ckv_ref, kpe_ref, kpe128_ref, cout_ref, lout_ref = rest[:8]
    rest = rest[8:]
    if emit_kv:
        k_ref, vt_ref = rest[:2]
        rest = rest[2:]
    xbuf, b_scr, hcar = rest
    t = pl.program_id(1)

    @pl.when(t == 0)
    def _():
        xbuf[0:CONV_PAD, :] = jnp.zeros((CONV_PAD, D_RNN), F32)
        xbuf[CONV_PAD - (CONV_W - 1):CONV_PAD, :] = cst_ref[0]
        hcar[...] = lst_ref[0]

    x = x_ref[0]
    h = _rms(x, gpre_ref[...]) * (1.0 + sc_ref[0]) + sh_ref[0]
    hb = h.astype(BF16)

    groups = T // SUBLANES
    xbuf[CONV_PAD:CONV_PAD + T, :] = _dot(hb, win_ref[:, OFF_XR:OFF_XR + D_RNN])
    tail = xbuf[T + CONV_PAD - (CONV_W - 1):T + CONV_PAD, :]
    xall = xbuf[...].reshape(groups + 1, SUBLANES, D_RNN)
    row_wide = lax.broadcasted_iota(jnp.int32, (groups, SUBLANES, D_RNN), 1)
    xc = bconv_ref[...] + xall[1:] * wconv_ref[CONV_W - 1:CONV_W, :]
    for shift in range(1, CONV_W):
        rot = pltpu.roll(xall, shift, axis=1)
        shifted = jnp.where(row_wide >= shift, rot[1:], rot[:-1])
        xc = xc + shifted * wconv_ref[CONV_W - 1 - shift:CONV_W - shift, :]
    xc = xc.reshape(T, D_RNN)
    cout_ref[0] = tail
    xbuf[CONV_PAD - (CONV_W - 1):CONV_PAD, :] = tail

    lam = lam_ref[...]
    softplus_neg_lam = jnp.maximum(-lam, 0.0) + jnp.log1p(jnp.exp(-jnp.abs(lam)))
    row_in_group = lax.broadcasted_iota(jnp.int32, (groups, SUBLANES, RNN_BLOCK), 1)
    keeps = [row_in_group >= d for d in (1, 2, 4)]
    for n in range(N_RNN_BLOCKS):
        blk = slice(n * RNN_BLOCK, (n + 1) * RNN_BLOCK)
        xcb = xc[:, blk]
        g = _dot(xcb.astype(BF16), wgate_ref[n])
        r = _sigmoid(g[:, :RNN_BLOCK] + br_ref[:, blk])
        i = _sigmoid(g[:, RNN_BLOCK:] + bi_ref[:, blk])
        log_a = -LRU_C * r * softplus_neg_lam[:, blk]
        a = jnp.exp(log_a)
        z = -jnp.tanh(log_a) * (a * a + 1.0)
        b = jnp.where(z > 0.0, z * lax.rsqrt(z), 0.0) * (i * xcb)
        a = a.reshape(groups, SUBLANES, RNN_BLOCK)
        b = b.reshape(groups, SUBLANES, RNN_BLOCK)
        for keep, d in zip(keeps, (1, 2, 4)):
            a_prev = jnp.where(keep, pltpu.roll(a, d, axis=1), 1.0)
            b_prev = jnp.where(keep, pltpu.roll(b, d, axis=1), 0.0)
            b = b + a * b_prev
            a = a * a_prev
        hprev = hcar[:, blk]
        for grp in range(groups):
            rows = slice(grp * SUBLANES, (grp + 1) * SUBLANES)
            hg = b[grp] + a[grp] * hprev
            b_scr[rows, blk] = hg
            hprev = hg[SUBLANES - 1:SUBLANES, :]
        hcar[:, blk] = hprev
    lout_ref[0] = hcar[...]

    gr = _dot(hb, win_ref[:, OFF_GR:OFF_GR + D_RNN])
    y_a = _dot((b_scr[...] * _gelu_tanh(gr)).astype(BF16), wrnn_ref[...])
    ga = _dot(hb, win_ref[:, OFF_GA:OFF_GA + D_MODEL])
    ma_ref[0] = (_sigmoid(ga) * y_a).astype(BF16)
    gb = _dot(hb, win_ref[:, OFF_GB:OFF_GB + D_MODEL])
    sgb_ref[0] = _sigmoid(gb).astype(BF16)

    rc = rc_ref[...]
    rs = rs_ref[...]
    ql = _dot(hb, win_ref[:, OFF_QL:OFF_QL + Q_LORA])
    q = _dot(_rms(ql, gq_ref[...]).astype(BF16), wqup_ref[...])
    for hd in range(N_HEADS):
        base = hd * HEAD_K
        q_ref[0, :, base:base + QK_NOPE] = (q[:, base:base + QK_NOPE] * Q_SCALE).astype(BF16)
        pe = _rope(q[:, base + QK_NOPE:base + HEAD_K], rc, rs)
        q_ref[0, :, base + QK_NOPE:base + HEAD_K] = (pe * Q_SCALE).astype(BF16)
    kvl = _dot(hb, win_ref[:, OFF_KVL:OFF_KVL + KV_LORA])
    ckv = _rms(kvl, gkv_ref[...])
    ckv_ref[0] = ckv
    kp = _rope(_dot(hb, win_ref[:, OFF_KR:OFF_KR + 2 * ROPE_DIM]), rc, rs)
    kpe_ref[0] = kp[:, :ROPE_DIM]
    kpb = kp.astype(BF16)
    kpe128_ref[0] = kpb
    if emit_kv:
        _emit_kv(ckv.astype(BF16), kpb, wk_ref, wvt_ref, k_ref.at[0], vt_ref, True)


def _mixer_in(x, sh1, sc1, conv_state, lru_state, rope_c, rope_s, W, T, emit_kv):
    B, S, _ = x.shape
    nt = S // T
    kern = functools.partial(_mixer_in_kernel, T=T, emit_kv=emit_kv)
    bt = lambda b, t: (b, t, 0)
    bo = lambda b, t: (b, 0, 0)
    tt = lambda b, t: (t, 0)
    in_specs = [
        pl.BlockSpec((1, T, D_MODEL), bt),
        pl.BlockSpec((1, 1, D_MODEL), bo),
        pl.BlockSpec((1, 1, D_MODEL), bo),
        _const_spec((1, D_MODEL)),
        _const_spec((D_MODEL, IN_COLS2)),
        pl.BlockSpec((1, CONV_W - 1, D_RNN), bo),
        pl.BlockSpec((1, 1, D_RNN), bo),
        _const_spec((CONV_W, D_RNN)),
        _const_spec((1, D_RNN)),
        _const_spec((N_RNN_BLOCKS, RNN_BLOCK, 2 * RNN_BLOCK)),
        _const_spec((1, D_RNN)),
        _const_spec((1, D_RNN)),
        _const_spec((1, D_RNN)),
        _const_spec((D_RNN, D_MODEL)),
        _const_spec((1, Q_LORA)),
        _const_spec((Q_LORA, N_HEADS * HEAD_K)),
        _const_spec((1, KV_LORA)),
        pl.BlockSpec((T, LANES), tt),
        pl.BlockSpec((T, LANES), tt),
    ]
    out_specs = [
        pl.BlockSpec((1, T, D_MODEL), bt),
        pl.BlockSpec((1, T, D_MODEL), bt),
        pl.BlockSpec((1, T, N_HEADS * HEAD_K), bt),
        pl.BlockSpec((1, T, KV_LORA), bt),
        pl.BlockSpec((1, T, ROPE_DIM), bt),
        pl.BlockSpec((1, T, LANES), bt),
        pl.BlockSpec((1, CONV_W - 1, D_RNN), bo),
        pl.BlockSpec((1, 1, D_RNN), bo),
    ]
    out_shape = [
        jax.ShapeDtypeStruct((B, S, D_MODEL), BF16),
        jax.ShapeDtypeStruct((B, S, D_MODEL), BF16),
        jax.ShapeDtypeStruct((B, S, N_HEADS * HEAD_K), BF16),
        jax.ShapeDtypeStruct((B, S, KV_LORA), F32),
        jax.ShapeDtypeStruct((B, S, ROPE_DIM), F32),
        jax.ShapeDtypeStruct((B, S, LANES), BF16),
        jax.ShapeDtypeStruct((B, CONV_W - 1, D_RNN), F32),
        jax.ShapeDtypeStruct((B, 1, D_RNN), F32),
    ]
    scratch = [
        pltpu.VMEM((T + CONV_PAD, D_RNN), F32),
        pltpu.VMEM((T, D_RNN), F32),
        pltpu.VMEM((1, D_RNN), F32),
    ]
    args = [x, sh1, sc1, W["g_pre1"], W["w_in2"], conv_state, lru_state, W["w_conv"], W["b_conv"],
            W["w_gates"], W["b_rgate"], W["b_igate"], W["lam"], W["w_rnn_out"], W["g_q"], W["w_qup"],
            W["g_kv"], rope_c, rope_s]
    if emit_kv:
        args += [W["w_k_up"], W["w_v_up_t"]]
        in_specs += [_const_spec(W["w_k_up"].shape), _const_spec(W["w_v_up_t"].shape)]
        out_specs += [pl.BlockSpec((1, T, N_HEADS * HEAD_K), bt),
                      pl.BlockSpec((1, N_HEADS, V_HEAD, T), lambda b, t: (b * nt + t, 0, 0, 0))]
        out_shape += [jax.ShapeDtypeStruct((B, S, N_HEADS * HEAD_K), BF16),
                      jax.ShapeDtypeStruct((B * nt, N_HEADS, V_HEAD, T), BF16)]
    return pl.pallas_call(
        kern, grid=(B, nt), in_specs=in_specs, out_specs=out_specs, out_shape=out_shape,
        scratch_shapes=scratch, compiler_params=_cparams(("arbitrary", "arbitrary")),
        name="mixer_in",
    )(*args)


def _emit_kv(c, kpe, wk_ref, wv_ref, k_ref, v_ref, v_transposed):
    kn = _dot(c, wk_ref[...])
    for hd in range(N_HEADS):
        base = hd * HEAD_K
        k_ref[:, base:base + QK_NOPE] = kn[:, hd * QK_NOPE:(hd + 1) * QK_NOPE].astype(BF16)
        k_ref[:, base + QK_NOPE:base + HEAD_K] = kpe
    if v_transposed:
        vt = lax.dot_general(wv_ref[...], c, (((1,), (1,)), ((), ())), preferred_element_type=F32)
        for hd in range(N_HEADS):
            v_ref[0, hd] = vt[hd * V_HEAD:(hd + 1) * V_HEAD, :].astype(BF16)
    else:
        v_ref[...] = _dot(c, wv_ref[...]).astype(BF16)


def _kv_up_kernel(ckv_ref, kpe_ref, wk_ref, wv_ref, k_ref, v_ref, *, v_transposed):
    _emit_kv(ckv_ref[...].astype(BF16), kpe_ref[...], wk_ref, wv_ref, k_ref, v_ref, v_transposed)


def _kv_up(ckv, kpe128, wk, wv, T, v_transposed):
    R = ckv.shape[0]
    row = lambda i: (i, 0)
    if v_transposed:
        v_spec = pl.BlockSpec((1, N_HEADS, V_HEAD, T), lambda i: (i, 0, 0, 0))
        v_shape = jax.ShapeDtypeStruct((R // T, N_HEADS, V_HEAD, T), BF16)
    else:
        v_spec = pl.BlockSpec((T, N_HEADS * V_HEAD), row)
        v_shape = jax.ShapeDtypeStruct((R, N_HEADS * V_HEAD), BF16)
    return pl.pallas_call(
        functools.partial(_kv_up_kernel, v_transposed=v_transposed), grid=(R // T,),
        in_specs=[pl.BlockSpec((T, KV_LORA), row), pl.BlockSpec((T, LANES), row),
                  _const_spec(wk.shape), _const_spec(wv.shape)],
        out_specs=[pl.BlockSpec((T, N_HEADS * HEAD_K), row), v_spec],
        out_shape=[jax.ShapeDtypeStruct((R, N_HEADS * HEAD_K), BF16), v_shape],
        compiler_params=_cparams(("arbitrary",)),
        name="kv_up",
    )(ckv, kpe128, wk, wv)


def _attn_prompt_kernel(q_ref, k_ref, vt_ref, o_ref, qt_scr, sa, sb, xa, xb, m_scr, l_scr, acc_scr, *, QB, HP):
    qi = pl.program_id(2)
    s0, s1 = (sa, xa), (sb, xb)
    for hh in range(HP):
        qt_scr[hh] = q_ref[0, :, hh * HEAD_K:(hh + 1) * HEAD_K].T
    m_scr[...] = jnp.full(m_scr.shape, NEG, F32)
    l_scr[...] = jnp.zeros(l_scr.shape, F32)
    acc_scr[...] = jnp.zeros(acc_scr.shape, F32)

    def scores(j, dst):
        start = pl.multiple_of(j * QB, QB)
        for hh in range(HP):
            s = _dot(k_ref[0, pl.ds(start, QB), hh * HEAD_K:(hh + 1) * HEAD_K], qt_scr[hh])
            dst[0][hh] = s
            dst[1][hh] = jnp.max(s, axis=0, keepdims=True)

    def update(j, src, masked):
        for hh in range(HP):
            s = src[0][hh]
            if masked:
                ck = lax.broadcasted_iota(jnp.int32, (QB, QB), 0) // CHUNK
                cq = lax.broadcasted_iota(jnp.int32, (QB, QB), 1) // CHUNK
                s = jnp.where(ck <= cq, s, NEG)
                smax = jnp.max(s, axis=0, keepdims=True)
            else:
                smax = src[1][hh]
            m_old = m_scr[hh]
            m_new = jnp.maximum(m_old, smax)
            p = jnp.exp2(s - m_new)
            alpha = jnp.exp2(m_old - m_new)
            l_scr[hh] = alpha * l_scr[hh] + jnp.sum(p, axis=0, keepdims=True)
            acc_scr[hh] = alpha * acc_scr[hh] + _dot(vt_ref[j, hh], p.astype(BF16))
            m_scr[hh] = m_new

    scores(0, s0)

    def pair(jj, c):
        j = 2 * jj
        scores(j + 1, s1)
        update(j, s0, False)
        scores(j + 2, s0)
        update(j + 1, s1, False)
        return c

    lax.fori_loop(0, qi // 2, pair, 0)

    @pl.when(qi % 2 == 0)
    def _():
        update(qi, s0, True)

    @pl.when(qi % 2 == 1)
    def _():
        scores(qi, s1)
        update(qi - 1, s0, False)
        update(qi, s1, True)

    for hh in range(HP):
        o_ref[0, :, hh * V_HEAD:(hh + 1) * V_HEAD] = (acc_scr[hh] / l_scr[hh]).T.astype(BF16)


def _attn_prompt(q, k, vt, QB, HP):
    B, S, _ = q.shape
    nkb = S // QB
    kern = functools.partial(_attn_prompt_kernel, QB=QB, HP=HP)
    return pl.pallas_call(
        kern, grid=(B, N_HEADS // HP, S // QB),
        in_specs=[pl.BlockSpec((1, QB, HP * HEAD_K), lambda b, h, i: (b, i, h)),
                  pl.BlockSpec((1, S, HP * HEAD_K), lambda b, h, i: (b, 0, h)),
                  pl.BlockSpec((nkb, HP, V_HEAD, QB), lambda b, h, i: (b, h, 0, 0))],
        out_specs=pl.BlockSpec((1, QB, HP * V_HEAD), lambda b, h, i: (b, i, h)),
        out_shape=jax.ShapeDtypeStruct((B, S, N_HEADS * V_HEAD), BF16),
        scratch_shapes=[pltpu.VMEM((HP, HEAD_K, QB), BF16),
                        pltpu.VMEM((HP, QB, QB), F32), pltpu.VMEM((HP, QB, QB), F32),
                        pltpu.VMEM((HP, 1, QB), F32), pltpu.VMEM((HP, 1, QB), F32),
                        pltpu.VMEM((HP, 1, QB), F32),
                        pltpu.VMEM((HP, 1, QB), F32), pltpu.VMEM((HP, V_HEAD, QB), F32)],
        compiler_params=_cparams(("arbitrary", "arbitrary", "arbitrary")),
        name="attn_prompt",
    )(q, k, vt)


def _attn_sample_kernel(q_ref, k_ref, v_ref, o_ref):
    for hd in range(N_HEADS):
        q = q_ref[0, :, hd * HEAD_K:(hd + 1) * HEAD_K]
        k = k_ref[0, :, hd * HEAD_K:(hd + 1) * HEAD_K]
        s = lax.dot_general(q, k, (((1,), (1,)), ((), ())), preferred_element_type=F32)
        p = jnp.exp2(s - jnp.max(s, axis=1, keepdims=True))
        l = jnp.sum(p, axis=1, keepdims=True)
        o = _dot(p.astype(BF16), v_ref[0, :, hd * V_HEAD:(hd + 1) * V_HEAD])
        o_ref[0, :, hd * V_HEAD:(hd + 1) * V_HEAD] = (o / l).astype(BF16)


def _attn_sample(q, k, v):
    B, S, _ = q.shape
    LK = k.shape[1]
    b3 = lambda b: (b, 0, 0)
    return pl.pallas_call(
        _attn_sample_kernel, grid=(B,),
        in_specs=[pl.BlockSpec((1, S, N_HEADS * HEAD_K), b3),
                  pl.BlockSpec((1, LK, N_HEADS * HEAD_K), b3),
                  pl.BlockSpec((1, LK, N_HEADS * V_HEAD), b3)],
        out_specs=pl.BlockSpec((1, S, N_HEADS * V_HEAD), b3),
        out_shape=jax.ShapeDtypeStruct((B, S, N_HEADS * V_HEAD), BF16),
        compiler_params=_cparams(("arbitrary",)),
        name="attn_sample",
    )(q, k, v)


def _first_max(rows):
    best = rows[0]
    for v in rows[1:]:
        best = jnp.maximum(best, v)
    idx = jnp.full(best.shape, len(rows) - 1, jnp.int32)
    for j in range(len(rows) - 2, -1, -1):
        idx = jnp.where(rows[j] == best, j, idx)
    return best, idx


def _merge_kernel(o_ref, ma_ref, sgb_ref, x_ref, gt1_ref, sh2_ref, sc2_ref, gpost1_ref, gpre2_ref,
                  wao_ref, wout_ref, wr_ref, br_ref, x1_ref, h2_ref, route_ref, cnt_ref, *, T):
    nc = T // CHUNK
    y_b = _dot(o_ref[...], wao_ref[...])
    m = ma_ref[...].astype(F32) + sgb_ref[...].astype(F32) * y_b
    y = _dot(m.astype(BF16), wout_ref[...])
    yn = _rms(y, gpost1_ref[...]).reshape(nc, CHUNK, D_MODEL)
    x1 = x_ref[...].reshape(nc, CHUNK, D_MODEL) + gt1_ref[...] * yn
    x1_ref[...] = x1.reshape(T, D_MODEL)
    h2 = (_rms(x1, gpre2_ref[...]) * (1.0 + sc2_ref[...]) + sh2_ref[...]).reshape(T, D_MODEL)
    h2b = h2.astype(BF16)
    h2_ref[...] = h2b

    logits = lax.dot_general(wr_ref[...], h2b, (((1,), (1,)), ((), ())), preferred_element_type=F32)
    logits = logits + br_ref[...]
    row = lambda r: logits[r:r + 1, :]
    gmax, gidx = _first_max([row(g) for g in range(N_GROUPS)])
    gsum = jnp.exp(row(0) - gmax)
    for g in range(1, N_GROUPS):
        gsum = gsum + jnp.exp(row(g) - gmax)
    sel = []
    for j in range(EXP_PER_GROUP):
        v = row(ROUTE_OFF + (N_GROUPS - 1) * EXP_PER_GROUP + j)
        for g in range(N_GROUPS - 2, -1, -1):
            v = jnp.where(gidx == g, row(ROUTE_OFF + g * EXP_PER_GROUP + j), v)
        sel.append(v)
    m1, j1 = _first_max(sel)
    m2, j2 = _first_max([jnp.where(j1 == j, NEG, sel[j]) for j in range(EXP_PER_GROUP)])
    e1 = gidx * EXP_PER_GROUP + j1
    e2 = gidx * EXP_PER_GROUP + j2
    ex = jnp.exp(m2 - m1)
    w1 = 1.0 / (gsum * (1.0 + ex))
    w2 = w1 * ex
    erow = lax.broadcasted_iota(jnp.int32, (N_EXPERTS, T), 0)
    uses = jnp.where((erow == e1) | (erow == e2), 1.0, 0.0)
    later = lax.broadcasted_iota(jnp.int32, (T, T), 0) < lax.broadcasted_iota(jnp.int32, (T, T), 1)
    rank = _dot(uses.astype(BF16), jnp.where(later, 1.0, 0.0).astype(BF16))
    cnt_ref[0] = jnp.sum(uses, axis=1, keepdims=True).astype(jnp.int32)
    rank1 = jnp.sum(jnp.where(erow == e1, rank, 0.0), axis=0, keepdims=True)
    rank2 = jnp.sum(jnp.where(erow == e2, rank, 0.0), axis=0, keepdims=True)
    fields = ((e1 + ROUTE_OFF).astype(F32), (e2 + ROUTE_OFF).astype(F32), w1, w2, rank1, rank2)
    frow = lax.broadcasted_iota(jnp.int32, (SUBLANES, T), 0)
    out = jnp.zeros((SUBLANES, T), F32)
    for f, val in enumerate(fields):
        out = jnp.where(frow == f, val, out)
    route_ref[...] = out


def _merge(o, ma, sgb, x, gt1c, sh2c, sc2c, W, T):
    N = x.shape[0]
    nc = T // CHUNK
    row = lambda i: (i, 0)
    chunk = lambda i: (i, 0, 0)
    kern = functools.partial(_merge_kernel, T=T)
    return pl.pallas_call(
        kern, grid=(N // T,),
        in_specs=[pl.BlockSpec((T, D_MODEL), row), pl.BlockSpec((T, D_MODEL), row),
                  pl.BlockSpec((T, D_MODEL), row), pl.BlockSpec((T, D_MODEL), row),
                  pl.BlockSpec((nc, 1, D_MODEL), chunk), pl.BlockSpec((nc, 1, D_MODEL), chunk),
                  pl.BlockSpec((nc, 1, D_MODEL), chunk),
                  _const_spec((1, D_MODEL)), _const_spec((1, D_MODEL)),
                  _const_spec((D_MODEL, D_MODEL)), _const_spec((D_MODEL, D_MODEL)),
                  _const_spec((ROUTE_ROWS, D_MODEL)), _const_spec((ROUTE_ROWS, 1))],
        out_specs=[pl.BlockSpec((T, D_MODEL), row), pl.BlockSpec((T, D_MODEL), row),
                   pl.BlockSpec((SUBLANES, T), lambda i: (0, i)), pl.BlockSpec((1, N_EXPERTS, 1), chunk)],
        out_shape=[jax.ShapeDtypeStruct((N, D_MODEL), F32), jax.ShapeDtypeStruct((N, D_MODEL), BF16),
                   jax.ShapeDtypeStruct((SUBLANES, N), F32),
                   jax.ShapeDtypeStruct((N // T, N_EXPERTS, 1), jnp.int32)],
        compiler_params=_cparams(("arbitrary",)),
        name="merge",
    )(o, ma, sgb, x, gt1c, sh2c, sc2c, W["g_post1"], W["g_pre2"], W["w_attn_out"], W["w_out"],
      W["w_route"], W["b_route"])


def _moe_kernel(cnt_ref, h_ref, route_ref, x1_ref, gt2_ref, gpost2_ref, wg_ref, wu_ref, wd_ref, y_ref,
                key_l, key_s, comb_w, parked, *, T, TR, CH, SG):
    i = pl.program_id(0)
    e = pl.program_id(1)
    nc = T // CHUNK
    nr = T // TR

    def keys(sel, rank, tok):
        expert = sel.astype(jnp.int32) - ROUTE_OFF
        rank = rank.astype(jnp.int32)
        for r in range(1, nr):
            before = jnp.zeros_like(rank)
            for x in range(N_EXPERTS):
                before = jnp.where(expert == x, cnt_ref[(i * nr + r - 1) * N_EXPERTS + x], before)
            rank = rank + jnp.where(tok >= r * TR, before, 0)
        return expert * KEY_STRIDE + rank

    @pl.when(e == 0)
    def _():
        y_ref[...] = jnp.zeros(y_ref.shape, F32)
        rt = route_ref[...]
        tok_l = lax.broadcasted_iota(jnp.int32, (1, T), 1)
        k1 = keys(rt[0:1, :], rt[4:5, :], tok_l)
        k2 = keys(rt[1:2, :], rt[5:6, :], tok_l)
        key_l[0:1, :] = k1
        key_l[1:2, :] = k2
        fields = (k1.astype(F32), k2.astype(F32), rt[0:1, :] - ROUTE_OFF, rt[2:3, :], rt[3:4, :])
        frow = lax.broadcasted_iota(jnp.int32, (SUBLANES, T), 0)
        rows = jnp.zeros((SUBLANES, T), F32)
        for f, val in enumerate(fields):
            rows = jnp.where(frow == f, val, rows)
        cols = jnp.concatenate([rows, jnp.zeros((LANES - SUBLANES, T), F32)], axis=0).T
        for f in range(3):
            key_s[f] = cols[:, f:f + 1].astype(jnp.int32)
        comb_w[0] = cols[:, 3:4]
        comb_w[1] = cols[:, 4:5]

    k1_l, k2_l = key_l[0:1, :], key_l[1:2, :]
    k1_s, k2_s = key_s[0], key_s[1]
    total = cnt_ref[(i * nr) * N_EXPERTS + e]
    for r in range(1, nr):
        total = total + cnt_ref[(i * nr + r) * N_EXPERTS + e]

    def expert_rows(c):
        want_s = e * KEY_STRIDE + c * CH + lax.broadcasted_iota(jnp.int32, (CH, 1), 0)
        pick = (k1_l == want_s) | (k2_l == want_s)
        xg = _dot(jnp.where(pick, 1.0, 0.0).astype(BF16), h_ref[...]).astype(BF16)
        g = _dot(xg, wg_ref[0])
        u = _dot(xg, wu_ref[0])
        hid = (g * jax.nn.sigmoid(g) * u).astype(BF16)
        return pick, _dot(hid, wd_ref[0])

    pick, out = expert_rows(0)
    rt = route_ref[...]
    comb_l = jnp.where(rt[0:1, :].astype(jnp.int32) - ROUTE_OFF == e, rt[2:3, :], rt[3:4, :])
    comb_c = jnp.sum(jnp.where(pick, comb_l, 0.0), axis=1, keepdims=True)
    slot = e % SG
    parked[pl.ds(pl.multiple_of(slot * CH, CH), CH), :] = (out * comb_c).astype(BF16)

    @pl.when(slot == SG - 1)
    def _():
        col = lax.broadcasted_iota(jnp.int32, (1, SG * CH), 1)
        which = jnp.zeros_like(col)
        for x in range(1, SG):
            which = which + jnp.where(col >= x * CH, 1, 0)
        want_l = (e - (SG - 1) + which) * KEY_STRIDE + col - which * CH
        put = (k1_s == want_l) | (k2_s == want_l)
        y_ref[...] += _dot(jnp.where(put, 1.0, 0.0).astype(BF16), parked[...])

    def overflow(c, carry):
        _, out = expert_rows(c)
        want_l = e * KEY_STRIDE + c * CH + lax.broadcasted_iota(jnp.int32, (1, CH), 1)
        put = (k1_s == want_l) | (k2_s == want_l)
        comb_s = jnp.where(key_s[2] == e, comb_w[0], comb_w[1])
        y_ref[...] += comb_s * _dot(jnp.where(put, 1.0, 0.0).astype(BF16), out.astype(BF16))
        return carry

    lax.fori_loop(1, (total + CH - 1) // CH, overflow, 0)

    @pl.when(e == N_EXPERTS - 1)
    def _():
        on = _rms(y_ref[...], gpost2_ref[...]).reshape(nc, CHUNK, D_MODEL)
        y = x1_ref[...].reshape(nc, CHUNK, D_MODEL) + gt2_ref[...] * on
        y_ref[...] = y.reshape(T, D_MODEL)


def _moe(h2, route, cnt, x1, gt2c, W, T, TR, CH, SG):
    N = h2.shape[0]
    nc = T // CHUNK
    row = lambda i, e, c: (i, 0)
    wsel = lambda i, e, c: (e, 0, 0)
    assert N_EXPERTS % SG == 0
    kern = functools.partial(_moe_kernel, T=T, TR=TR, CH=CH, SG=SG)
    grid_spec = pltpu.PrefetchScalarGridSpec(
        num_scalar_prefetch=1, grid=(N // T, N_EXPERTS),
        in_specs=[pl.BlockSpec((T, D_MODEL), row), pl.BlockSpec((SUBLANES, T), lambda i, e, c: (0, i)),
                  pl.BlockSpec((T, D_MODEL), row),
                  pl.BlockSpec((nc, 1, D_MODEL), lambda i, e, c: (i, 0, 0)),
                  pl.BlockSpec((1, D_MODEL), lambda i, e, c: (0, 0)),
                  pl.BlockSpec((1, D_MODEL, D_EXPERT), wsel),
                  pl.BlockSpec((1, D_MODEL, D_EXPERT), wsel),
                  pl.BlockSpec((1, D_EXPERT, D_MODEL), wsel)],
        out_specs=pl.BlockSpec((T, D_MODEL), row),
        scratch_shapes=[pltpu.VMEM((8, T), jnp.int32), pltpu.VMEM((3, T, 1), jnp.int32),
                        pltpu.VMEM((2, T, 1), F32), pltpu.VMEM((SG * CH, D_MODEL), BF16)])
    return pl.pallas_call(
        kern, grid_spec=grid_spec,
        out_shape=jax.ShapeDtypeStruct((N, D_MODEL), F32),
        compiler_params=_cparams(("arbitrary", "arbitrary")),
        name="moe",
    )(cnt, h2, route, x1, gt2c, W["g_post2"], W["w_exp_gate"], W["w_exp_up"], W["w_exp_down"])


def _rotate_half_cols(w):
    half = ROPE_DIM // 2
    return jnp.concatenate([-w[..., half:], w[..., :half]], axis=-1)


def _rope_tables(pos):
    inv = ROPE_THETA ** (-jnp.arange(0, ROPE_DIM, 2, dtype=F32) / ROPE_DIM)
    ang = pos.astype(F32)[:, None] * inv
    z = jnp.zeros((pos.shape[0], LANES - ROPE_DIM), F32)
    c, s = jnp.cos(ang), jnp.sin(ang)
    return jnp.concatenate([c, c, z], axis=1), jnp.concatenate([s, s, z], axis=1)


def _chunk_rows(v, seq):
    B, D = v.shape
    return jnp.broadcast_to(v[:, None, None, :], (B, seq // CHUNK, 1, D)).reshape(B * (seq // CHUNK), 1, D)


def _layer(x, ada, ckv_past, kpe_past, conv_state, lru_state, pos0, W, T_in, T_tok):
    B, S, _ = x.shape
    L = 0 if ckv_past is None else ckv_past.shape[1]
    sh1, sc1, gt1, sh2, sc2, gt2 = jnp.split(ada, 6, axis=-1)
    rope_c, rope_s = _rope_tables(pos0 + jnp.arange(S))
    outs = _mixer_in(x, sh1[:, None, :], sc1[:, None, :], conv_state, lru_state[:, None, :], rope_c, rope_s,
                     W, T_in, L == 0)
    ma, sgb, q, ckv, kpe, kpe128, conv_new, lru_new = outs[:8]

    if L == 0:
        assert T_in == ATTN_BLOCK, "mixer tiles are the attention key blocks"
        k, vt = outs[8:]
        o = _attn_prompt(q, k, vt, ATTN_BLOCK, ATTN_HEADS_PER_STEP)
    else:
        ckv_all = jnp.concatenate([ckv_past, ckv], axis=1)
        kpe_past128 = jnp.pad(kpe_past, ((0, 0), (0, 0), (0, LANES - ROPE_DIM))).astype(BF16)
        kpe_all = jnp.concatenate([kpe_past128, kpe128], axis=1)
        LK = L + S
        k, v = _kv_up(ckv_all.reshape(B * LK, KV_LORA), kpe_all.reshape(B * LK, LANES), W["w_k_up"], W["w_v_up"],
                      256, False)
        o = _attn_sample(q, k.reshape(B, LK, -1), v.reshape(B, LK, -1))

    N = B * S
    x1, h2, route, cnt = _merge(o.reshape(N, D_MODEL), ma.reshape(N, D_MODEL), sgb.reshape(N, D_MODEL),
                                x.reshape(N, D_MODEL), _chunk_rows(gt1, S), _chunk_rows(sh2, S),
                                _chunk_rows(sc2, S), W, T_tok)
    cnt = cnt.reshape(-1)
    y = _moe(h2, route, cnt, x1, _chunk_rows(gt2, S), W, MOE_TILE, T_tok, MOE_CHUNK, MOE_SCATTER_GROUP)
    return y.reshape(B, S, D_MODEL), ckv, kpe, conv_new, lru_new.reshape(B, D_RNN)


def kernel(x_prompt, x_sample, c_prompt, c_sample, cache_ckv, cache_kpe, state_conv, state_rglru, w_ada, b_ada, g_pre1, g_post1, g_pre2, g_post2, w_in, w_conv, b_conv, w_rgate, b_rgate, w_igate, b_igate, lru_lambda, w_rnn_out, g_q_lat, w_q_up, g_kv_lat, w_k_up, w_v_up, w_attn_out, w_out, w_group, b_group, w_erouter, b_erouter, w_exp_gate, w_exp_up, w_exp_down):
    assert w_in.shape[0] == 1, "single-layer trunk"
    B = x_prompt.shape[0]
    wi = w_in[0]
    sp = lambda a, b: wi[:, a:b]
    xr, gr = sp(0, D_RNN), sp(D_RNN, 2 * D_RNN)
    o = 2 * D_RNN
    ql, kvl, kr = sp(o, o + Q_LORA), sp(o + Q_LORA, o + Q_LORA + KV_LORA), \
        sp(o + Q_LORA + KV_LORA, o + Q_LORA + KV_LORA + ROPE_DIM)
    o = o + Q_LORA + KV_LORA + ROPE_DIM
    ga, gb = sp(o, o + D_MODEL), sp(o + D_MODEL, o + 2 * D_MODEL)
    wq = w_q_up[0].reshape(Q_LORA, N_HEADS, QK_NOPE + ROPE_DIM)
    wq_pe = wq[..., QK_NOPE:]
    row = lambda a: a[0].reshape(1, -1)
    W = {
        "g_pre1": row(g_pre1), "g_post1": row(g_post1), "g_pre2": row(g_pre2), "g_post2": row(g_post2),
        "w_in2": jnp.concatenate([xr, gr, ql, kvl, kr, _rotate_half_cols(kr), ga, gb], axis=1).astype(BF16),
        "w_conv": w_conv[0], "b_conv": row(b_conv),
        "w_gates": jnp.concatenate([w_rgate[0], w_igate[0]], axis=-1).astype(BF16),
        "b_rgate": row(b_rgate), "b_igate": row(b_igate), "lam": row(lru_lambda),
        "w_rnn_out": w_rnn_out[0].astype(BF16),
        "g_q": row(g_q_lat), "g_kv": row(g_kv_lat),
        "w_qup": jnp.concatenate([wq[..., :QK_NOPE], wq_pe, _rotate_half_cols(wq_pe)], axis=-1)
                 .reshape(Q_LORA, N_HEADS * HEAD_K).astype(BF16),
        "w_k_up": w_k_up[0].astype(BF16), "w_v_up": w_v_up[0].astype(BF16),
        "w_v_up_t": w_v_up[0].T.astype(BF16),
        "w_attn_out": w_attn_out[0].astype(BF16), "w_out": w_out[0].astype(BF16),
        "w_route": jnp.pad(jnp.concatenate([w_group[0], w_erouter[0]], axis=1).T,
                           ((0, ROUTE_ROWS - N_GROUPS - N_EXPERTS), (0, 0))).astype(BF16),
        "b_route": jnp.pad(jnp.concatenate([b_group[0], b_erouter[0]]), (0, ROUTE_ROWS - N_GROUPS - N_EXPERTS))
                   .reshape(ROUTE_ROWS, 1),
        "w_exp_gate": w_exp_gate[0].astype(BF16), "w_exp_up": w_exp_up[0].astype(BF16),
        "w_exp_down": w_exp_down[0].astype(BF16),
    }
    ada = _ada(jnp.concatenate([c_prompt, c_sample], axis=0), w_ada[0], b_ada[0])
    zeros_conv = jnp.zeros((B, CONV_W - 1, D_RNN), F32)
    zeros_lru = jnp.zeros((B, D_RNN), F32)
    yp, ckv_p, kpe_p, conv_p, lru_p = _layer(x_prompt, ada[:B], None, None, zeros_conv, zeros_lru, 0, W, 512, 512)
    ys, ckv_s, kpe_s, conv_s, lru_s = _layer(x_sample, ada[B:], cache_ckv[0], cache_kpe[0], state_conv[0],
                                             state_rglru[0], cache_ckv.shape[2], W, 64, 512)
    return (yp, ys, ckv_p[None], kpe_p[None], conv_p[None], lru_p[None],
            ckv_s[None], kpe_s[None], conv_s[None], lru_s[None])
```

```python
import functools

import jax
import jax.numpy as jnp
from jax import lax
from jax.experimental import pallas as pl
from jax.experimental.pallas import tpu as pltpu

F32 = jnp.float32
BF16 = jnp.bfloat16

D_MODEL = 1024
CHUNK = 64
D_RNN = 1024
N_RNN_BLOCKS = 8
RNN_BLOCK = D_RNN // N_RNN_BLOCKS
CONV_W = 4
LRU_C = 8.0
N_HEADS = 8
QK_NOPE = 128
ROPE_DIM = 64
V_HEAD = 128
Q_LORA = 384
KV_LORA = 256
ROPE_THETA = 10000.0
SM_SCALE = (QK_NOPE + ROPE_DIM) ** -0.5
LOG2E = 1.4426950408889634
Q_SCALE = SM_SCALE * LOG2E
N_GROUPS = 4
EXP_PER_GROUP = 4
N_EXPERTS = N_GROUPS * EXP_PER_GROUP
D_EXPERT = 512
EPS = 1e-6

LANES = 128
SUBLANES = 8
HEAD_K = QK_NOPE + 2 * ROPE_DIM
OFF_XR = 0
OFF_GR = OFF_XR + D_RNN
OFF_QL = OFF_GR + D_RNN
OFF_KVL = OFF_QL + Q_LORA
OFF_KR = OFF_KVL + KV_LORA
OFF_GA = OFF_KR + 2 * ROPE_DIM
OFF_GB = OFF_GA + D_MODEL
IN_COLS2 = OFF_GB + D_MODEL
ROUTE_OFF = N_GROUPS
ROUTE_ROWS = 32
CONV_PAD = 8
NEG = -1e30
GELU_C0 = 0.7978845608028654
GELU_C1 = GELU_C0 * 0.044715
ATTN_BLOCK = 512
MOE_TILE = 1024
KEY_STRIDE = 2048
MOE_SCATTER_GROUP = 4
MOE_CHUNK = 192
ATTN_HEADS_PER_STEP = 2
VMEM_LIMIT = 56 * 1024 * 1024


def _cparams(sem):
    return pltpu.CompilerParams(dimension_semantics=sem, vmem_limit_bytes=VMEM_LIMIT)


def _const_spec(shape):
    n = len(shape)
    return pl.BlockSpec(shape, lambda *_: (0,) * n, pipeline_mode=pl.Buffered(1))


def _rms(x, g):
    return x * lax.rsqrt(jnp.mean(x * x, axis=-1, keepdims=True) + EPS) * g


def _dot(a, b):
    return jnp.dot(a, b, preferred_element_type=F32)


def _sigmoid(x):
    return 0.5 * jnp.tanh(0.5 * x) + 0.5


def _gelu_tanh(x):
    hx = 0.5 * x
    return hx + hx * jnp.tanh(x * (GELU_C0 + GELU_C1 * (x * x)))


def _rope(v, c, s):
    return v * c + pltpu.roll(v, ROPE_DIM, axis=1) * s


def _ada_kernel(c_ref, w_ref, b_ref, o_ref):
    c = c_ref[...]
    s = c * jax.nn.sigmoid(c)
    o_ref[...] = jnp.dot(s, w_ref[...], preferred_element_type=F32,
                         precision=lax.Precision.HIGHEST) + b_ref[...]


def _ada(c, w, b):
    nb = c.shape[0]
    n = w.shape[1]
    bn = n // 6
    return pl.pallas_call(
        _ada_kernel,
        grid=(n // bn,),
        in_specs=[_const_spec((nb, D_MODEL)),
                  pl.BlockSpec((D_MODEL, bn), lambda j: (0, j)),
                  pl.BlockSpec((1, bn), lambda j: (0, j))],
        out_specs=pl.BlockSpec((nb, bn), lambda j: (0, j)),
        out_shape=jax.ShapeDtypeStruct((nb, n), F32),
        compiler_params=_cparams(("arbitrary",)),
        name="ada",
    )(c, w, b.reshape(1, n))


def _mixer_in_kernel(x_ref, sh_ref, sc_ref, gpre_ref, win_ref, cst_ref, lst_ref, wconv_ref, bconv_ref,
                     wgate_ref, br_ref, bi_ref, lam_ref, wrnn_ref, gq_ref, wqup_ref, gkv_ref,
                     rc_ref, rs_ref, *rest, T, emit_kv):
    if emit_kv:
        wk_ref, wvt_ref = rest[:2]
        rest = rest[2:]
    ma_ref, sgb_ref, q_ref, ckv_ref, kpe_ref, kpe128_ref, cout_ref, lout_ref = rest[:8]
    rest = rest[8:]
    if emit_kv:
        k_ref, vt_ref = rest[:2]
        rest = rest[2:]
    xbuf, b_scr, hcar = rest
    t = pl.program_id(1)

    @pl.when(t == 0)
    def _():
        xbuf[0:CONV_PAD, :] = jnp.zeros((CONV_PAD, D_RNN), F32)
        xbuf[CONV_PAD - (CONV_W - 1):CONV_PAD, :] = cst_ref[0]
        hcar[...] = lst_ref[0]

    x = x_ref[0]
    h = _rms(x, gpre_ref[...]) * (1.0 + sc_ref[0]) + sh_ref[0]
    hb = h.astype(BF16)

    groups = T // SUBLANES
    xbuf[CONV_PAD:CONV_PAD + T, :] = _dot(hb, win_ref[:, OFF_XR:OFF_XR + D_RNN])
    tail = xbuf[T + CONV_PAD - (CONV_W - 1):T + CONV_PAD, :]
    xall = xbuf[...].reshape(groups + 1, SUBLANES, D_RNN)
    row_wide = lax.broadcasted_iota(jnp.int32, (groups, SUBLANES, D_RNN), 1)
    xc = bconv_ref[...] + xall[1:] * wconv_ref[CONV_W - 1:CONV_W, :]
    for shift in range(1, CONV_W):
        rot = pltpu.roll(xall, shift, axis=1)
        shifted = jnp.where(row_wide >= shift, rot[1:], rot[:-1])
        xc = xc + shifted * wconv_ref[CONV_W - 1 - shift:CONV_W - shift, :]
    xc = xc.reshape(T, D_RNN)
    cout_ref[0] = tail
    xbuf[CONV_PAD - (CONV_W - 1):CONV_PAD, :] = tail

    lam = lam_ref[...]
    softplus_neg_lam = jnp.maximum(-lam, 0.0) + jnp.log1p(jnp.exp(-jnp.abs(lam)))
    row_in_group = lax.broadcasted_iota(jnp.int32, (groups, SUBLANES, RNN_BLOCK), 1)
    keeps = [row_in_group >= d for d in (1, 2, 4)]
    for n in range(N_RNN_BLOCKS):
        blk = slice(n * RNN_BLOCK, (n + 1) * RNN_BLOCK)
        xcb = xc[:, blk]
        g = _dot(xcb.astype(BF16), wgate_ref[n])
        r = _sigmoid(g[:, :RNN_BLOCK] + br_ref[:, blk])
        i = _sigmoid(g[:, RNN_BLOCK:] + bi_ref[:, blk])
        log_a = -LRU_C * r * softplus_neg_lam[:, blk]
        a = jnp.exp(log_a)
        z = -jnp.tanh(log_a) * (a * a + 1.0)
        b = jnp.where(z > 0.0, z * lax.rsqrt(z), 0.0) * (i * xcb)
        a = a.reshape(groups, SUBLANES, RNN_BLOCK)
        b = b.reshape(groups, SUBLANES, RNN_BLOCK)
        for keep, d in zip(keeps, (1, 2, 4)):
            a_prev = jnp.where(keep, pltpu.roll(a, d, axis=1), 1.0)
            b_prev = jnp.where(keep, pltpu.roll(b, d, axis=1), 0.0)
            b = b + a * b_prev
            a = a * a_prev
        hprev = hcar[:, blk]
        for grp in range(groups):
            rows = slice(grp * SUBLANES, (grp + 1) * SUBLANES)
            hg = b[grp] + a[grp] * hprev
            b_scr[rows, blk] = hg
            hprev = hg[SUBLANES - 1:SUBLANES, :]
        hcar[:, blk] = hprev
    lout_ref[0] = hcar[...]

    gr = _dot(hb, win_ref[:, OFF_GR:OFF_GR + D_RNN])
    y_a = _dot((b_scr[...] * _gelu_tanh(gr)).astype(BF16), wrnn_ref[...])
    ga = _dot(hb, win_ref[:, OFF_GA:OFF_GA + D_MODEL])
    ma_ref[0] = (_sigmoid(ga) * y_a).astype(BF16)
    gb = _dot(hb, win_ref[:, OFF_GB:OFF_GB + D_MODEL])
    sgb_ref[0] = _sigmoid(gb).astype(BF16)

    rc = rc_ref[...]
    rs = rs_ref[...]
    ql = _dot(hb, win_ref[:, OFF_QL:OFF_QL + Q_LORA])
    q = _dot(_rms(ql, gq_ref[...]).astype(BF16), wqup_ref[...])
    for hd in range(N_HEADS):
        base = hd * HEAD_K
        q_ref[0, :, base:base + QK_NOPE] = (q[:, base:base + QK_NOPE] * Q_SCALE).astype(BF16)
        pe = _rope(q[:, base + QK_NOPE:base + HEAD_K], rc, rs)
        q_ref[0, :, base + QK_NOPE:base + HEAD_K] = (pe * Q_SCALE).astype(BF16)
    kvl = _dot(hb, win_ref[:, OFF_KVL:OFF_KVL + KV_LORA])
    ckv = _rms(kvl, gkv_ref[...])
    ckv_ref[0] = ckv
    kp = _rope(_dot(hb, win_ref[:, OFF_KR:OFF_KR + 2 * ROPE_DIM]), rc, rs)
    kpe_ref[0] = kp[:, :ROPE_DIM]
    kpb = kp.astype(BF16)
    kpe128_ref[0] = kpb
    if emit_kv:
        _emit_kv(ckv.astype(BF16), kpb, wk_ref, wvt_ref, k_ref.at[0], vt_ref, True)


def _mixer_in(x, sh1, sc1, conv_state, lru_state, rope_c, rope_s, W, T, emit_kv):
    B, S, _ = x.shape
    nt = S // T
    kern = functools.partial(_mixer_in_kernel, T=T, emit_kv=emit_kv)
    bt = lambda b, t: (b, t, 0)
    bo = lambda b, t: (b, 0, 0)
    tt = lambda b, t: (t, 0)
    in_specs = [
        pl.BlockSpec((1, T, D_MODEL), bt),
        pl.BlockSpec((1, 1, D_MODEL), bo),
        pl.BlockSpec((1, 1, D_MODEL), bo),
        _const_spec((1, D_MODEL)),
        _const_spec((D_MODEL, IN_COLS2)),
        pl.BlockSpec((1, CONV_W - 1, D_RNN), bo),
        pl.BlockSpec((1, 1, D_RNN), bo),
        _const_spec((CONV_W, D_RNN)),
        _const_spec((1, D_RNN)),
        _const_spec((N_RNN_BLOCKS, RNN_BLOCK, 2 * RNN_BLOCK)),
        _const_spec((1, D_RNN)),
        _const_spec((1, D_RNN)),
        _const_spec((1, D_RNN)),
        _const_spec((D_RNN, D_MODEL)),
        _const_spec((1, Q_LORA)),
        _const_spec((Q_LORA, N_HEADS * HEAD_K)),
        _const_spec((1, KV_LORA)),
        pl.BlockSpec((T, LANES), tt),
        pl.BlockSpec((T, LANES), tt),
    ]
    out_specs = [
        pl.BlockSpec((1, T, D_MODEL), bt),
        pl.BlockSpec((1, T, D_MODEL), bt),
        pl.BlockSpec((1, T, N_HEADS * HEAD_K), bt),
        pl.BlockSpec((1, T, KV_LORA), bt),
        pl.BlockSpec((1, T, ROPE_DIM), bt),
        pl.BlockSpec((1, T, LANES), bt),
        pl.BlockSpec((1, CONV_W - 1, D_RNN), bo),
        pl.BlockSpec((1, 1, D_RNN), bo),
    ]
    out_shape = [
        jax.ShapeDtypeStruct((B, S, D_MODEL), BF16),
        jax.ShapeDtypeStruct((B, S, D_MODEL), BF16),
        jax.ShapeDtypeStruct((B, S, N_HEADS * HEAD_K), BF16),
        jax.ShapeDtypeStruct((B, S, KV_LORA), F32),
        jax.ShapeDtypeStruct((B, S, ROPE_DIM), F32),
        jax.ShapeDtypeStruct((B, S, LANES), BF16),
        jax.ShapeDtypeStruct((B, CONV_W - 1, D_RNN), F32),
        jax.ShapeDtypeStruct((B, 1, D_RNN), F32),
    ]
    scratch = [
        pltpu.VMEM((T + CONV_PAD, D_RNN), F32),
        pltpu.VMEM((T, D_RNN), F32),
        pltpu.VMEM((1, D_RNN), F32),
    ]
    args = [x, sh1, sc1, W["g_pre1"], W["w_in2"], conv_state, lru_state, W["w_conv"], W["b_conv"],
            W["w_gates"], W["b_rgate"], W["b_igate"], W["lam"], W["w_rnn_out"], W["g_q"], W["w_qup"],
            W["g_kv"], rope_c, rope_s]
    if emit_kv:
        args += [W["w_k_up"], W["w_v_up_t"]]
        in_specs += [_const_spec(W["w_k_up"].shape), _const_spec(W["w_v_up_t"].shape)]
        out_specs += [pl.BlockSpec((1, T, N_HEADS * HEAD_K), bt),
                      pl.BlockSpec((1, N_HEADS, V_HEAD, T), lambda b, t: (b * nt + t, 0, 0, 0))]
        out_shape += [jax.ShapeDtypeStruct((B, S, N_HEADS * HEAD_K), BF16),
                      jax.ShapeDtypeStruct((B * nt, N_HEADS, V_HEAD, T), BF16)]
    return pl.pallas_call(
        kern, grid=(B, nt), in_specs=in_specs, out_specs=out_specs, out_shape=out_shape,
        scratch_shapes=scratch, compiler_params=_cparams(("arbitrary", "arbitrary")),
        name="mixer_in",
    )(*args)


def _emit_kv(c, kpe, wk_ref, wv_ref, k_ref, v_ref, v_transposed):
    kn = _dot(c, wk_ref[...])
    for hd in range(N_HEADS):
        base = hd * HEAD_K
        k_ref[:, base:base + QK_NOPE] = kn[:, hd * QK_NOPE:(hd + 1) * QK_NOPE].astype(BF16)
        k_ref[:, base + QK_NOPE:base + HEAD_K] = kpe
    if v_transposed:
        vt = lax.dot_general(wv_ref[...], c, (((1,), (1,)), ((), ())), preferred_element_type=F32)
        for hd in range(N_HEADS):
            v_ref[0, hd] = vt[hd * V_HEAD:(hd + 1) * V_HEAD, :].astype(BF16)
    else:
        v_ref[...] = _dot(c, wv_ref[...]).astype(BF16)


def _kv_up_kernel(ckv_ref, kpe_ref, wk_ref, wv_ref, k_ref, v_ref, *, v_transposed):
    _emit_kv(ckv_ref[...].astype(BF16), kpe_ref[...], wk_ref, wv_ref, k_ref, v_ref, v_transposed)


def _kv_up(ckv, kpe128, wk, wv, T, v_transposed):
    R = ckv.shape[0]
    row = lambda i: (i, 0)
    if v_transposed:
        v_spec = pl.BlockSpec((1, N_HEADS, V_HEAD, T), lambda i: (i, 0, 0, 0))
        v_shape = jax.ShapeDtypeStruct((R // T, N_HEADS, V_HEAD, T), BF16)
    else:
        v_spec = pl.BlockSpec((T, N_HEADS * V_HEAD), row)
        v_shape = jax.ShapeDtypeStruct((R, N_HEADS * V_HEAD), BF16)
    return pl.pallas_call(
        functools.partial(_kv_up_kernel, v_transposed=v_transposed), grid=(R // T,),
        in_specs=[pl.BlockSpec((T, KV_LORA), row), pl.BlockSpec((T, LANES), row),
                  _const_spec(wk.shape), _const_spec(wv.shape)],
        out_specs=[pl.BlockSpec((T, N_HEADS * HEAD_K), row), v_spec],
        out_shape=[jax.ShapeDtypeStruct((R, N_HEADS * HEAD_K), BF16), v_shape],
        compiler_params=_cparams(("arbitrary",)),
        name="kv_up",
    )(ckv, kpe128, wk, wv)


def _attn_prompt_kernel(q_ref, k_ref, vt_ref, o_ref, qt_scr, sa, sb, xa, xb, m_scr, l_scr, acc_scr, *, QB, HP):
    qi = pl.program_id(2)
    s0, s1 = (sa, xa), (sb, xb)
    for hh in range(HP):
        qt_scr[hh] = q_ref[0, :, hh * HEAD_K:(hh + 1) * HEAD_K].T
    m_scr[...] = jnp.full(m_scr.shape, NEG, F32)
    l_scr[...] = jnp.zeros(l_scr.shape, F32)
    acc_scr[...] = jnp.zeros(acc_scr.shape, F32)

    def scores(j, dst):
        start = pl.multiple_of(j * QB, QB)
        for hh in range(HP):
            s = _dot(k_ref[0, pl.ds(start, QB), hh * HEAD_K:(hh + 1) * HEAD_K], qt_scr[hh])
            dst[0][hh] = s
            dst[1][hh] = jnp.max(s, axis=0, keepdims=True)

    def update(j, src, masked):
        for hh in range(HP):
            s = src[0][hh]
            if masked:
                ck = lax.broadcasted_iota(jnp.int32, (QB, QB), 0) // CHUNK
                cq = lax.broadcasted_iota(jnp.int32, (QB, QB), 1) // CHUNK
                s = jnp.where(ck <= cq, s, NEG)
                smax = jnp.max(s, axis=0, keepdims=True)
            else:
                smax = src[1][hh]
            m_old = m_scr[hh]
            m_new = jnp.maximum(m_old, smax)
            p = jnp.exp2(s - m_new)
            alpha = jnp.exp2(m_old - m_new)
            l_scr[hh] = alpha * l_scr[hh] + jnp.sum(p, axis=0, keepdims=True)
            acc_scr[hh] = alpha * acc_scr[hh] + _dot(vt_ref[j, hh], p.astype(BF16))
            m_scr[hh] = m_new

    scores(0, s0)

    def pair(jj, c):
        j = 2 * jj
        scores(j + 1, s1)
        update(j, s0, False)
        scores(j + 2, s0)
        update(j + 1, s1, False)
        return c

    lax.fori_loop(0, qi // 2, pair, 0)

    @pl.when(qi % 2 == 0)
    def _():
        update(qi, s0, True)

    @pl.when(qi % 2 == 1)
    def _():
        scores(qi, s1)
        update(qi - 1, s0, False)
        update(qi, s1, True)

    for hh in range(HP):
        o_ref[0, :, hh * V_HEAD:(hh + 1) * V_HEAD] = (acc_scr[hh] / l_scr[hh]).T.astype(BF16)


def _attn_prompt(q, k, vt, QB, HP):
    B, S, _ = q.shape
    nkb = S // QB
    kern = functools.partial(_attn_prompt_kernel, QB=QB, HP=HP)
    return pl.pallas_call(
        kern, grid=(B, N_HEADS // HP, S // QB),
        in_specs=[pl.BlockSpec((1, QB, HP * HEAD_K), lambda b, h, i: (b, i, h)),
                  pl.BlockSpec((1, S, HP * HEAD_K), lambda b, h, i: (b, 0, h)),
                  pl.BlockSpec((nkb, HP, V_HEAD, QB), lambda b, h, i: (b, h, 0, 0))],
        out_specs=pl.BlockSpec((1, QB, HP * V_HEAD), lambda b, h, i: (b, i, h)),
        out_shape=jax.ShapeDtypeStruct((B, S, N_HEADS * V_HEAD), BF16),
        scratch_shapes=[pltpu.VMEM((HP, HEAD_K, QB), BF16),
                        pltpu.VMEM((HP, QB, QB), F32), pltpu.VMEM((HP, QB, QB), F32),
                        pltpu.VMEM((HP, 1, QB), F32), pltpu.VMEM((HP, 1, QB), F32),
                        pltpu.VMEM((HP, 1, QB), F32),
                        pltpu.VMEM((HP, 1, QB), F32), pltpu.VMEM((HP, V_HEAD, QB), F32)],
        compiler_params=_cparams(("arbitrary", "arbitrary", "arbitrary")),
        name="attn_prompt",
    )(q, k, vt)


def _attn_sample_kernel(q_ref, k_ref, v_ref, o_ref):
    for hd in range(N_HEADS):
        q = q_ref[0, :, hd * HEAD_K:(hd + 1) * HEAD_K]
        k = k_ref[0, :, hd * HEAD_K:(hd + 1) * HEAD_K]
        s = lax.dot_general(q, k, (((1,), (1,)), ((), ())), preferred_element_type=F32)
        p = jnp.exp2(s - jnp.max(s, axis=1, keepdims=True))
        l = jnp.sum(p, axis=1, keepdims=True)
        o = _dot(p.astype(BF16), v_ref[0, :, hd * V_HEAD:(hd + 1) * V_HEAD])
        o_ref[0, :, hd * V_HEAD:(hd + 1) * V_HEAD] = (o / l).astype(BF16)


def _attn_sample(q, k, v):
    B, S, _ = q.shape
    LK = k.shape[1]
    b3 = lambda b: (b, 0, 0)
    return pl.pallas_call(
        _attn_sample_kernel, grid=(B,),
        in_specs=[pl.BlockSpec((1, S, N_HEADS * HEAD_K), b3),
                  pl.BlockSpec((1, LK, N_HEADS * HEAD_K), b3),
                  pl.BlockSpec((1, LK, N_HEADS * V_HEAD), b3)],
        out_specs=pl.BlockSpec((1, S, N_HEADS * V_HEAD), b3),
        out_shape=jax.ShapeDtypeStruct((B, S, N_HEADS * V_HEAD), BF16),
        compiler_params=_cparams(("arbitrary",)),
        name="attn_sample",
    )(q, k, v)


def _first_max(rows):
    best = rows[0]
    for v in rows[1:]:
        best = jnp.maximum(best, v)
    idx = jnp.full(best.shape, len(rows) - 1, jnp.int32)
    for j in range(len(rows) - 2, -1, -1):
        idx = jnp.where(rows[j] == best, j, idx)
    return best, idx


def _merge_kernel(o_ref, ma_ref, sgb_ref, x_ref, gt1_ref, sh2_ref, sc2_ref, gpost1_ref, gpre2_ref,
                  wao_ref, wout_ref, wr_ref, br_ref, x1_ref, h2_ref, route_ref, cnt_ref, *, T):
    nc = T // CHUNK
    y_b = _dot(o_ref[...], wao_ref[...])
    m = ma_ref[...].astype(F32) + sgb_ref[...].astype(F32) * y_b
    y = _dot(m.astype(BF16), wout_ref[...])
    yn = _rms(y, gpost1_ref[...]).reshape(nc, CHUNK, D_MODEL)
    x1 = x_ref[...].reshape(nc, CHUNK, D_MODEL) + gt1_ref[...] * yn
    x1_ref[...] = x1.reshape(T, D_MODEL)
    h2 = (_rms(x1, gpre2_ref[...]) * (1.0 + sc2_ref[...]) + sh2_ref[...]).reshape(T, D_MODEL)
    h2b = h2.astype(BF16)
    h2_ref[...] = h2b

    logits = lax.dot_general(wr_ref[...], h2b, (((1,), (1,)), ((), ())), preferred_element_type=F32)
    logits = logits + br_ref[...]
    row = lambda r: logits[r:r + 1, :]
    gmax, gidx = _first_max([row(g) for g in range(N_GROUPS)])
    gsum = jnp.exp(row(0) - gmax)
    for g in range(1, N_GROUPS):
        gsum = gsum + jnp.exp(row(g) - gmax)
    sel = []
    for j in range(EXP_PER_GROUP):
        v = row(ROUTE_OFF + (N_GROUPS - 1) * EXP_PER_GROUP + j)
        for g in range(N_GROUPS - 2, -1, -1):
            v = jnp.where(gidx == g, row(ROUTE_OFF + g * EXP_PER_GROUP + j), v)
        sel.append(v)
    m1, j1 = _first_max(sel)
    m2, j2 = _first_max([jnp.where(j1 == j, NEG, sel[j]) for j in range(EXP_PER_GROUP)])
    e1 = gidx * EXP_PER_GROUP + j1
    e2 = gidx * EXP_PER_GROUP + j2
    ex = jnp.exp(m2 - m1)
    w1 = 1.0 / (gsum * (1.0 + ex))
    w2 = w1 * ex
    erow = lax.broadcasted_iota(jnp.int32, (N_EXPERTS, T), 0)
    uses = jnp.where((erow == e1) | (erow == e2), 1.0, 0.0)
    later = lax.broadcasted_iota(jnp.int32, (T, T), 0) < lax.broadcasted_iota(jnp.int32, (T, T), 1)
    rank = _dot(uses.astype(BF16), jnp.where(later, 1.0, 0.0).astype(BF16))
    cnt_ref[0] = jnp.sum(uses, axis=1, keepdims=True).astype(jnp.int32)
    rank1 = jnp.sum(jnp.where(erow == e1, rank, 0.0), axis=0, keepdims=True)
    rank2 = jnp.sum(jnp.where(erow == e2, rank, 0.0), axis=0, keepdims=True)
    fields = ((e1 + ROUTE_OFF).astype(F32), (e2 + ROUTE_OFF).astype(F32), w1, w2, rank1, rank2)
    frow = lax.broadcasted_iota(jnp.int32, (SUBLANES, T), 0)
    out = jnp.zeros((SUBLANES, T), F32)
    for f, val in enumerate(fields):
        out = jnp.where(frow == f, val, out)
    route_ref[...] = out


def _merge(o, ma, sgb, x, gt1c, sh2c, sc2c, W, T):
    N = x.shape[0]
    nc = T // CHUNK
    row = lambda i: (i, 0)
    chunk = lambda i: (i, 0, 0)
    kern = functools.partial(_merge_kernel, T=T)
    return pl.pallas_call(
        kern, grid=(N // T,),
        in_specs=[pl.BlockSpec((T, D_MODEL), row), pl.BlockSpec((T, D_MODEL), row),
                  pl.BlockSpec((T, D_MODEL), row), pl.BlockSpec((T, D_MODEL), row),
                  pl.BlockSpec((nc, 1, D_MODEL), chunk), pl.BlockSpec((nc, 1, D_MODEL), chunk),
                  pl.BlockSpec((nc, 1, D_MODEL), chunk),
                  _const_spec((1, D_MODEL)), _const_spec((1, D_MODEL)),
                  _const_spec((D_MODEL, D_MODEL)), _const_spec((D_MODEL, D_MODEL)),
                  _const_spec((ROUTE_ROWS, D_MODEL)), _const_spec((ROUTE_ROWS, 1))],
        out_specs=[pl.BlockSpec((T, D_MODEL), row), pl.BlockSpec((T, D_MODEL), row),
                   pl.BlockSpec((SUBLANES, T), lambda i: (0, i)), pl.BlockSpec((1, N_EXPERTS, 1), chunk)],
        out_shape=[jax.ShapeDtypeStruct((N, D_MODEL), F32), jax.ShapeDtypeStruct((N, D_MODEL), BF16),
                   jax.ShapeDtypeStruct((SUBLANES, N), F32),
                   jax.ShapeDtypeStruct((N // T, N_EXPERTS, 1), jnp.int32)],
        compiler_params=_cparams(("arbitrary",)),
        name="merge",
    )(o, ma, sgb, x, gt1c, sh2c, sc2c, W["g_post1"], W["g_pre2"], W["w_attn_out"], W["w_out"],
      W["w_route"], W["b_route"])


def _moe_kernel(cnt_ref, h_ref, route_ref, x1_ref, gt2_ref, gpost2_ref, wg_ref, wu_ref, wd_ref, y_ref,
                key_l, key_s, comb_w, parked, *, T, TR, CH, SG):
    i = pl.program_id(0)
    e = pl.program_id(1)
    nc = T // CHUNK
    nr = T // TR

    def keys(sel, rank, tok):
        expert = sel.astype(jnp.int32) - ROUTE_OFF
        rank = rank.astype(jnp.int32)
        for r in range(1, nr):
            before = jnp.zeros_like(rank)
            for x in range(N_EXPERTS):
                before = jnp.where(expert == x, cnt_ref[(i * nr + r - 1) * N_EXPERTS + x], before)
            rank = rank + jnp.where(tok >= r * TR, before, 0)
        return expert * KEY_STRIDE + rank

    @pl.when(e == 0)
    def _():
        y_ref[...] = jnp.zeros(y_ref.shape, F32)
        rt = route_ref[...]
        tok_l = lax.broadcasted_iota(jnp.int32, (1, T), 1)
        k1 = keys(rt[0:1, :], rt[4:5, :], tok_l)
        k2 = keys(rt[1:2, :], rt[5:6, :], tok_l)
        key_l[0:1, :] = k1
        key_l[1:2, :] = k2
        fields = (k1.astype(F32), k2.astype(F32), rt[0:1, :] - ROUTE_OFF, rt[2:3, :], rt[3:4, :])
        frow = lax.broadcasted_iota(jnp.int32, (SUBLANES, T), 0)
        rows = jnp.zeros((SUBLANES, T), F32)
        for f, val in enumerate(fields):
            rows = jnp.where(frow == f, val, rows)
        cols = jnp.concatenate([rows, jnp.zeros((LANES - SUBLANES, T), F32)], axis=0).T
        for f in range(3):
            key_s[f] = cols[:, f:f + 1].astype(jnp.int32)
        comb_w[0] = cols[:, 3:4]
        comb_w[1] = cols[:, 4:5]

    k1_l, k2_l = key_l[0:1, :], key_l[1:2, :]
    k1_s, k2_s = key_s[0], key_s[1]
    total = cnt_ref[(i * nr) * N_EXPERTS + e]
    for r in range(1, nr):
        total = total + cnt_ref[(i * nr + r) * N_EXPERTS + e]

    def expert_rows(c):
        want_s = e * KEY_STRIDE + c * CH + lax.broadcasted_iota(jnp.int32, (CH, 1), 0)
        pick = (k1_l == want_s) | (k2_l == want_s)
        xg = _dot(jnp.where(pick, 1.0, 0.0).astype(BF16), h_ref[...]).astype(BF16)
        g = _dot(xg, wg_ref[0])
        u = _dot(xg, wu_ref[0])
        hid = (g * jax.nn.sigmoid(g) * u).astype(BF16)
        return pick, _dot(hid, wd_ref[0])

    pick, out = expert_rows(0)
    rt = route_ref[...]
    comb_l = jnp.where(rt[0:1, :].astype(jnp.int32) - ROUTE_OFF == e, rt[2:3, :], rt[3:4, :])
    comb_c = jnp.sum(jnp.where(pick, comb_l, 0.0), axis=1, keepdims=True)
    slot = e % SG
    parked[pl.ds(pl.multiple_of(slot * CH, CH), CH), :] = (out * comb_c).astype(BF16)

    @pl.when(slot == SG - 1)
    def _():
        col = lax.broadcasted_iota(jnp.int32, (1, SG * CH), 1)
        which = jnp.zeros_like(col)
        for x in range(1, SG):
            which = which + jnp.where(col >= x * CH, 1, 0)
        want_l = (e - (SG - 1) + which) * KEY_STRIDE + col - which * CH
        put = (k1_s == want_l) | (k2_s == want_l)
        y_ref[...] += _dot(jnp.where(put, 1.0, 0.0).astype(BF16), parked[...])

    def overflow(c, carry):
        _, out = expert_rows(c)
        want_l = e * KEY_STRIDE + c * CH + lax.broadcasted_iota(jnp.int32, (1, CH), 1)
        put = (k1_s == want_l) | (k2_s == want_l)
        comb_s = jnp.where(key_s[2] == e, comb_w[0], comb_w[1])
        y_ref[...] += comb_s * _dot(jnp.where(put, 1.0, 0.0).astype(BF16), out.astype(BF16))
        return carry

    lax.fori_loop(1, (total + CH - 1) // CH, overflow, 0)

    @pl.when(e == N_EXPERTS - 1)
    def _():
        on = _rms(y_ref[...], gpost2_ref[...]).reshape(nc, CHUNK, D_MODEL)
        y = x1_ref[...].reshape(nc, CHUNK, D_MODEL) + gt2_ref[...] * on
        y_ref[...] = y.reshape(T, D_MODEL)


def _moe(h2, route, cnt, x1, gt2c, W, T, TR, CH, SG):
    N = h2.shape[0]
    nc = T // CHUNK
    row = lambda i, e, c: (i, 0)
    wsel = lambda i, e, c: (e, 0, 0)
    assert N_EXPERTS % SG == 0
    kern = functools.partial(_moe_kernel, T=T, TR=TR, CH=CH, SG=SG)
    grid_spec = pltpu.PrefetchScalarGridSpec(
        num_scalar_prefetch=1, grid=(N // T, N_EXPERTS),
        in_specs=[pl.BlockSpec((T, D_MODEL), row), pl.BlockSpec((SUBLANES, T), lambda i, e, c: (0, i)),
                  pl.BlockSpec((T, D_MODEL), row),
                  pl.BlockSpec((nc, 1, D_MODEL), lambda i, e, c: (i, 0, 0)),
                  pl.BlockSpec((1, D_MODEL), lambda i, e, c: (0, 0)),
                  pl.BlockSpec((1, D_MODEL, D_EXPERT), wsel),
                  pl.BlockSpec((1, D_MODEL, D_EXPERT), wsel),
                  pl.BlockSpec((1, D_EXPERT, D_MODEL), wsel)],
        out_specs=pl.BlockSpec((T, D_MODEL), row),
        scratch_shapes=[pltpu.VMEM((8, T), jnp.int32), pltpu.VMEM((3, T, 1), jnp.int32),
                        pltpu.VMEM((2, T, 1), F32), pltpu.VMEM((SG * CH, D_MODEL), BF16)])
    return pl.pallas_call(
        kern, grid_spec=grid_spec,
        out_shape=jax.ShapeDtypeStruct((N, D_MODEL), F32),
        compiler_params=_cparams(("arbitrary", "arbitrary")),
        name="moe",
    )(cnt, h2, route, x1, gt2c, W["g_post2"], W["w_exp_gate"], W["w_exp_up"], W["w_exp_down"])


def _rotate_half_cols(w):
    half = ROPE_DIM // 2
    return jnp.concatenate([-w[..., half:], w[..., :half]], axis=-1)


def _rope_tables(pos):
    inv = ROPE_THETA ** (-jnp.arange(0, ROPE_DIM, 2, dtype=F32) / ROPE_DIM)
    ang = pos.astype(F32)[:, None] * inv
    z = jnp.zeros((pos.shape[0], LANES - ROPE_DIM), F32)
    c, s = jnp.cos(ang), jnp.sin(ang)
    return jnp.concatenate([c, c, z], axis=1), jnp.concatenate([s, s, z], axis=1)


def _chunk_rows(v, seq):
    B, D = v.shape
    return jnp.broadcast_to(v[:, None, None, :], (B, seq // CHUNK, 1, D)).reshape(B * (seq // CHUNK), 1, D)


def _layer(x, ada, ckv_past, kpe_past, conv_state, lru_state, pos0, W, T_in, T_tok):
    B, S, _ = x.shape
    L = 0 if ckv_past is None else ckv_past.shape[1]
    sh1, sc1, gt1, sh2, sc2, gt2 = jnp.split(ada, 6, axis=-1)
    rope_c, rope_s = _rope_tables(pos0 + jnp.arange(S))
    outs = _mixer_in(x, sh1[:, None, :], sc1[:, None, :], conv_state, lru_state[:, None, :], rope_c, rope_s,
                     W, T_in, L == 0)
    ma, sgb, q, ckv, kpe, kpe128, conv_new, lru_new = outs[:8]

    if L == 0:
        assert T_in == ATTN_BLOCK, "mixer tiles are the attention key blocks"
        k, vt = outs[8:]
        o = _attn_prompt(q, k, vt, ATTN_BLOCK, ATTN_HEADS_PER_STEP)
    else:
        ckv_all = jnp.concatenate([ckv_past, ckv], axis=1)
        kpe_past128 = jnp.pad(kpe_past, ((0, 0), (0, 0), (0, LANES - ROPE_DIM))).astype(BF16)
        kpe_all = jnp.concatenate([kpe_past128, kpe128], axis=1)
        LK = L + S
        k, v = _kv_up(ckv_all.reshape(B * LK, KV_LORA), kpe_all.reshape(B * LK, LANES), W["w_k_up"], W["w_v_up"],
                      LK, False)
        o = _attn_sample(q, k.reshape(B, LK, -1), v.reshape(B, LK, -1))

    N = B * S
    x1, h2, route, cnt = _merge(o.reshape(N, D_MODEL), ma.reshape(N, D_MODEL), sgb.reshape(N, D_MODEL),
                                x.reshape(N, D_MODEL), _chunk_rows(gt1, S), _chunk_rows(sh2, S),
                                _chunk_rows(sc2, S), W, T_tok)
    cnt = cnt.reshape(-1)
    y = _moe(h2, route, cnt, x1, _chunk_rows(gt2, S), W, MOE_TILE, T_tok, MOE_CHUNK, MOE_SCATTER_GROUP)
    return y.reshape(B, S, D_MODEL), ckv, kpe, conv_new, lru_new.reshape(B, D_RNN)


def kernel(x_prompt, x_sample, c_prompt, c_sample, cache_ckv, cache_kpe, state_conv, state_rglru, w_ada, b_ada, g_pre1, g_post1, g_pre2, g_post2, w_in, w_conv, b_conv, w_rgate, b_rgate, w_igate, b_igate, lru_lambda, w_rnn_out, g_q_lat, w_q_up, g_kv_lat, w_k_up, w_v_up, w_attn_out, w_out, w_group, b_group, w_erouter, b_erouter, w_exp_gate, w_exp_up, w_exp_down):
    assert w_in.shape[0] == 1, "single-layer trunk"
    B = x_prompt.shape[0]
    wi = w_in[0]
    sp = lambda a, b: wi[:, a:b]
    xr, gr = sp(0, D_RNN), sp(D_RNN, 2 * D_RNN)
    o = 2 * D_RNN
    ql, kvl, kr = sp(o, o + Q_LORA), sp(o + Q_LORA, o + Q_LORA + KV_LORA), \
        sp(o + Q_LORA + KV_LORA, o + Q_LORA + KV_LORA + ROPE_DIM)
    o = o + Q_LORA + KV_LORA + ROPE_DIM
    ga, gb = sp(o, o + D_MODEL), sp(o + D_MODEL, o + 2 * D_MODEL)
    wq = w_q_up[0].reshape(Q_LORA, N_HEADS, QK_NOPE + ROPE_DIM)
    wq_pe = wq[..., QK_NOPE:]
    row = lambda a: a[0].reshape(1, -1)
    W = {
        "g_pre1": row(g_pre1), "g_post1": row(g_post1), "g_pre2": row(g_pre2), "g_post2": row(g_post2),
        "w_in2": jnp.concatenate([xr, gr, ql, kvl, kr, _rotate_half_cols(kr), ga, gb], axis=1).astype(BF16),
        "w_conv": w_conv[0], "b_conv": row(b_conv),
        "w_gates": jnp.concatenate([w_rgate[0], w_igate[0]], axis=-1).astype(BF16),
        "b_rgate": row(b_rgate), "b_igate": row(b_igate), "lam": row(lru_lambda),
        "w_rnn_out": w_rnn_out[0].astype(BF16),
        "g_q": row(g_q_lat), "g_kv": row(g_kv_lat),
        "w_qup": jnp.concatenate([wq[..., :QK_NOPE], wq_pe, _rotate_half_cols(wq_pe)], axis=-1)
                 .reshape(Q_LORA, N_HEADS * HEAD_K).astype(BF16),
        "w_k_up": w_k_up[0].astype(BF16), "w_v_up": w_v_up[0].astype(BF16),
        "w_v_up_t": w_v_up[0].T.astype(BF16),
        "w_attn_out": w_attn_out[0].astype(BF16), "w_out": w_out[0].astype(BF16),
        "w_route": jnp.pad(jnp.concatenate([w_group[0], w_erouter[0]], axis=1).T,
                           ((0, ROUTE_ROWS - N_GROUPS - N_EXPERTS), (0, 0))).astype(BF16),
        "b_route": jnp.pad(jnp.concatenate([b_group[0], b_erouter[0]]), (0, ROUTE_ROWS - N_GROUPS - N_EXPERTS))
                   .reshape(ROUTE_ROWS, 1),
        "w_exp_gate": w_exp_gate[0].astype(BF16), "w_exp_up": w_exp_up[0].astype(BF16),
        "w_exp_down": w_exp_down[0].astype(BF16),
    }
    ada = _ada(jnp.concatenate([c_prompt, c_sample], axis=0), w_ada[0], b_ada[0])
    zeros_conv = jnp.zeros((B, CONV_W - 1, D_RNN), F32)
    zeros_lru = jnp.zeros((B, D_RNN), F32)
    yp, ckv_p, kpe_p, conv_p, lru_p = _layer(x_prompt, ada[:B], None, None, zeros_conv, zeros_lru, 0, W, 512, 512)
    ys, ckv_s, kpe_s, conv_s, lru_s = _layer(x_sample, ada[B:], cache_ckv[0], cache_kpe[0], state_conv[0],
                                             state_rglru[0], cache_ckv.shape[2], W, 64, 512)
    return (yp, ys, ckv_p[None], kpe_p[None], conv_p[None], lru_p[None],
            ckv_s[None], kpe_s[None], conv_s[None], lru_s[None])
```

```python
import functools

import jax
import jax.numpy as jnp
from jax import lax
from jax.experimental import pallas as pl
from jax.experimental.pallas import tpu as pltpu

F32 = jnp.float32
BF16 = jnp.bfloat16

D_MODEL = 1024
CHUNK = 64
D_RNN = 1024
N_RNN_BLOCKS = 8
RNN_BLOCK = D_RNN // N_RNN_BLOCKS
CONV_W = 4
LRU_C = 8.0
N_HEADS = 8
QK_NOPE = 128
ROPE_DIM = 64
V_HEAD = 128
Q_LORA = 384
KV_LORA = 256
ROPE_THETA = 10000.0
SM_SCALE = (QK_NOPE + ROPE_DIM) ** -0.5
LOG2E = 1.4426950408889634
Q_SCALE = SM_SCALE * LOG2E
N_GROUPS = 4
EXP_PER_GROUP = 4
N_EXPERTS = N_GROUPS * EXP_PER_GROUP
D_EXPERT = 512
EPS = 1e-6

LANES = 128
SUBLANES = 8
HEAD_K = QK_NOPE + 2 * ROPE_DIM
OFF_XR = 0
OFF_GR = OFF_XR + D_RNN
OFF_QL = OFF_GR + D_RNN
OFF_KVL = OFF_QL + Q_LORA
OFF_KR = OFF_KVL + KV_LORA
OFF_GA = OFF_KR + 2 * ROPE_DIM
OFF_GB = OFF_GA + D_MODEL
IN_COLS2 = OFF_GB + D_MODEL
ROUTE_OFF = N_GROUPS
ROUTE_ROWS = 32
CONV_PAD = 8
NEG = -1e30
GELU_C0 = 0.7978845608028654
GELU_C1 = GELU_C0 * 0.044715
ATTN_BLOCK = 512
MOE_TILE = 1024
KEY_STRIDE = 2048
MOE_SCATTER_GROUP = 4
MOE_CHUNK = 192
ATTN_HEADS_PER_STEP = 4
VMEM_LIMIT = 56 * 1024 * 1024


def _cparams(sem):
    return pltpu.CompilerParams(dimension_semantics=sem, vmem_limit_bytes=VMEM_LIMIT)


def _const_spec(shape):
    n = len(shape)
    return pl.BlockSpec(shape, lambda *_: (0,) * n, pipeline_mode=pl.Buffered(1))


def _rms(x, g):
    return x * lax.rsqrt(jnp.mean(x * x, axis=-1, keepdims=True) + EPS) * g


def _dot(a, b):
    return jnp.dot(a, b, preferred_element_type=F32)


def _sigmoid(x):
    return 0.5 * jnp.tanh(0.5 * x) + 0.5


def _gelu_tanh(x):
    hx = 0.5 * x
    return hx + hx * jnp.tanh(x * (GELU_C0 + GELU_C1 * (x * x)))


def _rope(v, c, s):
    return v * c + pltpu.roll(v, ROPE_DIM, axis=1) * s


def _ada_kernel(c_ref, w_ref, b_ref, o_ref):
    c = c_ref[...]
    s = c * jax.nn.sigmoid(c)
    o_ref[...] = jnp.dot(s, w_ref[...], preferred_element_type=F32,
                         precision=lax.Precision.HIGHEST) + b_ref[...]


def _ada(c, w, b):
    nb = c.shape[0]
    n = w.shape[1]
    bn = n // 6
    return pl.pallas_call(
        _ada_kernel,
        grid=(n // bn,),
        in_specs=[_const_spec((nb, D_MODEL)),
                  pl.BlockSpec((D_MODEL, bn), lambda j: (0, j)),
                  pl.BlockSpec((1, bn), lambda j: (0, j))],
        out_specs=pl.BlockSpec((nb, bn), lambda j: (0, j)),
        out_shape=jax.ShapeDtypeStruct((nb, n), F32),
        compiler_params=_cparams(("arbitrary",)),
        name="ada",
    )(c, w, b.reshape(1, n))


def _mixer_in_kernel(x_ref, sh_ref, sc_ref, gpre_ref, win_ref, cst_ref, lst_ref, wconv_ref, bconv_ref,
                     wgate_ref, br_ref, bi_ref, lam_ref, wrnn_ref, gq_ref, wqup_ref, gkv_ref,
                     rc_ref, rs_ref, *rest, T, emit_kv):
    if emit_kv:
        wk_ref, wvt_ref = rest[:2]
        rest = rest[2:]
    ma_ref, sgb_ref, q_ref, ckv_ref, kpe_ref, kpe128_ref, cout_ref, lout_ref = rest[:8]
    rest = rest[8:]
    if emit_kv:
        k_ref, vt_ref = rest[:2]
        rest = rest[2:]
    xbuf, b_scr, hcar = rest
    t = pl.program_id(1)

    @pl.when(t == 0)
    def _():
        xbuf[0:CONV_PAD, :] = jnp.zeros((CONV_PAD, D_RNN), F32)
        xbuf[CONV_PAD - (CONV_W - 1):CONV_PAD, :] = cst_ref[0]
        hcar[...] = lst_ref[0]

    x = x_ref[0]
    h = _rms(x, gpre_ref[...]) * (1.0 + sc_ref[0]) + sh_ref[0]
    hb = h.astype(BF16)

    groups = T // SUBLANES
    xbuf[CONV_PAD:CONV_PAD + T, :] = _dot(hb, win_ref[:, OFF_XR:OFF_XR + D_RNN])
    tail = xbuf[T + CONV_PAD - (CONV_W - 1):T + CONV_PAD, :]
    xall = xbuf[...].reshape(groups + 1, SUBLANES, D_RNN)
    row_wide = lax.broadcasted_iota(jnp.int32, (groups, SUBLANES, D_RNN), 1)
    xc = bconv_ref[...] + xall[1:] * wconv_ref[CONV_W - 1:CONV_W, :]
    for shift in range(1, CONV_W):
        rot = pltpu.roll(xall, shift, axis=1)
        shifted = jnp.where(row_wide >= shift, rot[1:], rot[:-1])
        xc = xc + shifted * wconv_ref[CONV_W - 1 - shift:CONV_W - shift, :]
    xc = xc.reshape(T, D_RNN)
    cout_ref[0] = tail
    xbuf[CONV_PAD - (CONV_W - 1):CONV_PAD, :] = tail

    lam = lam_ref[...]
    softplus_neg_lam = jnp.maximum(-lam, 0.0) + jnp.log1p(jnp.exp(-jnp.abs(lam)))
    row_in_group = lax.broadcasted_iota(jnp.int32, (groups, SUBLANES, RNN_BLOCK), 1)
    keeps = [row_in_group >= d for d in (1, 2, 4)]
    for n in range(N_RNN_BLOCKS):
        blk = slice(n * RNN_BLOCK, (n + 1) * RNN_BLOCK)
        xcb = xc[:, blk]
        g = _dot(xcb.astype(BF16), wgate_ref[n])
        r = _sigmoid(g[:, :RNN_BLOCK] + br_ref[:, blk])
        i = _sigmoid(g[:, RNN_BLOCK:] + bi_ref[:, blk])
        log_a = -LRU_C * r * softplus_neg_lam[:, blk]
        a = jnp.exp(log_a)
        z = -jnp.tanh(log_a) * (a * a + 1.0)
        b = jnp.where(z > 0.0, z * lax.rsqrt(z), 0.0) * (i * xcb)
        a = a.reshape(groups, SUBLANES, RNN_BLOCK)
        b = b.reshape(groups, SUBLANES, RNN_BLOCK)
        for keep, d in zip(keeps, (1, 2, 4)):
            a_prev = jnp.where(keep, pltpu.roll(a, d, axis=1), 1.0)
            b_prev = jnp.where(keep, pltpu.roll(b, d, axis=1), 0.0)
            b = b + a * b_prev
            a = a * a_prev
        hprev = hcar[:, blk]
        for grp in range(groups):
            rows = slice(grp * SUBLANES, (grp + 1) * SUBLANES)
            hg = b[grp] + a[grp] * hprev
            b_scr[rows, blk] = hg
            hprev = hg[SUBLANES - 1:SUBLANES, :]
        hcar[:, blk] = hprev
    lout_ref[0] = hcar[...]

    gr = _dot(hb, win_ref[:, OFF_GR:OFF_GR + D_RNN])
    y_a = _dot((b_scr[...] * _gelu_tanh(gr)).astype(BF16), wrnn_ref[...])
    ga = _dot(hb, win_ref[:, OFF_GA:OFF_GA + D_MODEL])
    ma_ref[0] = (_sigmoid(ga) * y_a).astype(BF16)
    gb = _dot(hb, win_ref[:, OFF_GB:OFF_GB + D_MODEL])
    sgb_ref[0] = _sigmoid(gb).astype(BF16)

    rc = rc_ref[...]
    rs = rs_ref[...]
    ql = _dot(hb, win_ref[:, OFF_QL:OFF_QL + Q_LORA])
    q = _dot(_rms(ql, gq_ref[...]).astype(BF16), wqup_ref[...])
    for hd in range(N_HEADS):
        base = hd * HEAD_K
        q_ref[0, :, base:base + QK_NOPE] = (q[:, base:base + QK_NOPE] * Q_SCALE).astype(BF16)
        pe = _rope(q[:, base + QK_NOPE:base + HEAD_K], rc, rs)
        q_ref[0, :, base + QK_NOPE:base + HEAD_K] = (pe * Q_SCALE).astype(BF16)
    kvl = _dot(hb, win_ref[:, OFF_KVL:OFF_KVL + KV_LORA])
    ckv = _rms(kvl, gkv_ref[...])
    ckv_ref[0] = ckv
    kp = _rope(_dot(hb, win_ref[:, OFF_KR:OFF_KR + 2 * ROPE_DIM]), rc, rs)
    kpe_ref[0] = kp[:, :ROPE_DIM]
    kpb = kp.astype(BF16)
    kpe128_ref[0] = kpb
    if emit_kv:
        _emit_kv(ckv.astype(BF16), kpb, wk_ref, wvt_ref, k_ref.at[0], vt_ref, True)


def _mixer_in(x, sh1, sc1, conv_state, lru_state, rope_c, rope_s, W, T, emit_kv):
    B, S, _ = x.shape
    nt = S // T
    kern = functools.partial(_mixer_in_kernel, T=T, emit_kv=emit_kv)
    bt = lambda b, t: (b, t, 0)
    bo = lambda b, t: (b, 0, 0)
    tt = lambda b, t: (t, 0)
    in_specs = [
        pl.BlockSpec((1, T, D_MODEL), bt),
        pl.BlockSpec((1, 1, D_MODEL), bo),
        pl.BlockSpec((1, 1, D_MODEL), bo),
        _const_spec((1, D_MODEL)),
        _const_spec((D_MODEL, IN_COLS2)),
        pl.BlockSpec((1, CONV_W - 1, D_RNN), bo),
        pl.BlockSpec((1, 1, D_RNN), bo),
        _const_spec((CONV_W, D_RNN)),
        _const_spec((1, D_RNN)),
        _const_spec((N_RNN_BLOCKS, RNN_BLOCK, 2 * RNN_BLOCK)),
        _const_spec((1, D_RNN)),
        _const_spec((1, D_RNN)),
        _const_spec((1, D_RNN)),
        _const_spec((D_RNN, D_MODEL)),
        _const_spec((1, Q_LORA)),
        _const_spec((Q_LORA, N_HEADS * HEAD_K)),
        _const_spec((1, KV_LORA)),
        pl.BlockSpec((T, LANES), tt),
        pl.BlockSpec((T, LANES), tt),
    ]
    out_specs = [
        pl.BlockSpec((1, T, D_MODEL), bt),
        pl.BlockSpec((1, T, D_MODEL), bt),
        pl.BlockSpec((1, T, N_HEADS * HEAD_K), bt),
        pl.BlockSpec((1, T, KV_LORA), bt),
        pl.BlockSpec((1, T, ROPE_DIM), bt),
        pl.BlockSpec((1, T, LANES), bt),
        pl.BlockSpec((1, CONV_W - 1, D_RNN), bo),
        pl.BlockSpec((1, 1, D_RNN), bo),
    ]
    out_shape = [
        jax.ShapeDtypeStruct((B, S, D_MODEL), BF16),
        jax.ShapeDtypeStruct((B, S, D_MODEL), BF16),
        jax.ShapeDtypeStruct((B, S, N_HEADS * HEAD_K), BF16),
        jax.ShapeDtypeStruct((B, S, KV_LORA), F32),
        jax.ShapeDtypeStruct((B, S, ROPE_DIM), F32),
        jax.ShapeDtypeStruct((B, S, LANES), BF16),
        jax.ShapeDtypeStruct((B, CONV_W - 1, D_RNN), F32),
        jax.ShapeDtypeStruct((B, 1, D_RNN), F32),
    ]
    scratch = [
        pltpu.VMEM((T + CONV_PAD, D_RNN), F32),
        pltpu.VMEM((T, D_RNN), F32),
        pltpu.VMEM((1, D_RNN), F32),
    ]
    args = [x, sh1, sc1, W["g_pre1"], W["w_in2"], conv_state, lru_state, W["w_conv"], W["b_conv"],
            W["w_gates"], W["b_rgate"], W["b_igate"], W["lam"], W["w_rnn_out"], W["g_q"], W["w_qup"],
            W["g_kv"], rope_c, rope_s]
    if emit_kv:
        args += [W["w_k_up"], W["w_v_up_t"]]
        in_specs += [_const_spec(W["w_k_up"].shape), _const_spec(W["w_v_up_t"].shape)]
        out_specs += [pl.BlockSpec((1, T, N_HEADS * HEAD_K), bt),
                      pl.BlockSpec((1, N_HEADS, V_HEAD, T), lambda b, t: (b * nt + t, 0, 0, 0))]
        out_shape += [jax.ShapeDtypeStruct((B, S, N_HEADS * HEAD_K), BF16),
                      jax.ShapeDtypeStruct((B * nt, N_HEADS, V_HEAD, T), BF16)]
    return pl.pallas_call(
        kern, grid=(B, nt), in_specs=in_specs, out_specs=out_specs, out_shape=out_shape,
        scratch_shapes=scratch, compiler_params=_cparams(("arbitrary", "arbitrary")),
        name="mixer_in",
    )(*args)


def _emit_kv(c, kpe, wk_ref, wv_ref, k_ref, v_ref, v_transposed):
    kn = _dot(c, wk_ref[...])
    for hd in range(N_HEADS):
        base = hd * HEAD_K
        k_ref[:, base:base + QK_NOPE] = kn[:, hd * QK_NOPE:(hd + 1) * QK_NOPE].astype(BF16)
        k_ref[:, base + QK_NOPE:base + HEAD_K] = kpe
    if v_transposed:
        vt = lax.dot_general(wv_ref[...], c, (((1,), (1,)), ((), ())), preferred_element_type=F32)
        for hd in range(N_HEADS):
            v_ref[0, hd] = vt[hd * V_HEAD:(hd + 1) * V_HEAD, :].astype(BF16)
    else:
        v_ref[...] = _dot(c, wv_ref[...]).astype(BF16)


def _kv_up_kernel(ckv_ref, kpe_ref, wk_ref, wv_ref, k_ref, v_ref, *, v_transposed):
    _emit_kv(ckv_ref[...].astype(BF16), kpe_ref[...], wk_ref, wv_ref, k_ref, v_ref, v_transposed)


def _kv_up(ckv, kpe128, wk, wv, T, v_transposed):
    R = ckv.shape[0]
    row = lambda i: (i, 0)
    if v_transposed:
        v_spec = pl.BlockSpec((1, N_HEADS, V_HEAD, T), lambda i: (i, 0, 0, 0))
        v_shape = jax.ShapeDtypeStruct((R // T, N_HEADS, V_HEAD, T), BF16)
    else:
        v_spec = pl.BlockSpec((T, N_HEADS * V_HEAD), row)
        v_shape = jax.ShapeDtypeStruct((R, N_HEADS * V_HEAD), BF16)
    return pl.pallas_call(
        functools.partial(_kv_up_kernel, v_transposed=v_transposed), grid=(R // T,),
        in_specs=[pl.BlockSpec((T, KV_LORA), row), pl.BlockSpec((T, LANES), row),
                  _const_spec(wk.shape), _const_spec(wv.shape)],
        out_specs=[pl.BlockSpec((T, N_HEADS * HEAD_K), row), v_spec],
        out_shape=[jax.ShapeDtypeStruct((R, N_HEADS * HEAD_K), BF16), v_shape],
        compiler_params=_cparams(("arbitrary",)),
        name="kv_up",
    )(ckv, kpe128, wk, wv)


def _attn_prompt_kernel(q_ref, k_ref, vt_ref, o_ref, qt_scr, sa, sb, xa, xb, m_scr, l_scr, acc_scr, *, QB, HP):
    qi = pl.program_id(2)
    s0, s1 = (sa, xa), (sb, xb)
    for hh in range(HP):
        qt_scr[hh] = q_ref[0, :, hh * HEAD_K:(hh + 1) * HEAD_K].T
    m_scr[...] = jnp.full(m_scr.shape, NEG, F32)
    l_scr[...] = jnp.zeros(l_scr.shape, F32)
    acc_scr[...] = jnp.zeros(acc_scr.shape, F32)

    def scores(j, dst):
        start = pl.multiple_of(j * QB, QB)
        for hh in range(HP):
            s = _dot(k_ref[0, pl.ds(start, QB), hh * HEAD_K:(hh + 1) * HEAD_K], qt_scr[hh])
            dst[0][hh] = s
            dst[1][hh] = jnp.max(s, axis=0, keepdims=True)

    def update(j, src, masked):
        for hh in range(HP):
            s = src[0][hh]
            if masked:
                ck = lax.broadcasted_iota(jnp.int32, (QB, QB), 0) // CHUNK
                cq = lax.broadcasted_iota(jnp.int32, (QB, QB), 1) // CHUNK
                s = jnp.where(ck <= cq, s, NEG)
                smax = jnp.max(s, axis=0, keepdims=True)
            else:
                smax = src[1][hh]
            m_old = m_scr[hh]
            m_new = jnp.maximum(m_old, smax)
            p = jnp.exp2(s - m_new)
            alpha = jnp.exp2(m_old - m_new)
            l_scr[hh] = alpha * l_scr[hh] + jnp.sum(p, axis=0, keepdims=True)
            acc_scr[hh] = alpha * acc_scr[hh] + _dot(vt_ref[j, hh], p.astype(BF16))
            m_scr[hh] = m_new

    scores(0, s0)

    def pair(jj, c):
        j = 2 * jj
        scores(j + 1, s1)
        update(j, s0, False)
        scores(j + 2, s0)
        update(j + 1, s1, False)
        return c

    lax.fori_loop(0, qi // 2, pair, 0)

    @pl.when(qi % 2 == 0)
    def _():
        update(qi, s0, True)

    @pl.when(qi % 2 == 1)
    def _():
        scores(qi, s1)
        update(qi - 1, s0, False)
        update(qi, s1, True)

    for hh in range(HP):
        o_ref[0, :, hh * V_HEAD:(hh + 1) * V_HEAD] = (acc_scr[hh] / l_scr[hh]).T.astype(BF16)


def _attn_prompt(q, k, vt, QB, HP):
    B, S, _ = q.shape
    nkb = S // QB
    kern = functools.partial(_attn_prompt_kernel, QB=QB, HP=HP)
    return pl.pallas_call(
        kern, grid=(B, N_HEADS // HP, S // QB),
        in_specs=[pl.BlockSpec((1, QB, HP * HEAD_K), lambda b, h, i: (b, i, h)),
                  pl.BlockSpec((1, S, HP * HEAD_K), lambda b, h, i: (b, 0, h), pipeline_mode=pl.Buffered(1)),
                  pl.BlockSpec((nkb, HP, V_HEAD, QB), lambda b, h, i: (b, h, 0, 0),
                               pipeline_mode=pl.Buffered(1))],
        out_specs=pl.BlockSpec((1, QB, HP * V_HEAD), lambda b, h, i: (b, i, h)),
        out_shape=jax.ShapeDtypeStruct((B, S, N_HEADS * V_HEAD), BF16),
        scratch_shapes=[pltpu.VMEM((HP, HEAD_K, QB), BF16),
                        pltpu.VMEM((HP, QB, QB), F32), pltpu.VMEM((HP, QB, QB), F32),
                        pltpu.VMEM((HP, 1, QB), F32), pltpu.VMEM((HP, 1, QB), F32),
                        pltpu.VMEM((HP, 1, QB), F32),
                        pltpu.VMEM((HP, 1, QB), F32), pltpu.VMEM((HP, V_HEAD, QB), F32)],
        compiler_params=_cparams(("arbitrary", "arbitrary", "arbitrary")),
        name="attn_prompt",
    )(q, k, vt)


def _attn_sample_kernel(q_ref, k_ref, v_ref, o_ref):
    for hd in range(N_HEADS):
        q = q_ref[0, :, hd * HEAD_K:(hd + 1) * HEAD_K]
        k = k_ref[0, :, hd * HEAD_K:(hd + 1) * HEAD_K]
        s = lax.dot_general(q, k, (((1,), (1,)), ((), ())), preferred_element_type=F32)
        p = jnp.exp2(s - jnp.max(s, axis=1, keepdims=True))
        l = jnp.sum(p, axis=1, keepdims=True)
        o = _dot(p.astype(BF16), v_ref[0, :, hd * V_HEAD:(hd + 1) * V_HEAD])
        o_ref[0, :, hd * V_HEAD:(hd + 1) * V_HEAD] = (o / l).astype(BF16)


def _attn_sample(q, k, v):
    B, S, _ = q.shape
    LK = k.shape[1]
    b3 = lambda b: (b, 0, 0)
    return pl.pallas_call(
        _attn_sample_kernel, grid=(B,),
        in_specs=[pl.BlockSpec((1, S, N_HEADS * HEAD_K), b3),
                  pl.BlockSpec((1, LK, N_HEADS * HEAD_K), b3),
                  pl.BlockSpec((1, LK, N_HEADS * V_HEAD), b3)],
        out_specs=pl.BlockSpec((1, S, N_HEADS * V_HEAD), b3),
        out_shape=jax.ShapeDtypeStruct((B, S, N_HEADS * V_HEAD), BF16),
        compiler_params=_cparams(("arbitrary",)),
        name="attn_sample",
    )(q, k, v)


def _first_max(rows):
    best = rows[0]
    for v in rows[1:]:
        best = jnp.maximum(best, v)
    idx = jnp.full(best.shape, len(rows) - 1, jnp.int32)
    for j in range(len(rows) - 2, -1, -1):
        idx = jnp.where(rows[j] == best, j, idx)
    return best, idx


def _merge_kernel(o_ref, ma_ref, sgb_ref, x_ref, gt1_ref, sh2_ref, sc2_ref, gpost1_ref, gpre2_ref,
                  wao_ref, wout_ref, wr_ref, br_ref, x1_ref, h2_ref, route_ref, cnt_ref, *, T):
    nc = T // CHUNK
    y_b = _dot(o_ref[...], wao_ref[...])
    m = ma_ref[...].astype(F32) + sgb_ref[...].astype(F32) * y_b
    y = _dot(m.astype(BF16), wout_ref[...])
    yn = _rms(y, gpost1_ref[...]).reshape(nc, CHUNK, D_MODEL)
    x1 = x_ref[...].reshape(nc, CHUNK, D_MODEL) + gt1_ref[...] * yn
    x1_ref[...] = x1.reshape(T, D_MODEL)
    h2 = (_rms(x1, gpre2_ref[...]) * (1.0 + sc2_ref[...]) + sh2_ref[...]).reshape(T, D_MODEL)
    h2b = h2.astype(BF16)
    h2_ref[...] = h2b

    logits = lax.dot_general(wr_ref[...], h2b, (((1,), (1,)), ((), ())), preferred_element_type=F32)
    logits = logits + br_ref[...]
    row = lambda r: logits[r:r + 1, :]
    gmax, gidx = _first_max([row(g) for g in range(N_GROUPS)])
    gsum = jnp.exp(row(0) - gmax)
    for g in range(1, N_GROUPS):
        gsum = gsum + jnp.exp(row(g) - gmax)
    sel = []
    for j in range(EXP_PER_GROUP):
        v = row(ROUTE_OFF + (N_GROUPS - 1) * EXP_PER_GROUP + j)
        for g in range(N_GROUPS - 2, -1, -1):
            v = jnp.where(gidx == g, row(ROUTE_OFF + g * EXP_PER_GROUP + j), v)
        sel.append(v)
    m1, j1 = _first_max(sel)
    m2, j2 = _first_max([jnp.where(j1 == j, NEG, sel[j]) for j in range(EXP_PER_GROUP)])
    e1 = gidx * EXP_PER_GROUP + j1
    e2 = gidx * EXP_PER_GROUP + j2
    ex = jnp.exp(m2 - m1)
    w1 = 1.0 / (gsum * (1.0 + ex))
    w2 = w1 * ex
    erow = lax.broadcasted_iota(jnp.int32, (N_EXPERTS, T), 0)
    uses = jnp.where((erow == e1) | (erow == e2), 1.0, 0.0)
    later = lax.broadcasted_iota(jnp.int32, (T, T), 0) < lax.broadcasted_iota(jnp.int32, (T, T), 1)
    rank = _dot(uses.astype(BF16), jnp.where(later, 1.0, 0.0).astype(BF16))
    cnt_ref[0] = jnp.sum(uses, axis=1, keepdims=True).astype(jnp.int32)
    rank1 = jnp.sum(jnp.where(erow == e1, rank, 0.0), axis=0, keepdims=True)
    rank2 = jnp.sum(jnp.where(erow == e2, rank, 0.0), axis=0, keepdims=True)
    fields = ((e1 + ROUTE_OFF).astype(F32), (e2 + ROUTE_OFF).astype(F32), w1, w2, rank1, rank2)
    frow = lax.broadcasted_iota(jnp.int32, (SUBLANES, T), 0)
    out = jnp.zeros((SUBLANES, T), F32)
    for f, val in enumerate(fields):
        out = jnp.where(frow == f, val, out)
    route_ref[...] = out


def _merge(o, ma, sgb, x, gt1c, sh2c, sc2c, W, T):
    N = x.shape[0]
    nc = T // CHUNK
    row = lambda i: (i, 0)
    chunk = lambda i: (i, 0, 0)
    kern = functools.partial(_merge_kernel, T=T)
    return pl.pallas_call(
        kern, grid=(N // T,),
        in_specs=[pl.BlockSpec((T, D_MODEL), row), pl.BlockSpec((T, D_MODEL), row),
                  pl.BlockSpec((T, D_MODEL), row), pl.BlockSpec((T, D_MODEL), row),
                  pl.BlockSpec((nc, 1, D_MODEL), chunk), pl.BlockSpec((nc, 1, D_MODEL), chunk),
                  pl.BlockSpec((nc, 1, D_MODEL), chunk),
                  _const_spec((1, D_MODEL)), _const_spec((1, D_MODEL)),
                  _const_spec((D_MODEL, D_MODEL)), _const_spec((D_MODEL, D_MODEL)),
                  _const_spec((ROUTE_ROWS, D_MODEL)), _const_spec((ROUTE_ROWS, 1))],
        out_specs=[pl.BlockSpec((T, D_MODEL), row), pl.BlockSpec((T, D_MODEL), row),
                   pl.BlockSpec((SUBLANES, T), lambda i: (0, i)), pl.BlockSpec((1, N_EXPERTS, 1), chunk)],
        out_shape=[jax.ShapeDtypeStruct((N, D_MODEL), F32), jax.ShapeDtypeStruct((N, D_MODEL), BF16),
                   jax.ShapeDtypeStruct((SUBLANES, N), F32),
                   jax.ShapeDtypeStruct((N // T, N_EXPERTS, 1), jnp.int32)],
        compiler_params=_cparams(("arbitrary",)),
        name="merge",
    )(o, ma, sgb, x, gt1c, sh2c, sc2c, W["g_post1"], W["g_pre2"], W["w_attn_out"], W["w_out"],
      W["w_route"], W["b_route"])


def _moe_kernel(cnt_ref, h_ref, route_ref, x1_ref, gt2_ref, gpost2_ref, wg_ref, wu_ref, wd_ref, y_ref,
                key_l, key_s, comb_w, parked, *, T, TR, CH, SG):
    i = pl.program_id(0)
    e = pl.program_id(1)
    nc = T // CHUNK
    nr = T // TR

    def keys(sel, rank, tok):
        expert = sel.astype(jnp.int32) - ROUTE_OFF
        rank = rank.astype(jnp.int32)
        for r in range(1, nr):
            before = jnp.zeros_like(rank)
            for x in range(N_EXPERTS):
                before = jnp.where(expert == x, cnt_ref[(i * nr + r - 1) * N_EXPERTS + x], before)
            rank = rank + jnp.where(tok >= r * TR, before, 0)
        return expert * KEY_STRIDE + rank

    @pl.when(e == 0)
    def _():
        y_ref[...] = jnp.zeros(y_ref.shape, F32)
        rt = route_ref[...]
        tok_l = lax.broadcasted_iota(jnp.int32, (1, T), 1)
        k1 = keys(rt[0:1, :], rt[4:5, :], tok_l)
        k2 = keys(rt[1:2, :], rt[5:6, :], tok_l)
        key_l[0:1, :] = k1
        key_l[1:2, :] = k2
        fields = (k1.astype(F32), k2.astype(F32), rt[0:1, :] - ROUTE_OFF, rt[2:3, :], rt[3:4, :])
        frow = lax.broadcasted_iota(jnp.int32, (SUBLANES, T), 0)
        rows = jnp.zeros((SUBLANES, T), F32)
        for f, val in enumerate(fields):
            rows = jnp.where(frow == f, val, rows)
        cols = jnp.concatenate([rows, jnp.zeros((LANES - SUBLANES, T), F32)], axis=0).T
        for f in range(3):
            key_s[f] = cols[:, f:f + 1].astype(jnp.int32)
        comb_w[0] = cols[:, 3:4]
        comb_w[1] = cols[:, 4:5]

    k1_l, k2_l = key_l[0:1, :], key_l[1:2, :]
    k1_s, k2_s = key_s[0], key_s[1]
    total = cnt_ref[(i * nr) * N_EXPERTS + e]
    for r in range(1, nr):
        total = total + cnt_ref[(i * nr + r) * N_EXPERTS + e]

    def expert_rows(c):
        want_s = e * KEY_STRIDE + c * CH + lax.broadcasted_iota(jnp.int32, (CH, 1), 0)
        pick = (k1_l == want_s) | (k2_l == want_s)
        xg = _dot(jnp.where(pick, 1.0, 0.0).astype(BF16), h_ref[...]).astype(BF16)
        g = _dot(xg, wg_ref[0])
        u = _dot(xg, wu_ref[0])
        hid = (g * jax.nn.sigmoid(g) * u).astype(BF16)
        return pick, _dot(hid, wd_ref[0])

    pick, out = expert_rows(0)
    rt = route_ref[...]
    comb_l = jnp.where(rt[0:1, :].astype(jnp.int32) - ROUTE_OFF == e, rt[2:3, :], rt[3:4, :])
    comb_c = jnp.sum(jnp.where(pick, comb_l, 0.0), axis=1, keepdims=True)
    slot = e % SG
    parked[pl.ds(pl.multiple_of(slot * CH, CH), CH), :] = (out * comb_c).astype(BF16)

    @pl.when(slot == SG - 1)
    def _():
        col = lax.broadcasted_iota(jnp.int32, (1, SG * CH), 1)
        which = jnp.zeros_like(col)
        for x in range(1, SG):
            which = which + jnp.where(col >= x * CH, 1, 0)
        want_l = (e - (SG - 1) + which) * KEY_STRIDE + col - which * CH
        put = (k1_s == want_l) | (k2_s == want_l)
        y_ref[...] += _dot(jnp.where(put, 1.0, 0.0).astype(BF16), parked[...])

    def overflow(c, carry):
        _, out = expert_rows(c)
        want_l = e * KEY_STRIDE + c * CH + lax.broadcasted_iota(jnp.int32, (1, CH), 1)
        put = (k1_s == want_l) | (k2_s == want_l)
        comb_s = jnp.where(key_s[2] == e, comb_w[0], comb_w[1])
        y_ref[...] += comb_s * _dot(jnp.where(put, 1.0, 0.0).astype(BF16), out.astype(BF16))
        return carry

    lax.fori_loop(1, (total + CH - 1) // CH, overflow, 0)

    @pl.when(e == N_EXPERTS - 1)
    def _():
        on = _rms(y_ref[...], gpost2_ref[...]).reshape(nc, CHUNK, D_MODEL)
        y = x1_ref[...].reshape(nc, CHUNK, D_MODEL) + gt2_ref[...] * on
        y_ref[...] = y.reshape(T, D_MODEL)


def _moe(h2, route, cnt, x1, gt2c, W, T, TR, CH, SG):
    N = h2.shape[0]
    nc = T // CHUNK
    row = lambda i, e, c: (i, 0)
    wsel = lambda i, e, c: (e, 0, 0)
    assert N_EXPERTS % SG == 0
    kern = functools.partial(_moe_kernel, T=T, TR=TR, CH=CH, SG=SG)
    grid_spec = pltpu.PrefetchScalarGridSpec(
        num_scalar_prefetch=1, grid=(N // T, N_EXPERTS),
        in_specs=[pl.BlockSpec((T, D_MODEL), row), pl.BlockSpec((SUBLANES, T), lambda i, e, c: (0, i)),
                  pl.BlockSpec((T, D_MODEL), row),
                  pl.BlockSpec((nc, 1, D_MODEL), lambda i, e, c: (i, 0, 0)),
                  pl.BlockSpec((1, D_MODEL), lambda i, e, c: (0, 0)),
                  pl.BlockSpec((1, D_MODEL, D_EXPERT), wsel),
                  pl.BlockSpec((1, D_MODEL, D_EXPERT), wsel),
                  pl.BlockSpec((1, D_EXPERT, D_MODEL), wsel)],
        out_specs=pl.BlockSpec((T, D_MODEL), row),
        scratch_shapes=[pltpu.VMEM((8, T), jnp.int32), pltpu.VMEM((3, T, 1), jnp.int32),
                        pltpu.VMEM((2, T, 1), F32), pltpu.VMEM((SG * CH, D_MODEL), BF16)])
    return pl.pallas_call(
        kern, grid_spec=grid_spec,
        out_shape=jax.ShapeDtypeStruct((N, D_MODEL), F32),
        compiler_params=_cparams(("arbitrary", "arbitrary")),
        name="moe",
    )(cnt, h2, route, x1, gt2c, W["g_post2"], W["w_exp_gate"], W["w_exp_up"], W["w_exp_down"])


def _rotate_half_cols(w):
    half = ROPE_DIM // 2
    return jnp.concatenate([-w[..., half:], w[..., :half]], axis=-1)


def _rope_tables(pos):
    inv = ROPE_THETA ** (-jnp.arange(0, ROPE_DIM, 2, dtype=F32) / ROPE_DIM)
    ang = pos.astype(F32)[:, None] * inv
    z = jnp.zeros((pos.shape[0], LANES - ROPE_DIM), F32)
    c, s = jnp.cos(ang), jnp.sin(ang)
    return jnp.concatenate([c, c, z], axis=1), jnp.concatenate([s, s, z], axis=1)


def _chunk_rows(v, seq):
    B, D = v.shape
    return jnp.broadcast_to(v[:, None, None, :], (B, seq // CHUNK, 1, D)).reshape(B * (seq // CHUNK), 1, D)


def _layer(x, ada, ckv_past, kpe_past, conv_state, lru_state, pos0, W, T_in, T_tok):
    B, S, _ = x.shape
    L = 0 if ckv_past is None else ckv_past.shape[1]
    sh1, sc1, gt1, sh2, sc2, gt2 = jnp.split(ada, 6, axis=-1)
    rope_c, rope_s = _rope_tables(pos0 + jnp.arange(S))
    outs = _mixer_in(x, sh1[:, None, :], sc1[:, None, :], conv_state, lru_state[:, None, :], rope_c, rope_s,
                     W, T_in, L == 0)
    ma, sgb, q, ckv, kpe, kpe128, conv_new, lru_new = outs[:8]

    if L == 0:
        assert T_in == ATTN_BLOCK, "mixer tiles are the attention key blocks"
        k, vt = outs[8:]
        o = _attn_prompt(q, k, vt, ATTN_BLOCK, ATTN_HEADS_PER_STEP)
    else:
        ckv_all = jnp.concatenate([ckv_past, ckv], axis=1)
        kpe_past128 = jnp.pad(kpe_past, ((0, 0), (0, 0), (0, LANES - ROPE_DIM))).astype(BF16)
        kpe_all = jnp.concatenate([kpe_past128, kpe128], axis=1)
        LK = L + S
        k, v = _kv_up(ckv_all.reshape(B * LK, KV_LORA), kpe_all.reshape(B * LK, LANES), W["w_k_up"], W["w_v_up"],
                      LK, False)
        o = _attn_sample(q, k.reshape(B, LK, -1), v.reshape(B, LK, -1))

    N = B * S
    x1, h2, route, cnt = _merge(o.reshape(N, D_MODEL), ma.reshape(N, D_MODEL), sgb.reshape(N, D_MODEL),
                                x.reshape(N, D_MODEL), _chunk_rows(gt1, S), _chunk_rows(sh2, S),
                                _chunk_rows(sc2, S), W, T_tok)
    cnt = cnt.reshape(-1)
    y = _moe(h2, route, cnt, x1, _chunk_rows(gt2, S), W, MOE_TILE, T_tok, MOE_CHUNK, MOE_SCATTER_GROUP)
    return y.reshape(B, S, D_MODEL), ckv, kpe, conv_new, lru_new.reshape(B, D_RNN)


def kernel(x_prompt, x_sample, c_prompt, c_sample, cache_ckv, cache_kpe, state_conv, state_rglru, w_ada, b_ada, g_pre1, g_post1, g_pre2, g_post2, w_in, w_conv, b_conv, w_rgate, b_rgate, w_igate, b_igate, lru_lambda, w_rnn_out, g_q_lat, w_q_up, g_kv_lat, w_k_up, w_v_up, w_attn_out, w_out, w_group, b_group, w_erouter, b_erouter, w_exp_gate, w_exp_up, w_exp_down):
    assert w_in.shape[0] == 1, "single-layer trunk"
    B = x_prompt.shape[0]
    wi = w_in[0]
    sp = lambda a, b: wi[:, a:b]
    xr, gr = sp(0, D_RNN), sp(D_RNN, 2 * D_RNN)
    o = 2 * D_RNN
    ql, kvl, kr = sp(o, o + Q_LORA), sp(o + Q_LORA, o + Q_LORA + KV_LORA), \
        sp(o + Q_LORA + KV_LORA, o + Q_LORA + KV_LORA + ROPE_DIM)
    o = o + Q_LORA + KV_LORA + ROPE_DIM
    ga, gb = sp(o, o + D_MODEL), sp(o + D_MODEL, o + 2 * D_MODEL)
    wq = w_q_up[0].reshape(Q_LORA, N_HEADS, QK_NOPE + ROPE_DIM)
    wq_pe = wq[..., QK_NOPE:]
    row = lambda a: a[0].reshape(1, -1)
    W = {
        "g_pre1": row(g_pre1), "g_post1": row(g_post1), "g_pre2": row(g_pre2), "g_post2": row(g_post2),
        "w_in2": jnp.concatenate([xr, gr, ql, kvl, kr, _rotate_half_cols(kr), ga, gb], axis=1).astype(BF16),
        "w_conv": w_conv[0], "b_conv": row(b_conv),
        "w_gates": jnp.concatenate([w_rgate[0], w_igate[0]], axis=-1).astype(BF16),
        "b_rgate": row(b_rgate), "b_igate": row(b_igate), "lam": row(lru_lambda),
        "w_rnn_out": w_rnn_out[0].astype(BF16),
        "g_q": row(g_q_lat), "g_kv": row(g_kv_lat),
        "w_qup": jnp.concatenate([wq[..., :QK_NOPE], wq_pe, _rotate_half_cols(wq_pe)], axis=-1)
                 .reshape(Q_LORA, N_HEADS * HEAD_K).astype(BF16),
        "w_k_up": w_k_up[0].astype(BF16), "w_v_up": w_v_up[0].astype(BF16),
        "w_v_up_t": w_v_up[0].T.astype(BF16),
        "w_attn_out": w_attn_out[0].astype(BF16), "w_out": w_out[0].astype(BF16),
        "w_route": jnp.pad(jnp.concatenate([w_group[0], w_erouter[0]], axis=1).T,
                           ((0, ROUTE_ROWS - N_GROUPS - N_EXPERTS), (0, 0))).astype(BF16),
        "b_route": jnp.pad(jnp.concatenate([b_group[0], b_erouter[0]]), (0, ROUTE_ROWS - N_GROUPS - N_EXPERTS))
                   .reshape(ROUTE_ROWS, 1),
        "w_exp_gate": w_exp_gate[0].astype(BF16), "w_exp_up": w_exp_up[0].astype(BF16),
        "w_exp_down": w_exp_down[0].astype(BF16),
    }
    ada = _ada(jnp.concatenate([c_prompt, c_sample], axis=0), w_ada[0], b_ada[0])
    zeros_conv = jnp.zeros((B, CONV_W - 1, D_RNN), F32)
    zeros_lru = jnp.zeros((B, D_RNN), F32)
    yp, ckv_p, kpe_p, conv_p, lru_p = _layer(x_prompt, ada[:B], None, None, zeros_conv, zeros_lru, 0, W, 512, 512)
    ys, ckv_s, kpe_s, conv_s, lru_s = _layer(x_sample, ada[B:], cache_ckv[0], cache_kpe[0], state_conv[0],
                                             state_rglru[0], cache_ckv.shape[2], W, 64, 512)
    return (yp, ys, ckv_p[None], kpe_p[None], conv_p[None], lru_p[None],
            ckv_s[None], kpe_s[None], conv_s[None], lru_s[None])
```
